```python
import math
import jax, jax.numpy as jnp
from jax import lax
import numpy as np

D_MODEL = 1024
BATCH = 8
SEQ = 2048
DEPTH = 4
DEC_BATCH = 128
DEC_SEQ = 1
PAST_LEN = 16384
PAGE_SIZE = 128

N_AB = (DEPTH + 1) // 2
N_C = DEPTH // 2
CONV_W = 4
W_A = D_MODEL
H_A = 8
BW_A = W_A // H_A
LRU_C = 8.0
H_B = 8
DK = 128
DV = 128
W_B = H_B * DV
QKV_B = 2 * H_B * DK + H_B * DV
CHUNK = 64
IN_AB = 2 * W_A + QKV_B + W_B + 2 * H_B
OUT_AB = W_A + W_B
W_C = D_MODEL
CG = 16
G_C = W_C // CG
P_C = 64
EPS = 1e-6

kernel_name = "hybrid_rglru_gdn_s5_step"


def rms_norm(x, w):
    x32 = x.astype(jnp.float32)
    y = x32 * lax.rsqrt(jnp.mean(x32 * x32, axis=-1, keepdims=True) + EPS)
    return (y * w.astype(jnp.float32)).astype(x.dtype)


def l2_norm(t):
    return t * lax.rsqrt(jnp.sum(t * t, axis=-1, keepdims=True) + EPS)


def causal_conv(x, buf, w):
    L = x.shape[1]
    xp = jnp.concatenate([buf.astype(x.dtype), x], axis=1)
    y = sum(xp[:, j:j + L] * w[j] for j in range(CONV_W))
    return y, xp[:, L:]


def linear_combine(e1, e2):
    a1, b1 = e1
    a2, b2 = e2
    return a1 * a2, a2 * b1 + b2


def rg_lru(x, gx_w, gx_b, ga_w, ga_b, a_param, h0, start_pos):
    B, L, _ = x.shape
    xb = x.reshape(B, L, H_A, BW_A)
    gate_x = jax.nn.sigmoid(jnp.einsum("blhi,hij->blhj", xb, gx_w).reshape(B, L, W_A) + gx_b)
    gate_a = jax.nn.sigmoid(jnp.einsum("blhi,hij->blhj", xb, ga_w).reshape(B, L, W_A) + ga_b)
    log_a = -LRU_C * gate_a * jax.nn.softplus(-a_param)
    a = jnp.exp(log_a)
    mult = jnp.sqrt(-jnp.expm1(2.0 * log_a))
    if start_pos == 0:
        mult = mult.at[:, 0].set(1.0)
    b = mult * gate_x * x
    b = b.at[:, 0].add(a[:, 0] * h0)
    _, h = lax.associative_scan(linear_combine, (a, b), axis=1)
    return h, h[:, -1]


def gated_delta_rule(q, k, v, g, beta, s0):
    B, L, H, _ = q.shape
    n = -(-L // CHUNK)
    pad = n * CHUNK - L

    def blocks(t):
        t = jnp.pad(t, [(0, 0), (0, pad)] + [(0, 0)] * (t.ndim - 2))
        t = t.reshape((B, n, CHUNK) + t.shape[2:])
        return jnp.moveaxis(t, (1, 2), (0, 3))

    qb, kb, vb, gb, bb = blocks(q), blocks(k), blocks(v), blocks(g), blocks(beta)
    gc = jnp.cumsum(gb, axis=-1)
    idx = jnp.arange(CHUNK)
    tri = idx[:, None] >= idx[None, :]
    strict = idx[:, None] > idx[None, :]
    decay = jnp.exp(jnp.where(tri, gc[..., :, None] - gc[..., None, :], -jnp.inf))
    k_beta = kb * bb[..., None]
    A = jnp.where(strict, jnp.einsum("nbhid,nbhjd->nbhij", k_beta, kb) * decay, 0.0)
    eye = jnp.eye(CHUNK, dtype=A.dtype)
    rhs = jnp.concatenate([vb * bb[..., None], k_beta * jnp.exp(gc)[..., None]], axis=-1)
    sol = lax.linalg.triangular_solve(A + eye, rhs, left_side=True, lower=True)
    u, w = sol[..., :DV], sol[..., DV:]
    qk = jnp.where(tri, jnp.einsum("nbhid,nbhjd->nbhij", qb, kb) * decay, 0.0)

    def step(S, inp):
        qc, kc, uc, wc, gcc, qkc = inp
        v_new = uc - jnp.einsum("bhck,bhkv->bhcv", wc, S)
        o = (jnp.einsum("bhck,bhkv->bhcv", qc * jnp.exp(gcc)[..., None], S)
             + jnp.einsum("bhij,bhjv->bhiv", qkc, v_new))
        g_last = gcc[..., -1:]
        S = (S * jnp.exp(g_last)[..., None]
             + jnp.einsum("bhck,bhcv->bhkv", kc * jnp.exp(g_last - gcc)[..., None], v_new))
        return S, o

    S, o = lax.scan(step, s0, (qb, kb, u, w, gc, qk))
    o = jnp.moveaxis(o, (0, 3), (1, 2)).reshape(B, n * CHUNK, H, DV)[:, :L]
    return o, S


def ab_mixer(h, conv_a, lru_h, conv_b, delta_s, start_pos, in_w, out_w, conv_a_w, conv_a_b,
             gx_w, gx_b, ga_w, ga_b, a_param, conv_b_w, a_log, dt_bias, gnorm_w):
    B, L, _ = h.shape
    f32 = jnp.float32
    proj = jnp.dot(h, in_w)
    cuts = np.cumsum([W_A, W_A, QKV_B, W_B, H_B]).tolist()
    xa, za, qkv, zb, b_raw, a_raw = jnp.split(proj, cuts, axis=-1)
    xa_c, new_conv_a = causal_conv(xa, conv_a, conv_a_w)
    xa_c = (xa_c + conv_a_b).astype(f32)
    ya, new_lru = rg_lru(xa_c, gx_w.astype(f32), gx_b.astype(f32), ga_w.astype(f32),
                         ga_b.astype(f32), a_param.astype(f32), lru_h.astype(f32), start_pos)
    ya = ya * jax.nn.silu(za.astype(f32))
    qkv_c, new_conv_b = causal_conv(qkv, conv_b, conv_b_w)
    qkv_c = jax.nn.silu(qkv_c.astype(f32))
    q, k, v = jnp.split(qkv_c, [H_B * DK, 2 * H_B * DK], axis=-1)
    q = l2_norm(q.reshape(B, L, H_B, DK)) * (DK ** -0.5)
    k = l2_norm(k.reshape(B, L, H_B, DK))
    v = v.reshape(B, L, H_B, DV)
    beta = jax.nn.sigmoid(b_raw.astype(f32))
    g = -jnp.exp(a_log.astype(f32)) * jax.nn.softplus(a_raw.astype(f32) + dt_bias.astype(f32))
    o, new_delta = gated_delta_rule(q, k, v, g, beta, delta_s.astype(f32))
    o = rms_norm(o, gnorm_w) * jax.nn.silu(zb.astype(f32).reshape(B, L, H_B, DV))
    y = jnp.concatenate([ya, o.reshape(B, L, W_B)], axis=-1).astype(h.dtype)
    return jnp.dot(y, out_w), new_conv_a, new_lru, new_conv_b, new_delta


def s5_mixer(h, s_re, s_im, in_w, out_w, a_re, a_im, b_re, b_im, c_re, c_im, d, log_dt,
             glu_w, glu_b):
    B, L, _ = h.shape
    f32 = jnp.float32
    u, z = jnp.split(jnp.dot(h, in_w), 2, axis=-1)
    u32 = u.astype(f32)
    A = lax.complex(a_re.astype(f32), a_im.astype(f32))
    dt = jnp.exp(log_dt.astype(f32))[:, None]
    A_bar = jnp.exp(A * dt)
    B_bar = ((A_bar - 1.0) / A)[..., None] * lax.complex(b_re.astype(f32), b_im.astype(f32))
    Bu = jnp.einsum("gpc,blgc->lbgp", B_bar,
                    u32.reshape(B, L, G_C, CG).astype(jnp.complex64))
    x0 = lax.complex(s_re.astype(f32), s_im.astype(f32))
    Bu = Bu.at[0].add(A_bar * x0)
    a_el = jnp.broadcast_to(A_bar, (L, 1, G_C, P_C))
    _, xs = lax.associative_scan(linear_combine, (a_el, Bu), axis=0)
    C = lax.complex(c_re.astype(f32), c_im.astype(f32))
    y = jnp.einsum("gcp,lbgp->blgc", C, xs).real.reshape(B, L, W_C) + d.astype(f32) * u32
    y = jax.nn.gelu(y)
    y = y * jax.nn.sigmoid(jnp.dot(y, glu_w.astype(f32)) + glu_b.astype(f32))
    y = (y * jax.nn.silu(z.astype(f32))).astype(h.dtype)
    return jnp.dot(y, out_w), xs[-1].real, xs[-1].imag


def trunk(x, c, conv_a, lru_h, conv_b, delta_s, s5_re, s5_im, start_pos, w):
    n_ca, n_lru, n_cb, n_ds, n_sr, n_si = [], [], [], [], [], []
    for i in range(DEPTH):
        j = i // 2
        mod = jnp.dot(jax.nn.silu(c), w["mod_w"][i]) + w["mod_b"][i]
        shift, scale, gate = jnp.split(mod[:, None, :], 3, axis=-1)
        h = rms_norm(x, w["norm_w"][i]) * (1.0 + scale) + shift
        if i % 2 == 0:
            y, ca, lh, cb, ds = ab_mixer(
                h, conv_a[j], lru_h[j], conv_b[j], delta_s[j], start_pos,
                w["ab_in_w"][j], w["ab_out_w"][j], w["conv_a_w"][j], w["conv_a_b"][j],
                w["lru_gx_w"][j], w["lru_gx_b"][j], w["lru_ga_w"][j], w["lru_ga_b"][j],
                w["lru_a_param"][j], w["conv_b_w"][j], w["gdn_a_log"][j],
                w["gdn_dt_bias"][j], w["gdn_norm_w"][j])
            n_ca.append(ca.astype(conv_a.dtype))
            n_lru.append(lh.astype(lru_h.dtype))
            n_cb.append(cb.astype(conv_b.dtype))
            n_ds.append(ds.astype(delta_s.dtype))
        else:
            y, sr, si = s5_mixer(
                h, s5_re[j], s5_im[j], w["c_in_w"][j], w["c_out_w"][j],
                w["s5_a_re"][j], w["s5_a_im"][j], w["s5_b_re"][j], w["s5_b_im"][j],
                w["s5_c_re"][j], w["s5_c_im"][j], w["s5_d"][j], w["s5_log_dt"][j],
                w["glu_w"][j], w["glu_b"][j])
            n_sr.append(sr.astype(s5_re.dtype))
            n_si.append(si.astype(s5_im.dtype))
        x = x + (gate * y).astype(x.dtype)
    y = rms_norm(x, w["final_norm_w"])
    return (y, jnp.stack(n_ca), jnp.stack(n_lru), jnp.stack(n_cb), jnp.stack(n_ds),
            jnp.stack(n_sr), jnp.stack(n_si))


def setup_inputs(seed: int = 0) -> dict:
    key = jax.random.key(seed)
    ks = list(jax.random.split(key, 48))
    f32 = jnp.float32

    def nrm(shape, s):
        return jax.random.normal(ks.pop(), shape, f32) * s

    def unif(shape, lo, hi):
        return jax.random.uniform(ks.pop(), shape, f32, lo, hi)

    a8 = unif((N_AB, W_A), 0.9, 0.999)
    base = a8 ** (1.0 / LRU_C)
    dt_b = jnp.exp(unif((N_AB, H_B), math.log(1e-3), math.log(1e-1)))
    return {
        "x_prompt": nrm((BATCH, SEQ, D_MODEL), 1.0),
        "x_sample": nrm((DEC_BATCH, DEC_SEQ, D_MODEL), 1.0),
        "c_prompt": nrm((BATCH, D_MODEL), 1.0),
        "c_sample": nrm((DEC_BATCH, D_MODEL), 1.0),
        "state_conv_a": nrm((N_AB, DEC_BATCH, CONV_W - 1, W_A), 1.0),
        "state_lru": nrm((N_AB, DEC_BATCH, W_A), 0.5),
        "state_conv_b": nrm((N_AB, DEC_BATCH, CONV_W - 1, QKV_B), 1.0),
        "state_delta": nrm((N_AB, DEC_BATCH, H_B, DK, DV), 0.1),
        "state_s5_re": nrm((N_C, DEC_BATCH, G_C, P_C), 0.1),
        "state_s5_im": nrm((N_C, DEC_BATCH, G_C, P_C), 0.1),
        "norm_w": 1.0 + nrm((DEPTH, D_MODEL), 0.01),
        "mod_w": nrm((DEPTH, D_MODEL, 3 * D_MODEL), D_MODEL ** -0.5),
        "mod_b": nrm((DEPTH, 3 * D_MODEL), 0.01),
        "ab_in_w": nrm((N_AB, D_MODEL, IN_AB), D_MODEL ** -0.5),
        "ab_out_w": nrm((N_AB, OUT_AB, D_MODEL), OUT_AB ** -0.5),
        "conv_a_w": nrm((N_AB, CONV_W, W_A), CONV_W ** -0.5),
        "conv_a_b": nrm((N_AB, W_A), 0.01),
        "lru_gx_w": nrm((N_AB, H_A, BW_A, BW_A), BW_A ** -0.5),
        "lru_gx_b": nrm((N_AB, W_A), 0.01),
        "lru_ga_w": nrm((N_AB, H_A, BW_A, BW_A), BW_A ** -0.5),
        "lru_ga_b": nrm((N_AB, W_A), 0.01),
        "lru_a_param": jnp.log(base) - jnp.log1p(-base),
        "conv_b_w": nrm((N_AB, CONV_W, QKV_B), CONV_W ** -0.5),
        "gdn_a_log": jnp.log(unif((N_AB, H_B), 1.0, 16.0)),
        "gdn_dt_bias": dt_b + jnp.log(-jnp.expm1(-dt_b)),
        "gdn_norm_w": 1.0 + nrm((N_AB, DV), 0.01),
        "c_in_w": nrm((N_C, D_MODEL, 2 * W_C), D_MODEL ** -0.5),
        "c_out_w": nrm((N_C, W_C, D_MODEL), W_C ** -0.5),
        "s5_a_re": -0.5 + nrm((N_C, G_C, P_C), 0.01),
        "s5_a_im": jnp.pi * jnp.arange(P_C, dtype=f32) + nrm((N_C, G_C, P_C), 0.01),
        "s5_b_re": nrm((N_C, G_C, P_C, CG), (2 * CG) ** -0.5),
        "s5_b_im": nrm((N_C, G_C, P_C, CG), (2 * CG) ** -0.5),
        "s5_c_re": nrm((N_C, G_C, CG, P_C), P_C ** -0.5),
        "s5_c_im": nrm((N_C, G_C, CG, P_C), P_C ** -0.5),
        "s5_d": nrm((N_C, W_C), 1.0),
        "s5_log_dt": unif((N_C, G_C), math.log(1e-3), math.log(1e-1)),
        "glu_w": nrm((N_C, W_C, W_C), W_C ** -0.5),
        "glu_b": nrm((N_C, W_C), 0.01),
        "final_norm_w": 1.0 + nrm((D_MODEL,), 0.01),
    }


def reference(x_prompt, x_sample, c_prompt, c_sample, state_conv_a, state_lru, state_conv_b,
              state_delta, state_s5_re, state_s5_im, norm_w, mod_w, mod_b, ab_in_w, ab_out_w,
              conv_a_w, conv_a_b, lru_gx_w, lru_gx_b, lru_ga_w, lru_ga_b, lru_a_param,
              conv_b_w, gdn_a_log, gdn_dt_bias, gdn_norm_w, c_in_w, c_out_w, s5_a_re, s5_a_im,
              s5_b_re, s5_b_im, s5_c_re, s5_c_im, s5_d, s5_log_dt, glu_w, glu_b, final_norm_w):
    w = dict(norm_w=norm_w, mod_w=mod_w, mod_b=mod_b, ab_in_w=ab_in_w, ab_out_w=ab_out_w,
             conv_a_w=conv_a_w, conv_a_b=conv_a_b, lru_gx_w=lru_gx_w, lru_gx_b=lru_gx_b,
             lru_ga_w=lru_ga_w, lru_ga_b=lru_ga_b, lru_a_param=lru_a_param, conv_b_w=conv_b_w,
             gdn_a_log=gdn_a_log, gdn_dt_bias=gdn_dt_bias, gdn_norm_w=gdn_norm_w,
             c_in_w=c_in_w, c_out_w=c_out_w, s5_a_re=s5_a_re, s5_a_im=s5_a_im,
             s5_b_re=s5_b_re, s5_b_im=s5_b_im, s5_c_re=s5_c_re, s5_c_im=s5_c_im, s5_d=s5_d,
             s5_log_dt=s5_log_dt, glu_w=glu_w, glu_b=glu_b, final_norm_w=final_norm_w)
    nb = x_prompt.shape[0]
    z_conv_a = jnp.zeros((N_AB, nb, CONV_W - 1, W_A), state_conv_a.dtype)
    z_lru = jnp.zeros((N_AB, nb, W_A), state_lru.dtype)
    z_conv_b = jnp.zeros((N_AB, nb, CONV_W - 1, QKV_B), state_conv_b.dtype)
    z_delta = jnp.zeros((N_AB, nb, H_B, DK, DV), state_delta.dtype)
    z_s5_re = jnp.zeros((N_C, nb, G_C, P_C), state_s5_re.dtype)
    z_s5_im = jnp.zeros((N_C, nb, G_C, P_C), state_s5_im.dtype)
    y_prompt, p_conv_a, p_lru, p_conv_b, p_delta, p_s5_re, p_s5_im = trunk(
        x_prompt, c_prompt, z_conv_a, z_lru, z_conv_b, z_delta, z_s5_re, z_s5_im, 0, w)
    y_sample, s_conv_a, s_lru, s_conv_b, s_delta, s_s5_re, s_s5_im = trunk(
        x_sample, c_sample, state_conv_a, state_lru, state_conv_b, state_delta,
        state_s5_re, state_s5_im, PAST_LEN, w)
    return (y_prompt, y_sample, p_conv_a, p_lru, p_conv_b, p_delta, p_s5_re, p_s5_im,
            s_conv_a, s_lru, s_conv_b, s_delta, s_s5_re, s_s5_im)
```

```python
import functools

import jax
import jax.numpy as jnp
from jax import lax
from jax.experimental import pallas as pl
from jax.experimental.pallas import tpu as pltpu

F32 = jnp.float32
BF16 = jnp.bfloat16

D_MODEL = 1024
DEPTH = 4
CONV_W = 4
W_A = 1024
H_A = 8
BW_A = 128
LRU_C = 8.0
H_B = 8
DK = 128
DV = 128
W_B = H_B * DV
QKV_B = 3 * W_B
CG = 16
G_C = 64
P_C = 64
NS = G_C * P_C
EPS = 1e-6
LANE = 128
GDN_BLOCK = 64
S5_LANE_BLOCKS = D_MODEL // LANE
S5_BLOCK_STATE = NS // S5_LANE_BLOCKS
VMEM_LIMIT = 58 * 1024 * 1024


def _dot(a, b):
    return jnp.dot(a.astype(BF16), b.astype(BF16), preferred_element_type=F32)


def _dot_nt(a, b):
    return lax.dot_general(a.astype(BF16), b.astype(BF16), (((1,), (1,)), ((), ())),
                           preferred_element_type=F32)


def _silu(x):
    return x * jax.nn.sigmoid(x)


def _softplus(x):
    return jnp.maximum(x, 0.0) + jnp.log1p(jnp.exp(-jnp.abs(x)))


def _expm1(x):
    return jnp.tanh(0.5 * x) * (jnp.exp(x) + 1.0)


def _norm_mod(x, normw, mod_ref, nb):
    rows = x.shape[0]
    ms = jnp.mean(x * x, axis=-1, keepdims=True)
    y = x * lax.rsqrt(ms + EPS) * normw
    shift = mod_ref[:, 0:D_MODEL]
    scale = mod_ref[:, D_MODEL:2 * D_MODEL]
    y3 = y.reshape(rows // nb, nb, D_MODEL)
    return (y3 * (1.0 + scale)[None] + shift[None]).reshape(rows, D_MODEL)


def _residual(x, out, mod_ref, nb):
    rows = x.shape[0]
    gate = mod_ref[:, 2 * D_MODEL:3 * D_MODEL]
    return x + (out.reshape(rows // nb, nb, D_MODEL) * gate[None]).reshape(rows, D_MODEL)


def _conv_inplace(ext, lane0, w_ref, w_lane0, nb, rows, post):
    sl = slice(lane0, lane0 + LANE)
    wl = slice(w_lane0, w_lane0 + LANE)
    acc = ext[0:rows, sl] * w_ref[0:1, wl]
    for j in range(1, CONV_W):
        acc = acc + ext[j * nb:j * nb + rows, sl] * w_ref[j:j + 1, wl]
    tail = ext[rows:rows + 3 * nb, sl]
    ext[3 * nb:3 * nb + rows, sl] = post(acc)
    ext[0:3 * nb, sl] = tail


def _l2norm(t):
    return t * lax.rsqrt(jnp.sum(t * t, axis=-1, keepdims=True) + EPS)


def _ab_pre(first, nb, nt, reset_first, x, mod_ref, normw_ref, w_in_ref, w_ba_ref, caw_ref, cab_ref,
            gxw_ref, gxb_ref, gaw_ref, gab_ref, apar_ref, cbw_ref, alog_ref, dtb_ref,
            ext_a, ext_b, za, zbs, la, lb, gsc, bsc, h_s):
    rows = nb * nt
    hb = _norm_mod(x, normw_ref[...], mod_ref, nb).astype(BF16)
    ext_a[3 * nb:3 * nb + rows, :] = _dot(hb, w_in_ref[:, 0:W_A])
    za[...] = _dot(hb, w_in_ref[:, W_A:2 * W_A])
    for n in range(QKV_B // 1024):
        part = _dot(hb, w_in_ref[:, 2 * W_A + n * 1024:2 * W_A + (n + 1) * 1024])
        for m in range(1024 // LANE):
            ext_b[n * (1024 // LANE) + m, 3 * nb:3 * nb + rows, :] = part[:, m * LANE:(m + 1) * LANE]
    part = _dot(hb, w_in_ref[:, 2 * W_A + QKV_B:2 * W_A + QKV_B + W_B])
    for m in range(H_B):
        zbs[m] = part[:, m * LANE:(m + 1) * LANE]
    ba = _dot(hb, w_ba_ref[...])
    bsc[...] = jax.nn.sigmoid(ba[:, 0:LANE])
    gsc[...] = -jnp.exp(alog_ref[...]) * _softplus(ba[:, LANE:2 * LANE] + dtb_ref[...])

    for blk in range(H_A):
        sl = slice(blk * BW_A, (blk + 1) * BW_A)
        _conv_inplace(ext_a, blk * BW_A, caw_ref, blk * BW_A, nb, rows,
                      lambda acc, sl=sl: acc + cab_ref[:, sl])
        xb = ext_a[3 * nb:3 * nb + rows, sl]
        xbb = xb.astype(BF16)
        gate_x = jax.nn.sigmoid(_dot(xbb, gxw_ref[blk]) + gxb_ref[:, sl])
        gate_a = jax.nn.sigmoid(_dot(xbb, gaw_ref[blk]) + gab_ref[:, sl])
        log_a = -LRU_C * gate_a * _softplus(-apar_ref[:, sl])
        mult = jnp.sqrt(-_expm1(2.0 * log_a))
        if reset_first:
            row = lax.broadcasted_iota(jnp.int32, (rows, BW_A), 0)
            mult = jnp.where(jnp.logical_and(first, row < nb), 1.0, mult)
        la[:, sl] = jnp.exp(log_a)
        lb[:, sl] = mult * gate_x * xb

    def lru_step(t, h):
        r = pl.multiple_of(t * nb, nb)
        h = la[pl.ds(r, nb), :] * h + lb[pl.ds(r, nb), :]
        lb[pl.ds(r, nb), :] = h
        return h

    if nt == 1:
        h = la[...] * h_s[...] + lb[...]
        lb[...] = h
        h_s[...] = h
    else:
        h_s[...] = lax.fori_loop(0, nt, lru_step, h_s[...])

    for n in range(QKV_B // LANE):
        if n < H_B:
            post = lambda acc: _l2norm(_silu(acc)) * (DK ** -0.5)
        elif n < 2 * H_B:
            post = lambda acc: _l2norm(_silu(acc))
        else:
            post = _silu
        _conv_inplace(ext_b.at[n], 0, cbw_ref, n * LANE, nb, rows, post)


def _gdn_block_masks():
    n = 2 * GDN_BLOCK
    ri = lax.broadcasted_iota(jnp.int32, (n, n), 0)
    ci = lax.broadcasted_iota(jnp.int32, (n, n), 1)
    same = (ri >= GDN_BLOCK) == (ci >= GDN_BLOCK)
    tri = jnp.where(jnp.logical_and(same, ri >= ci), 1.0, 0.0).astype(F32)
    strict = jnp.where(jnp.logical_and(same, ri > ci), 1.0, 0.0).astype(F32)
    eye = jnp.where(ri == ci, 1.0, 0.0).astype(F32)
    levels = []
    for l in range(6):
        rb = ri >> l
        sub = jnp.logical_and((rb & 1) == 1, (ci >> l) == rb - 1)
        levels.append(jnp.where(jnp.logical_and(same, sub), 1.0, 0.0).astype(F32))
    return tri, strict, eye, levels


def _gdn_block(b, nb, ext_b, zbs, gsc, bsc, s_ref, gnw_ref, masks):
    c_len = GDN_BLOCK
    rows = pl.ds(3 * nb + b, c_len, stride=nb)
    zrows = pl.ds(b, c_len, stride=nb)
    gcb = gsc[zrows, :]
    betab = bsc[zrows, :]
    glast = gsc[pl.ds((c_len - 1) * nb + b, 1), :]
    top = lax.broadcasted_iota(jnp.int32, (2 * c_len, LANE), 0) < c_len
    tri, strict, eye, levels = masks

    def cat(a0, a1):
        return jnp.concatenate([a0, a1], axis=0)

    for p in range(H_B // 2):
        h0, h1 = 2 * p, 2 * p + 1

        def head_pair(base):
            return cat(ext_b[base + h0, rows, :], ext_b[base + h1, rows, :])

        q = head_pair(0)
        k = head_pair(H_B)
        v = head_pair(2 * H_B)
        c = cat(gcb[:, h0:h0 + 1], gcb[:, h1:h1 + 1])
        bcol = cat(betab[:, h0:h0 + 1], betab[:, h1:h1 + 1])
        gl = cat(jnp.broadcast_to(glast[:, h0:h0 + 1], (c_len, 1)),
                 jnp.broadcast_to(glast[:, h1:h1 + 1], (c_len, 1)))
        cm = jnp.broadcast_to(c, (2 * c_len, 2 * c_len))
        decay = jnp.exp((cm - cm.T) * tri) * tri
        kb = k * bcol
        a_mat = _dot_nt(kb, k) * decay * strict
        qk = _dot_nt(q, k) * decay
        x = eye - a_mat * levels[0]
        for l in range(1, 6):
            x = x - _dot(x, _dot(a_mat * levels[l], x))
        rhs = jnp.concatenate([v * bcol, kb * jnp.exp(c)], axis=1)
        sol = _dot(x, rhs)
        u, w = sol[:, 0:DV], sol[:, DV:2 * DV]
        s0 = s_ref[b, h0]
        s1 = s_ref[b, h1]
        sp = jnp.concatenate([s0, s1], axis=1)
        ws = _dot(w, sp)
        v_new = u - jnp.where(top, ws[:, 0:DV], ws[:, DV:2 * DV])
        qs = _dot(q * jnp.exp(c), sp)
        o = jnp.where(top, qs[:, 0:DV], qs[:, DV:2 * DV]) + _dot(qk, v_new)
        kdec = k * jnp.exp(gl - c)
        vblk = jnp.concatenate([jnp.where(top, v_new, 0.0), jnp.where(top, 0.0, v_new)], axis=1)
        upd = _dot(kdec.T, vblk)
        s_ref[b, h0] = s0 * jnp.exp(glast[:, h0:h0 + 1]) + upd[:, 0:DV]
        s_ref[b, h1] = s1 * jnp.exp(glast[:, h1:h1 + 1]) + upd[:, DV:2 * DV]
        on = o * lax.rsqrt(jnp.mean(o * o, axis=-1, keepdims=True) + EPS) * gnw_ref[...]
        zb = cat(zbs[h0, zrows, :], zbs[h1, zrows, :])
        og = on * _silu(zb)
        ext_b[h0, rows, :] = og[0:c_len]
        ext_b[h1, rows, :] = og[c_len:2 * c_len]


def _ab_post(x, nb, mod_ref, ya, o, w_out_ref):
    out = _dot(ya.astype(BF16), w_out_ref[0:W_A, :]) + _dot(o.astype(BF16), w_out_ref[W_A:W_A + W_B, :])
    return _residual(x, out, mod_ref, nb)


def _ab_prompt_kernel(nb, nt, x_ref, mod_ref, normw_ref, w_in_ref, w_ba_ref, caw_ref, cab_ref, gxw_ref,
                      gxb_ref, gaw_ref, gab_ref, apar_ref, cbw_ref, alog_ref, dtb_ref, gnw_ref, w_out_ref,
                      y_ref, ca_out, lru_out, cb_out, s_ref,
                      ext_a, ext_b, za, zbs, la, lb, gsc, bsc, h_s):
    c = pl.program_id(0)
    rows = nb * nt

    @pl.when(c == 0)
    def _():
        ext_a[0:3 * nb, :] = jnp.zeros((3 * nb, W_A), F32)
        ext_b[:, 0:3 * nb, :] = jnp.zeros((QKV_B // LANE, 3 * nb, LANE), F32)
        h_s[...] = jnp.zeros(h_s.shape, F32)
        s_ref[...] = jnp.zeros(s_ref.shape, F32)

    x = x_ref[...]
    _ab_pre(c == 0, nb, nt, True, x, mod_ref, normw_ref, w_in_ref, w_ba_ref, caw_ref, cab_ref, gxw_ref,
            gxb_ref, gaw_ref, gab_ref, apar_ref, cbw_ref, alog_ref, dtb_ref,
            ext_a, ext_b, za, zbs, la, lb, gsc, bsc, h_s)

    def cum_step(t, acc):
        r = pl.multiple_of(t * nb, nb)
        acc = acc + gsc[pl.ds(r, nb), :]
        gsc[pl.ds(r, nb), :] = acc
        return acc

    lax.fori_loop(0, nt, cum_step, jnp.zeros((nb, LANE), F32))
    masks = _gdn_block_masks()

    def per_seq(b, carry):
        _gdn_block(b, nb, ext_b, zbs, gsc, bsc, s_ref, gnw_ref, masks)
        return carry

    lax.fori_loop(0, nb, per_seq, 0)

    ya = lb[...] * _silu(za[...])
    o = jnp.concatenate([ext_b[h, 3 * nb:3 * nb + rows, :] for h in range(H_B)], axis=1)
    y_ref[...] = _ab_post(x, nb, mod_ref, ya, o, w_out_ref)

    @pl.when(c == pl.num_programs(0) - 1)
    def _():
        ca_out[...] = ext_a[0:3 * nb, :]
        for n in range(QKV_B // LANE):
            cb_out[:, n * LANE:(n + 1) * LANE] = ext_b[n, 0:3 * nb, :]
        lru_out[...] = h_s[...]


def _ab_sample_pre_kernel(nb, x_ref, mod_ref, normw_ref, w_in_ref, w_ba_ref, caw_ref, cab_ref, gxw_ref,
                          gxb_ref, gaw_ref, gab_ref, apar_ref, cbw_ref, alog_ref, dtb_ref,
                          ca_in, lru_in, cb_in,
                          qkv_out, g_out, beta_out, zb_out, ya_out, ca_out, lru_out, cb_out,
                          ext_a, ext_b, za, zbs, la, lb, gsc, bsc, h_s):
    ext_a[0:3 * nb, :] = ca_in[...]
    for n in range(QKV_B // LANE):
        ext_b[n, 0:3 * nb, :] = cb_in[:, n * LANE:(n + 1) * LANE]
    h_s[...] = lru_in[...]
    _ab_pre(False, nb, 1, False, x_ref[...], mod_ref, normw_ref, w_in_ref, w_ba_ref, caw_ref, cab_ref,
            gxw_ref, gxb_ref, gaw_ref, gab_ref, apar_ref, cbw_ref, alog_ref, dtb_ref,
            ext_a, ext_b, za, zbs, la, lb, gsc, bsc, h_s)
    for n in range(QKV_B // LANE):
        qkv_out[:, n * LANE:(n + 1) * LANE] = ext_b[n, 3 * nb:4 * nb, :]
        cb_out[:, n * LANE:(n + 1) * LANE] = ext_b[n, 0:3 * nb, :]
    g_out[...] = gsc[...]
    beta_out[...] = bsc[...]
    for h in range(H_B):
        zb_out[:, h * LANE:(h + 1) * LANE] = zbs[h]
    ya_out[...] = lb[...] * _silu(za[...])
    ca_out[...] = ext_a[0:3 * nb, :]
    lru_out[...] = h_s[...]


def _gdn_step_kernel(bb, qkv_ref, g_ref, beta_ref, zb_ref, gnw_ref, s_in, o_ref, s_out):
    def per_seq(i, carry):
        g_row = g_ref[i]
        beta_row = beta_ref[i]
        for h in range(H_B):
            q = qkv_ref[i, :, h * LANE:(h + 1) * LANE]
            k = qkv_ref[i, :, H_B * DK + h * LANE:H_B * DK + (h + 1) * LANE]
            v = qkv_ref[i, :, 2 * H_B * DK + h * LANE:2 * H_B * DK + (h + 1) * LANE]
            eg = jnp.exp(g_row[:, h:h + 1])
            beta = beta_row[:, h:h + 1]
            kcol = jnp.broadcast_to(k, (DK, DK)).T
            qcol = jnp.broadcast_to(q, (DK, DK)).T
            s = s_in[i, h]
            ks = jnp.sum(kcol * s, axis=0, keepdims=True)
            v_new = beta * (v - eg * ks)
            s_new = eg * s + kcol * v_new
            s_out[i, h] = s_new
            o = jnp.sum(qcol * s_new, axis=0, keepdims=True)
            on = o * lax.rsqrt(jnp.mean(o * o, axis=-1, keepdims=True) + EPS) * gnw_ref[...]
            zb = zb_ref[i, :, h * LANE:(h + 1) * LANE]
            o_ref[i, :, h * LANE:(h + 1) * LANE] = on * _silu(zb)
        return carry

    lax.fori_loop(0, bb, per_seq, 0)


def _ab_sample_post_kernel(nb, x_ref, mod_ref, ya_ref, o_ref, w_out_ref, y_ref):
    y_ref[...] = _ab_post(x_ref[...], nb, mod_ref, ya_ref[...], o_ref[...], w_out_ref)


def _s5_kernel(nb, nt, has_state, final_norm, *refs):
    (x_ref, mod_ref, normw_ref, w_in_ref, bre_ref, bim_ref, cre_ref, cim_ref, abr_ref, abi_ref, d_ref,
     gluw_ref, glub_ref, w_out_ref, fnw_ref) = refs[:15]
    refs = refs[15:]
    if has_state:
        sre_in, sim_in = refs[:2]
        refs = refs[2:]
    y_ref, sre_out, sim_out, xs_re, xs_im, y_s = refs
    c = pl.program_id(0)
    rows = nb * nt

    @pl.when(c == 0)
    def _():
        if has_state:
            xs_re[0:nb, :] = sre_in[...]
            xs_im[0:nb, :] = sim_in[...]
        else:
            xs_re[0:nb, :] = jnp.zeros((nb, NS), F32)
            xs_im[0:nb, :] = jnp.zeros((nb, NS), F32)

    x = x_ref[...]
    hb = _norm_mod(x, normw_ref[...], mod_ref, nb).astype(BF16)
    uz = _dot(hb, w_in_ref[...])
    u = uz[:, 0:D_MODEL]
    z = uz[:, D_MODEL:2 * D_MODEL]
    ub = u.astype(BF16)
    for j in range(S5_LANE_BLOCKS):
        sl = slice(j * S5_BLOCK_STATE, (j + 1) * S5_BLOCK_STATE)
        uj = ub[:, j * LANE:(j + 1) * LANE]
        xs_re[nb:nb + rows, sl] = _dot(uj, bre_ref[j])
        xs_im[nb:nb + rows, sl] = _dot(uj, bim_ref[j])

    for j in range(S5_LANE_BLOCKS):
        sl = slice(j * S5_BLOCK_STATE, (j + 1) * S5_BLOCK_STATE)
        ar = jnp.broadcast_to(abr_ref[:, sl], (nb, S5_BLOCK_STATE))
        ai = jnp.broadcast_to(abi_ref[:, sl], (nb, S5_BLOCK_STATE))

        def step(t, carry, sl=sl, ar=ar, ai=ai):
            sr, si = carry
            r = pl.multiple_of(nb + t * nb, nb)
            nr = ar * sr - ai * si + xs_re[pl.ds(r, nb), sl]
            ni = ar * si + ai * sr + xs_im[pl.ds(r, nb), sl]
            xs_re[pl.ds(r, nb), sl] = nr
            xs_im[pl.ds(r, nb), sl] = ni
            return nr, ni

        carry = (xs_re[0:nb, sl], xs_im[0:nb, sl])
        if nt == 1:
            sr, si = carry
            nr = ar * sr - ai * si + xs_re[nb:2 * nb, sl]
            ni = ar * si + ai * sr + xs_im[nb:2 * nb, sl]
            xs_re[nb:2 * nb, sl] = nr
            xs_im[nb:2 * nb, sl] = ni
        else:
            nr, ni = lax.fori_loop(0, nt, step, carry)
        xs_re[0:nb, sl] = nr
        xs_im[0:nb, sl] = ni

    for j in range(S5_LANE_BLOCKS):
        sl = slice(j * S5_BLOCK_STATE, (j + 1) * S5_BLOCK_STATE)
        cl = slice(j * LANE, (j + 1) * LANE)
        yj = (_dot(xs_re[nb:nb + rows, sl].astype(BF16), cre_ref[j])
              - _dot(xs_im[nb:nb + rows, sl].astype(BF16), cim_ref[j]))
        yj = yj + d_ref[:, cl] * u[:, cl]
        cdf = 0.5 * (1.0 + jnp.tanh(0.7978845608028654 * (yj + 0.044715 * (yj * yj * yj))))
        y_s[:, cl] = yj * cdf
    y = y_s[...]
    y = y * jax.nn.sigmoid(_dot(y.astype(BF16), gluw_ref[...]) + glub_ref[...])
    y = y * _silu(z)
    xn = _residual(x, _dot(y.astype(BF16), w_out_ref[...]), mod_ref, nb)
    if final_norm:
        xn = xn * lax.rsqrt(jnp.mean(xn * xn, axis=-1, keepdims=True) + EPS) * fnw_ref[...]
    y_ref[...] = xn

    @pl.when(c == pl.num_programs(0) - 1)
    def _():
        sre_out[...] = xs_re[0:nb, :]
        sim_out[...] = xs_im[0:nb, :]


def _s5_prep_kernel(are_ref, aim_ref, ldt_ref, bre_ref, bim_ref, abr_out, abi_out, bbr_out, bbi_out):
    a_re = are_ref[...]
    a_im = aim_ref[...]
    dt = jnp.exp(ldt_ref[...])
    mag = jnp.exp(a_re * dt)
    abr = mag * jnp.cos(a_im * dt)
    abi = mag * jnp.sin(a_im * dt)
    abr_out[...] = abr
    abi_out[...] = abi
    den = a_re * a_re + a_im * a_im
    nr = abr - 1.0
    cr = (nr * a_re + abi * a_im) / den
    ci = (abi * a_re - nr * a_im) / den
    b_re = bre_ref[...]
    b_im = bim_ref[...]
    bbr_out[...] = cr[:, None, :] * b_re - ci[:, None, :] * b_im
    bbi_out[...] = cr[:, None, :] * b_im + ci[:, None, :] * b_re


def _mod_kernel(c_ref, w_ref, b_ref, o_ref):
    c = c_ref[...]
    o_ref[...] = _dot(_silu(c).astype(BF16), w_ref[...].astype(BF16)) + b_ref[...]


def _full(shape):
    n = len(shape)
    return pl.BlockSpec(shape, lambda *_: (0,) * n)


def _const(shape):
    n = len(shape)
    return pl.BlockSpec(shape, lambda *_: (0,) * n, pipeline_mode=pl.Buffered(1))


def _params(n_grid):
    return pltpu.CompilerParams(dimension_semantics=("arbitrary",) * n_grid, vmem_limit_bytes=VMEM_LIMIT)


def _mod_all(c_all, mod_w, mod_b):
    n = c_all.shape[0]
    return pl.pallas_call(
        _mod_kernel,
        grid=(DEPTH, 3),
        in_specs=[pl.BlockSpec((n, D_MODEL), lambda i, j: (0, 0)),
                  pl.BlockSpec((None, D_MODEL, D_MODEL), lambda i, j: (i, 0, j)),
                  pl.BlockSpec((None, 1, D_MODEL), lambda i, j: (i, 0, j))],
        out_specs=pl.BlockSpec((None, n, D_MODEL), lambda i, j: (i, 0, j)),
        out_shape=jax.ShapeDtypeStruct((DEPTH, n, 3 * D_MODEL), F32),
        compiler_params=_params(2),
        name="mod_all",
    )(c_all, mod_w, mod_b.reshape(DEPTH, 1, 3 * D_MODEL))


def _ab_weights(j, ab_in_w, ab_out_w, conv_a_w, conv_a_b, lru_gx_w, lru_gx_b, lru_ga_w, lru_ga_b,
                lru_a_param, conv_b_w, gdn_a_log, gdn_dt_bias, gdn_norm_w):
    n_main = 2 * W_A + QKV_B + W_B
    w_in = ab_in_w[j, :, 0:n_main].astype(BF16)
    w_ba = jnp.zeros((D_MODEL, 2 * LANE), F32)
    w_ba = w_ba.at[:, 0:H_B].set(ab_in_w[j, :, n_main:n_main + H_B])
    w_ba = w_ba.at[:, LANE:LANE + H_B].set(ab_in_w[j, :, n_main + H_B:n_main + 2 * H_B]).astype(BF16)
    pad = lambda t: jnp.zeros((1, LANE), F32).at[0, 0:H_B].set(t)
    return dict(
        w_in=w_in, w_ba=w_ba, caw=conv_a_w[j], cab=conv_a_b[j].reshape(1, W_A),
        gxw=lru_gx_w[j].astype(BF16), gxb=lru_gx_b[j].reshape(1, W_A),
        gaw=lru_ga_w[j].astype(BF16), gab=lru_ga_b[j].reshape(1, W_A),
        apar=lru_a_param[j].reshape(1, W_A), cbw=conv_b_w[j], alog=pad(gdn_a_log[j]),
        dtb=pad(gdn_dt_bias[j]), gnw=gdn_norm_w[j].reshape(1, DV), w_out=ab_out_w[j].astype(BF16))


_AB_PRE_NAMES = ("w_in", "w_ba", "caw", "cab", "gxw", "gxb", "gaw", "gab", "apar", "cbw", "alog", "dtb")


def _ab_scratch(nb, rows):
    return [pltpu.VMEM((3 * nb + rows, W_A), F32), pltpu.VMEM((QKV_B // LANE, 3 * nb + rows, LANE), F32),
            pltpu.VMEM((rows, W_A), F32), pltpu.VMEM((H_B, rows, LANE), F32),
            pltpu.VMEM((rows, W_A), F32), pltpu.VMEM((rows, W_A), F32),
            pltpu.VMEM((rows, LANE), F32), pltpu.VMEM((rows, LANE), F32), pltpu.VMEM((nb, W_A), F32)]


def _ab_prompt(x, mod, normw, w, nb, nt):
    rows = nb * nt
    n_chunks = x.shape[0] // rows
    pre = [w[k] for k in _AB_PRE_NAMES]
    return pl.pallas_call(
        functools.partial(_ab_prompt_kernel, nb, nt),
        grid=(n_chunks,),
        in_specs=[pl.BlockSpec((rows, D_MODEL), lambda c: (c, 0)), _const(mod.shape), _const(normw.shape)]
        + [_const(t.shape) for t in pre] + [_const(w["gnw"].shape), _const(w["w_out"].shape)],
        out_specs=[pl.BlockSpec((rows, D_MODEL), lambda c: (c, 0)), _full((3 * nb, W_A)), _full((nb, W_A)),
                   _full((3 * nb, QKV_B)), _full((nb, H_B, DK, DV))],
        out_shape=[jax.ShapeDtypeStruct(x.shape, F32), jax.ShapeDtypeStruct((3 * nb, W_A), F32),
                   jax.ShapeDtypeStruct((nb, W_A), F32), jax.ShapeDtypeStruct((3 * nb, QKV_B), F32),
                   jax.ShapeDtypeStruct((nb, H_B, DK, DV), F32)],
        scratch_shapes=_ab_scratch(nb, rows),
        compiler_params=_params(1),
        name="ab_prompt",
    )(x, mod, normw, *pre, w["gnw"], w["w_out"])


def _ab_sample(x, mod, normw, w, conv_a, lru_h, conv_b, delta):
    nb = x.shape[0]
    pre = [w[k] for k in _AB_PRE_NAMES]
    ins = [x, mod, normw, *pre, conv_a, lru_h, conv_b]
    outs = [(nb, QKV_B), (nb, LANE), (nb, LANE), (nb, W_B), (nb, W_A), (3 * nb, W_A), (nb, W_A),
            (3 * nb, QKV_B)]
    qkv, g, beta, zb, ya, ca_new, lru_new, cb_new = pl.pallas_call(
        functools.partial(_ab_sample_pre_kernel, nb),
        in_specs=[_full(t.shape) for t in ins],
        out_specs=[_full(s) for s in outs],
        out_shape=[jax.ShapeDtypeStruct(s, F32) for s in outs],
        scratch_shapes=_ab_scratch(nb, nb),
        compiler_params=pltpu.CompilerParams(vmem_limit_bytes=VMEM_LIMIT),
        name="ab_sample_pre",
    )(*ins)
    bb = 8
    row_block = lambda width: pl.BlockSpec((bb, 1, width), lambda i: (i, 0, 0))
    state_block = pl.BlockSpec((bb, H_B, DK, DV), lambda i: (i, 0, 0, 0))
    o, delta_new = pl.pallas_call(
        functools.partial(_gdn_step_kernel, bb),
        grid=(nb // bb,),
        in_specs=[row_block(QKV_B), row_block(LANE), row_block(LANE), row_block(W_B), _full((1, DV)),
                  state_block],
        out_specs=[row_block(W_B), state_block],
        out_shape=[jax.ShapeDtypeStruct((nb, 1, W_B), F32), jax.ShapeDtypeStruct(delta.shape, F32)],
        compiler_params=_params(1),
        name="gdn_step",
    )(qkv.reshape(nb, 1, QKV_B), g.reshape(nb, 1, LANE), beta.reshape(nb, 1, LANE), zb.reshape(nb, 1, W_B),
      w["gnw"], delta)
    o = o.reshape(nb, W_B)
    y = pl.pallas_call(
        functools.partial(_ab_sample_post_kernel, nb),
        in_specs=[_full(x.shape), _full(mod.shape), _full(ya.shape), _full(o.shape), _full(w["w_out"].shape)],
        out_specs=_full(x.shape),
        out_shape=jax.ShapeDtypeStruct(x.shape, F32),
        compiler_params=pltpu.CompilerParams(vmem_limit_bytes=VMEM_LIMIT),
        name="ab_sample_post",
    )(x, mod, ya, o, w["w_out"])
    return y, ca_new, lru_new, cb_new, delta_new


def _s5_prep(s5_a_re, s5_a_im, s5_log_dt, s5_b_re, s5_b_im):
    n = s5_a_re.shape[0]
    gp = pl.BlockSpec((None, G_C, P_C), lambda i: (i, 0, 0))
    gcp = pl.BlockSpec((None, G_C, CG, P_C), lambda i: (i, 0, 0, 0))
    return pl.pallas_call(
        _s5_prep_kernel,
        grid=(n,),
        in_specs=[gp, gp, pl.BlockSpec((None, G_C, 1), lambda i: (i, 0, 0)), gcp, gcp],
        out_specs=[gp, gp, gcp, gcp],
        out_shape=[jax.ShapeDtypeStruct((n, G_C, P_C), F32)] * 2
        + [jax.ShapeDtypeStruct((n, G_C, CG, P_C), F32)] * 2,
        compiler_params=_params(1),
        name="s5_prep",
    )(s5_a_re, s5_a_im, s5_log_dt.reshape(n, G_C, 1), jnp.swapaxes(s5_b_re, 2, 3), jnp.swapaxes(s5_b_im, 2, 3))


def _block_diag_in(t):
    gl = G_C // S5_LANE_BLOCKS
    t = t.reshape(S5_LANE_BLOCKS, gl, CG, P_C)
    eye = jnp.eye(gl, dtype=t.dtype)
    return jnp.einsum("jgcp,gh->jgchp", t, eye).reshape(S5_LANE_BLOCKS, gl * CG, gl * P_C)


def _block_diag_out(t):
    gl = G_C // S5_LANE_BLOCKS
    t = t.reshape(S5_LANE_BLOCKS, gl, CG, P_C)
    eye = jnp.eye(gl, dtype=t.dtype)
    return jnp.einsum("jgcp,gh->jgphc", t, eye).reshape(S5_LANE_BLOCKS, gl * P_C, gl * CG)


def _s5_layer(x, mod, normw, w, nb, nt, state, final_norm):
    rows = nb * nt
    n_chunks = x.shape[0] // rows
    consts = [mod, normw, w["w_in"], w["bre"], w["bim"], w["cre"], w["cim"], w["abr"], w["abi"], w["d"],
              w["gluw"], w["glub"], w["w_out"], w["fnw"]]
    ins = [x, *consts]
    in_specs = [pl.BlockSpec((rows, D_MODEL), lambda c: (c, 0))] + [_const(t.shape) for t in consts]
    if state is not None:
        ins += list(state)
        in_specs += [_const((nb, NS))] * 2
    return pl.pallas_call(
        functools.partial(_s5_kernel, nb, nt, state is not None, final_norm),
        grid=(n_chunks,),
        in_specs=in_specs,
        out_specs=[pl.BlockSpec((rows, D_MODEL), lambda c: (c, 0)), _full((nb, NS)), _full((nb, NS))],
        out_shape=[jax.ShapeDtypeStruct(x.shape, F32), jax.ShapeDtypeStruct((nb, NS), F32),
                   jax.ShapeDtypeStruct((nb, NS), F32)],
        scratch_shapes=[pltpu.VMEM((nb + rows, NS), F32), pltpu.VMEM((nb + rows, NS), F32),
                        pltpu.VMEM((rows, D_MODEL), F32)],
        compiler_params=_params(1),
        name="s5_layer",
    )(*ins)


def _to_time_major(t):
    b, l, c = t.shape
    return jnp.swapaxes(t, 0, 1).reshape(l * b, c)


def _from_time_major(t, b):
    lb, c = t.shape
    return jnp.swapaxes(t.reshape(lb // b, b, c), 0, 1)


def kernel(x_prompt, x_sample, c_prompt, c_sample, state_conv_a, state_lru, state_conv_b, state_delta, state_s5_re, state_s5_im, norm_w, mod_w, mod_b, ab_in_w, ab_out_w, conv_a_w, conv_a_b, lru_gx_w, lru_gx_b, lru_ga_w, lru_ga_b, lru_a_param, conv_b_w, gdn_a_log, gdn_dt_bias, gdn_norm_w, c_in_w, c_out_w, s5_a_re, s5_a_im, s5_b_re, s5_b_im, s5_c_re, s5_c_im, s5_d, s5_log_dt, glu_w, glu_b, final_norm_w):
    nbp, seq, _ = x_prompt.shape
    nbs = x_sample.shape[0]
    n_ab = ab_in_w.shape[0]
    n_c = c_in_w.shape[0]
    nt = GDN_BLOCK

    mods = _mod_all(jnp.concatenate([c_prompt, c_sample], axis=0), mod_w, mod_b)
    abr, abi, bbr, bbi = _s5_prep(s5_a_re, s5_a_im, s5_log_dt, s5_b_re, s5_b_im)
    fnw = final_norm_w.reshape(1, D_MODEL)

    xp = _to_time_major(x_prompt)
    xs = x_sample.reshape(nbs, D_MODEL)
    p_states = [[] for _ in range(6)]
    s_states = [[] for _ in range(6)]
    for i in range(DEPTH):
        j = i // 2
        normw = norm_w[i].reshape(1, D_MODEL)
        mod_p = mods[i, 0:nbp]
        mod_s = mods[i, nbp:nbp + nbs]
        if i % 2 == 0:
            w = _ab_weights(j, ab_in_w, ab_out_w, conv_a_w, conv_a_b, lru_gx_w, lru_gx_b, lru_ga_w, lru_ga_b,
                            lru_a_param, conv_b_w, gdn_a_log, gdn_dt_bias, gdn_norm_w)
            xp, ca, lh, cb, ds = _ab_prompt(xp, mod_p, normw, w, nbp, nt)
            for lst, val in zip(p_states[:4], (_from_time_major(ca, nbp), lh, _from_time_major(cb, nbp), ds)):
                lst.append(val)
            xs, ca, lh, cb, ds = _ab_sample(xs, mod_s, normw, w, _to_time_major(state_conv_a[j]),
                                            state_lru[j], _to_time_major(state_conv_b[j]), state_delta[j])
            for lst, val in zip(s_states[:4], (_from_time_major(ca, nbs), lh, _from_time_major(cb, nbs), ds)):
                lst.append(val)
        else:
            w = dict(
                w_in=c_in_w[j].astype(BF16), bre=_block_diag_in(bbr[j]).astype(BF16),
                bim=_block_diag_in(bbi[j]).astype(BF16), cre=_block_diag_out(s5_c_re[j]).astype(BF16),
                cim=_block_diag_out(s5_c_im[j]).astype(BF16), abr=abr[j].reshape(1, NS),
                abi=abi[j].reshape(1, NS), d=s5_d[j].reshape(1, D_MODEL), gluw=glu_w[j].astype(BF16),
                glub=glu_b[j].reshape(1, D_MODEL), w_out=c_out_w[j].astype(BF16), fnw=fnw)
            last = i == DEPTH - 1
            xp, sr, si = _s5_layer(xp, mod_p, normw, w, nbp, nt, None, last)
            p_states[4].append(sr.reshape(nbp, G_C, P_C))
            p_states[5].append(si.reshape(nbp, G_C, P_C))
            xs, sr, si = _s5_layer(xs, mod_s, normw, w, nbs, 1,
                                   (state_s5_re[j].reshape(nbs, NS), state_s5_im[j].reshape(nbs, NS)), last)
            s_states[4].append(sr.reshape(nbs, G_C, P_C))
            s_states[5].append(si.reshape(nbs, G_C, P_C))
    y_prompt = _from_time_major(xp, nbp)
    y_sample = xs.reshape(nbs, 1, D_MODEL)
    stack = lambda lists: tuple(jnp.stack(l) for l in lists)
    return (y_prompt, y_sample) + stack(p_states) + stack(s_states)
```

```python
import functools

import jax
import jax.numpy as jnp
from jax import lax
from jax.experimental import pallas as pl
from jax.experimental.pallas import tpu as pltpu

F32 = jnp.float32
BF16 = jnp.bfloat16

D_MODEL = 1024
DEPTH = 4
CONV_W = 4
W_A = 1024
H_A = 8
BW_A = 128
LRU_C = 8.0
H_B = 8
DK = 128
DV = 128
W_B = H_B * DV
QKV_B = 3 * W_B
CG = 16
G_C = 64
P_C = 64
NS = G_C * P_C
EPS = 1e-6
LANE = 128
GDN_BLOCK = 64
S5_LANE_BLOCKS = D_MODEL // LANE
S5_BLOCK_STATE = NS // S5_LANE_BLOCKS
VMEM_LIMIT = 58 * 1024 * 1024


def _dot(a, b):
    return jnp.dot(a.astype(BF16), b.astype(BF16), preferred_element_type=F32)


def _dot_nt(a, b):
    return lax.dot_general(a.astype(BF16), b.astype(BF16), (((1,), (1,)), ((), ())),
                           preferred_element_type=F32)


def _silu(x):
    return x * jax.nn.sigmoid(x)


def _softplus(x):
    return jnp.maximum(x, 0.0) + jnp.log1p(jnp.exp(-jnp.abs(x)))


def _expm1(x):
    return jnp.tanh(0.5 * x) * (jnp.exp(x) + 1.0)


def _norm_mod(x, normw, mod_ref, nb):
    rows = x.shape[0]
    ms = jnp.mean(x * x, axis=-1, keepdims=True)
    y = x * lax.rsqrt(ms + EPS) * normw
    shift = mod_ref[:, 0:D_MODEL]
    scale = mod_ref[:, D_MODEL:2 * D_MODEL]
    y3 = y.reshape(rows // nb, nb, D_MODEL)
    return (y3 * (1.0 + scale)[None] + shift[None]).reshape(rows, D_MODEL)


def _residual(x, out, mod_ref, nb):
    rows = x.shape[0]
    gate = mod_ref[:, 2 * D_MODEL:3 * D_MODEL]
    return x + (out.reshape(rows // nb, nb, D_MODEL) * gate[None]).reshape(rows, D_MODEL)


def _conv_inplace(ext, lane0, w_ref, w_lane0, nb, rows, post):
    sl = slice(lane0, lane0 + LANE)
    wl = slice(w_lane0, w_lane0 + LANE)
    acc = ext[0:rows, sl] * w_ref[0:1, wl]
    for j in range(1, CONV_W):
        acc = acc + ext[j * nb:j * nb + rows, sl] * w_ref[j:j + 1, wl]
    tail = ext[rows:rows + 3 * nb, sl]
    ext[3 * nb:3 * nb + rows, sl] = post(acc)
    ext[0:3 * nb, sl] = tail


def _l2norm(t):
    return t * lax.rsqrt(jnp.sum(t * t, axis=-1, keepdims=True) + EPS)


def _ab_pre(first, nb, nt, reset_first, x, mod_ref, normw_ref, w_in_ref, w_ba_ref, caw_ref, cab_ref,
            gxw_ref, gxb_ref, gaw_ref, gab_ref, apar_ref, cbw_ref, alog_ref, dtb_ref,
            ext_a, ext_b, za, zbs, la, lb, gsc, bsc, h_s):
    rows = nb * nt
    hb = _norm_mod(x, normw_ref[...], mod_ref, nb).astype(BF16)
    ext_a[3 * nb:3 * nb + rows, :] = _dot(hb, w_in_ref[:, 0:W_A])
    za[...] = _dot(hb, w_in_ref[:, W_A:2 * W_A])
    for n in range(QKV_B // 1024):
        part = _dot(hb, w_in_ref[:, 2 * W_A + n * 1024:2 * W_A + (n + 1) * 1024])
        for m in range(1024 // LANE):
            ext_b[n * (1024 // LANE) + m, 3 * nb:3 * nb + rows, :] = part[:, m * LANE:(m + 1) * LANE]
    part = _dot(hb, w_in_ref[:, 2 * W_A + QKV_B:2 * W_A + QKV_B + W_B])
    for m in range(H_B):
        zbs[m] = part[:, m * LANE:(m + 1) * LANE]
    ba = _dot(hb, w_ba_ref[...])
    bsc[...] = jax.nn.sigmoid(ba[:, 0:LANE])
    gsc[...] = -jnp.exp(alog_ref[...]) * _softplus(ba[:, LANE:2 * LANE] + dtb_ref[...])

    for blk in range(H_A):
        sl = slice(blk * BW_A, (blk + 1) * BW_A)
        _conv_inplace(ext_a, blk * BW_A, caw_ref, blk * BW_A, nb, rows,
                      lambda acc, sl=sl: acc + cab_ref[:, sl])
        xb = ext_a[3 * nb:3 * nb + rows, sl]
        xbb = xb.astype(BF16)
        gate_x = jax.nn.sigmoid(_dot(xbb, gxw_ref[blk]) + gxb_ref[:, sl])
        gate_a = jax.nn.sigmoid(_dot(xbb, gaw_ref[blk]) + gab_ref[:, sl])
        log_a = -LRU_C * gate_a * _softplus(-apar_ref[:, sl])
        mult = jnp.sqrt(-_expm1(2.0 * log_a))
        if reset_first:
            row = lax.broadcasted_iota(jnp.int32, (rows, BW_A), 0)
            mult = jnp.where(jnp.logical_and(first, row < nb), 1.0, mult)
        la[:, sl] = jnp.exp(log_a)
        lb[:, sl] = mult * gate_x * xb

    def lru_step(t, h):
        r = pl.multiple_of(t * nb, nb)
        h = la[pl.ds(r, nb), :] * h + lb[pl.ds(r, nb), :]
        lb[pl.ds(r, nb), :] = h
        return h

    if nt == 1:
        h = la[...] * h_s[...] + lb[...]
        lb[...] = h
        h_s[...] = h
    else:
        h_s[...] = lax.fori_loop(0, nt, lru_step, h_s[...])

    for n in range(QKV_B // LANE):
        if n < H_B:
            post = lambda acc: _l2norm(_silu(acc)) * (DK ** -0.5)
        elif n < 2 * H_B:
            post = lambda acc: _l2norm(_silu(acc))
        else:
            post = _silu
        _conv_inplace(ext_b.at[n], 0, cbw_ref, n * LANE, nb, rows, post)


def _gdn_block_masks():
    n = 2 * GDN_BLOCK
    ri = lax.broadcasted_iota(jnp.int32, (n, n), 0)
    ci = lax.broadcasted_iota(jnp.int32, (n, n), 1)
    same = (ri >= GDN_BLOCK) == (ci >= GDN_BLOCK)
    tri = jnp.where(jnp.logical_and(same, ri >= ci), 1.0, 0.0).astype(F32)
    strict = jnp.where(jnp.logical_and(same, ri > ci), 1.0, 0.0).astype(F32)
    eye = jnp.where(ri == ci, 1.0, 0.0).astype(F32)
    levels = []
    for l in range(6):
        rb = ri >> l
        sub = jnp.logical_and((rb & 1) == 1, (ci >> l) == rb - 1)
        levels.append(jnp.where(jnp.logical_and(same, sub), 1.0, 0.0).astype(F32))
    return tri, strict, eye, levels


def _gdn_block(b, nb, ext_b, zbs, gsc, bsc, s_ref, gnw_ref, masks):
    c_len = GDN_BLOCK
    rows = pl.ds(3 * nb + b, c_len, stride=nb)
    zrows = pl.ds(b, c_len, stride=nb)
    gcb = gsc[zrows, :]
    betab = bsc[zrows, :]
    glast = gsc[pl.ds((c_len - 1) * nb + b, 1), :]
    top = lax.broadcasted_iota(jnp.int32, (2 * c_len, LANE), 0) < c_len
    tri, strict, eye, levels = masks

    def cat(a0, a1):
        return jnp.concatenate([a0, a1], axis=0)

    pairs = range(H_B // 2)

    def head_pair(base, p):
        return cat(ext_b[base + 2 * p, rows, :], ext_b[base + 2 * p + 1, rows, :])

    def col_pair(t, p):
        return cat(t[:, 2 * p:2 * p + 1], t[:, 2 * p + 1:2 * p + 2])

    q = [head_pair(0, p) for p in pairs]
    k = [head_pair(H_B, p) for p in pairs]
    v = [head_pair(2 * H_B, p) for p in pairs]
    c = [col_pair(gcb, p) for p in pairs]
    bcol = [col_pair(betab, p) for p in pairs]
    gl = [col_pair(jnp.broadcast_to(glast, (c_len, LANE)), p) for p in pairs]
    decay = []
    for p in pairs:
        cm = jnp.broadcast_to(c[p], (2 * c_len, 2 * c_len))
        decay.append(jnp.exp((cm - cm.T) * tri) * tri)
    kb = [k[p] * bcol[p] for p in pairs]
    a_mat = [_dot_nt(kb[p], k[p]) * decay[p] * strict for p in pairs]
    qk = [_dot_nt(q[p], k[p]) * decay[p] for p in pairs]
    x = [eye - a_mat[p] * levels[0] for p in pairs]
    for l in range(1, 6):
        t = [_dot(a_mat[p] * levels[l], x[p]) for p in pairs]
        x = [x[p] - _dot(x[p], t[p]) for p in pairs]
    sol = [_dot(x[p], jnp.concatenate([v[p] * bcol[p], kb[p] * jnp.exp(c[p])], axis=1)) for p in pairs]
    s0 = [s_ref[b, 2 * p] for p in pairs]
    s1 = [s_ref[b, 2 * p + 1] for p in pairs]
    sp = [jnp.concatenate([s0[p], s1[p]], axis=1) for p in pairs]
    ws = [_dot(sol[p][:, DV:2 * DV], sp[p]) for p in pairs]
    qs = [_dot(q[p] * jnp.exp(c[p]), sp[p]) for p in pairs]
    v_new = [sol[p][:, 0:DV] - jnp.where(top, ws[p][:, 0:DV], ws[p][:, DV:2 * DV]) for p in pairs]
    o = [jnp.where(top, qs[p][:, 0:DV], qs[p][:, DV:2 * DV]) + _dot(qk[p], v_new[p]) for p in pairs]
    upd = []
    for p in pairs:
        kdec = k[p] * jnp.exp(gl[p] - c[p])
        vblk = jnp.concatenate([jnp.where(top, v_new[p], 0.0), jnp.where(top, 0.0, v_new[p])], axis=1)
        upd.append(_dot(kdec.T, vblk))
    for p in pairs:
        h0, h1 = 2 * p, 2 * p + 1
        s_ref[b, h0] = s0[p] * jnp.exp(glast[:, h0:h0 + 1]) + upd[p][:, 0:DV]
        s_ref[b, h1] = s1[p] * jnp.exp(glast[:, h1:h1 + 1]) + upd[p][:, DV:2 * DV]
        on = o[p] * lax.rsqrt(jnp.mean(o[p] * o[p], axis=-1, keepdims=True) + EPS) * gnw_ref[...]
        zb = cat(zbs[h0, zrows, :], zbs[h1, zrows, :])
        og = on * _silu(zb)
        ext_b[h0, rows, :] = og[0:c_len]
        ext_b[h1, rows, :] = og[c_len:2 * c_len]


def _ab_post(x, nb, mod_ref, ya, o, w_out_ref):
    out = _dot(ya.astype(BF16), w_out_ref[0:W_A, :]) + _dot(o.astype(BF16), w_out_ref[W_A:W_A + W_B, :])
    return _residual(x, out, mod_ref, nb)


def _ab_prompt_kernel(nb, nt, x_ref, mod_ref, normw_ref, w_in_ref, w_ba_ref, caw_ref, cab_ref, gxw_ref,
                      gxb_ref, gaw_ref, gab_ref, apar_ref, cbw_ref, alog_ref, dtb_ref, gnw_ref, w_out_ref,
                      y_ref, ca_out, lru_out, cb_out, s_ref,
                      ext_a, ext_b, za, zbs, la, lb, gsc, bsc, h_s):
    c = pl.program_id(0)
    rows = nb * nt

    @pl.when(c == 0)
    def _():
        ext_a[0:3 * nb, :] = jnp.zeros((3 * nb, W_A), F32)
        ext_b[:, 0:3 * nb, :] = jnp.zeros((QKV_B // LANE, 3 * nb, LANE), F32)
        h_s[...] = jnp.zeros(h_s.shape, F32)
        s_ref[...] = jnp.zeros(s_ref.shape, F32)

    x = x_ref[...]
    _ab_pre(c == 0, nb, nt, True, x, mod_ref, normw_ref, w_in_ref, w_ba_ref, caw_ref, cab_ref, gxw_ref,
            gxb_ref, gaw_ref, gab_ref, apar_ref, cbw_ref, alog_ref, dtb_ref,
            ext_a, ext_b, za, zbs, la, lb, gsc, bsc, h_s)

    def cum_step(t, acc):
        r = pl.multiple_of(t * nb, nb)
        acc = acc + gsc[pl.ds(r, nb), :]
        gsc[pl.ds(r, nb), :] = acc
        return acc

    lax.fori_loop(0, nt, cum_step, jnp.zeros((nb, LANE), F32))
    masks = _gdn_block_masks()

    def per_seq(b, carry):
        _gdn_block(b, nb, ext_b, zbs, gsc, bsc, s_ref, gnw_ref, masks)
        return carry

    lax.fori_loop(0, nb, per_seq, 0)

    ya = lb[...] * _silu(za[...])
    o = jnp.concatenate([ext_b[h, 3 * nb:3 * nb + rows, :] for h in range(H_B)], axis=1)
    y_ref[...] = _ab_post(x, nb, mod_ref, ya, o, w_out_ref)

    @pl.when(c == pl.num_programs(0) - 1)
    def _():
        ca_out[...] = ext_a[0:3 * nb, :]
        for n in range(QKV_B // LANE):
            cb_out[:, n * LANE:(n + 1) * LANE] = ext_b[n, 0:3 * nb, :]
        lru_out[...] = h_s[...]


def _ab_sample_pre_kernel(nb, x_ref, mod_ref, normw_ref, w_in_ref, w_ba_ref, caw_ref, cab_ref, gxw_ref,
                          gxb_ref, gaw_ref, gab_ref, apar_ref, cbw_ref, alog_ref, dtb_ref,
                          ca_in, lru_in, cb_in,
                          qkv_out, g_out, beta_out, zb_out, ya_out, ca_out, lru_out, cb_out,
                          ext_a, ext_b, za, zbs, la, lb, gsc, bsc, h_s):
    ext_a[0:3 * nb, :] = ca_in[...]
    for n in range(QKV_B // LANE):
        ext_b[n, 0:3 * nb, :] = cb_in[:, n * LANE:(n + 1) * LANE]
    h_s[...] = lru_in[...]
    _ab_pre(False, nb, 1, False, x_ref[...], mod_ref, normw_ref, w_in_ref, w_ba_ref, caw_ref, cab_ref,
            gxw_ref, gxb_ref, gaw_ref, gab_ref, apar_ref, cbw_ref, alog_ref, dtb_ref,
            ext_a, ext_b, za, zbs, la, lb, gsc, bsc, h_s)
    for n in range(QKV_B // LANE):
        qkv_out[:, n * LANE:(n + 1) * LANE] = ext_b[n, 3 * nb:4 * nb, :]
        cb_out[:, n * LANE:(n + 1) * LANE] = ext_b[n, 0:3 * nb, :]
    g_out[...] = gsc[...]
    beta_out[...] = bsc[...]
    for h in range(H_B):
        zb_out[:, h * LANE:(h + 1) * LANE] = zbs[h]
    ya_out[...] = lb[...] * _silu(za[...])
    ca_out[...] = ext_a[0:3 * nb, :]
    lru_out[...] = h_s[...]


def _gdn_step_kernel(bb, qkv_ref, g_ref, beta_ref, zb_ref, gnw_ref, s_in, o_ref, s_out):
    def per_seq(i, carry):
        g_row = g_ref[i]
        beta_row = beta_ref[i]
        for h in range(H_B):
            q = qkv_ref[i, :, h * LANE:(h + 1) * LANE]
            k = qkv_ref[i, :, H_B * DK + h * LANE:H_B * DK + (h + 1) * LANE]
            v = qkv_ref[i, :, 2 * H_B * DK + h * LANE:2 * H_B * DK + (h + 1) * LANE]
            eg = jnp.exp(g_row[:, h:h + 1])
            beta = beta_row[:, h:h + 1]
            kcol = jnp.broadcast_to(k, (DK, DK)).T
            qcol = jnp.broadcast_to(q, (DK, DK)).T
            s = s_in[i, h]
            ks = jnp.sum(kcol * s, axis=0, keepdims=True)
            v_new = beta * (v - eg * ks)
            s_new = eg * s + kcol * v_new
            s_out[i, h] = s_new
            o = jnp.sum(qcol * s_new, axis=0, keepdims=True)
            on = o * lax.rsqrt(jnp.mean(o * o, axis=-1, keepdims=True) + EPS) * gnw_ref[...]
            zb = zb_ref[i, :, h * LANE:(h + 1) * LANE]
            o_ref[i, :, h * LANE:(h + 1) * LANE] = on * _silu(zb)
        return carry

    lax.fori_loop(0, bb, per_seq, 0)


def _ab_sample_post_kernel(nb, x_ref, mod_ref, ya_ref, o_ref, w_out_ref, y_ref):
    y_ref[...] = _ab_post(x_ref[...], nb, mod_ref, ya_ref[...], o_ref[...], w_out_ref)


def _s5_kernel(nb, nt, has_state, final_norm, *refs):
    (x_ref, mod_ref, normw_ref, w_in_ref, bre_ref, bim_ref, cre_ref, cim_ref, abr_ref, abi_ref, d_ref,
     gluw_ref, glub_ref, w_out_ref, fnw_ref) = refs[:15]
    refs = refs[15:]
    if has_state:
        sre_in, sim_in = refs[:2]
        refs = refs[2:]
    y_ref, sre_out, sim_out, xs_re, xs_im, y_s = refs
    c = pl.program_id(0)
    rows = nb * nt

    @pl.when(c == 0)
    def _():
        if has_state:
            xs_re[0:nb, :] = sre_in[...]
            xs_im[0:nb, :] = sim_in[...]
        else:
            xs_re[0:nb, :] = jnp.zeros((nb, NS), F32)
            xs_im[0:nb, :] = jnp.zeros((nb, NS), F32)

    x = x_ref[...]
    hb = _norm_mod(x, normw_ref[...], mod_ref, nb).astype(BF16)
    uz = _dot(hb, w_in_ref[...])
    u = uz[:, 0:D_MODEL]
    z = uz[:, D_MODEL:2 * D_MODEL]
    ub = u.astype(BF16)
    for j in range(S5_LANE_BLOCKS):
        sl = slice(j * S5_BLOCK_STATE, (j + 1) * S5_BLOCK_STATE)
        uj = ub[:, j * LANE:(j + 1) * LANE]
        xs_re[nb:nb + rows, sl] = _dot(uj, bre_ref[j])
        xs_im[nb:nb + rows, sl] = _dot(uj, bim_ref[j])

    for j in range(S5_LANE_BLOCKS):
        sl = slice(j * S5_BLOCK_STATE, (j + 1) * S5_BLOCK_STATE)
        ar = jnp.broadcast_to(abr_ref[:, sl], (nb, S5_BLOCK_STATE))
        ai = jnp.broadcast_to(abi_ref[:, sl], (nb, S5_BLOCK_STATE))

        def step(t, carry, sl=sl, ar=ar, ai=ai):
            sr, si = carry
            r = pl.multiple_of(nb + t * nb, nb)
            nr = ar * sr - ai * si + xs_re[pl.ds(r, nb), sl]
            ni = ar * si + ai * sr + xs_im[pl.ds(r, nb), sl]
            xs_re[pl.ds(r, nb), sl] = nr
            xs_im[pl.ds(r, nb), sl] = ni
            return nr, ni

        carry = (xs_re[0:nb, sl], xs_im[0:nb, sl])
        if nt == 1:
            sr, si = carry
            nr = ar * sr - ai * si + xs_re[nb:2 * nb, sl]
            ni = ar * si + ai * sr + xs_im[nb:2 * nb, sl]
            xs_re[nb:2 * nb, sl] = nr
            xs_im[nb:2 * nb, sl] = ni
        else:
            nr, ni = lax.fori_loop(0, nt, step, carry)
        xs_re[0:nb, sl] = nr
        xs_im[0:nb, sl] = ni

    for j in range(S5_LANE_BLOCKS):
        sl = slice(j * S5_BLOCK_STATE, (j + 1) * S5_BLOCK_STATE)
        cl = slice(j * LANE, (j + 1) * LANE)
        yj = (_dot(xs_re[nb:nb + rows, sl].astype(BF16), cre_ref[j])
              - _dot(xs_im[nb:nb + rows, sl].astype(BF16), cim_ref[j]))
        yj = yj + d_ref[:, cl] * u[:, cl]
        cdf = 0.5 * (1.0 + jnp.tanh(0.7978845608028654 * (yj + 0.044715 * (yj * yj * yj))))
        y_s[:, cl] = yj * cdf
    y = y_s[...]
    y = y * jax.nn.sigmoid(_dot(y.astype(BF16), gluw_ref[...]) + glub_ref[...])
    y = y * _silu(z)
    xn = _residual(x, _dot(y.astype(BF16), w_out_ref[...]), mod_ref, nb)
    if final_norm:
        xn = xn * lax.rsqrt(jnp.mean(xn * xn, axis=-1, keepdims=True) + EPS) * fnw_ref[...]
    y_ref[...] = xn

    @pl.when(c == pl.num_programs(0) - 1)
    def _():
        sre_out[...] = xs_re[0:nb, :]
        sim_out[...] = xs_im[0:nb, :]


def _s5_prep_kernel(are_ref, aim_ref, ldt_ref, bre_ref, bim_ref, abr_out, abi_out, bbr_out, bbi_out):
    a_re = are_ref[...]
    a_im = aim_ref[...]
    dt = jnp.exp(ldt_ref[...])
    mag = jnp.exp(a_re * dt)
    abr = mag * jnp.cos(a_im * dt)
    abi = mag * jnp.sin(a_im * dt)
    abr_out[...] = abr
    abi_out[...] = abi
    den = a_re * a_re + a_im * a_im
    nr = abr - 1.0
    cr = (nr * a_re + abi * a_im) / den
    ci = (abi * a_re - nr * a_im) / den
    b_re = bre_ref[...]
    b_im = bim_ref[...]
    bbr_out[...] = cr[:, None, :] * b_re - ci[:, None, :] * b_im
    bbi_out[...] = cr[:, None, :] * b_im + ci[:, None, :] * b_re


def _mod_kernel(c_ref, w_ref, b_ref, o_ref):
    c = c_ref[...]
    o_ref[...] = _dot(_silu(c).astype(BF16), w_ref[...].astype(BF16)) + b_ref[...]


def _full(shape):
    n = len(shape)
    return pl.BlockSpec(shape, lambda *_: (0,) * n)


def _const(shape):
    n = len(shape)
    return pl.BlockSpec(shape, lambda *_: (0,) * n, pipeline_mode=pl.Buffered(1))


def _params(n_grid):
    return pltpu.CompilerParams(dimension_semantics=("arbitrary",) * n_grid, vmem_limit_bytes=VMEM_LIMIT)


def _mod_all(c_all, mod_w, mod_b):
    n = c_all.shape[0]
    return pl.pallas_call(
        _mod_kernel,
        grid=(DEPTH, 3),
        in_specs=[pl.BlockSpec((n, D_MODEL), lambda i, j: (0, 0)),
                  pl.BlockSpec((None, D_MODEL, D_MODEL), lambda i, j: (i, 0, j)),
                  pl.BlockSpec((None, 1, D_MODEL), lambda i, j: (i, 0, j))],
        out_specs=pl.BlockSpec((None, n, D_MODEL), lambda i, j: (i, 0, j)),
        out_shape=jax.ShapeDtypeStruct((DEPTH, n, 3 * D_MODEL), F32),
        compiler_params=_params(2),
        name="mod_all",
    )(c_all, mod_w, mod_b.reshape(DEPTH, 1, 3 * D_MODEL))


def _ab_weights(j, ab_in_w, ab_out_w, conv_a_w, conv_a_b, lru_gx_w, lru_gx_b, lru_ga_w, lru_ga_b,
                lru_a_param, conv_b_w, gdn_a_log, gdn_dt_bias, gdn_norm_w):
    n_main = 2 * W_A + QKV_B + W_B
    w_in = ab_in_w[j, :, 0:n_main].astype(BF16)
    w_ba = jnp.zeros((D_MODEL, 2 * LANE), F32)
    w_ba = w_ba.at[:, 0:H_B].set(ab_in_w[j, :, n_main:n_main + H_B])
    w_ba = w_ba.at[:, LANE:LANE + H_B].set(ab_in_w[j, :, n_main + H_B:n_main + 2 * H_B]).astype(BF16)
    pad = lambda t: jnp.zeros((1, LANE), F32).at[0, 0:H_B].set(t)
    return dict(
        w_in=w_in, w_ba=w_ba, caw=conv_a_w[j], cab=conv_a_b[j].reshape(1, W_A),
        gxw=lru_gx_w[j].astype(BF16), gxb=lru_gx_b[j].reshape(1, W_A),
        gaw=lru_ga_w[j].astype(BF16), gab=lru_ga_b[j].reshape(1, W_A),
        apar=lru_a_param[j].reshape(1, W_A), cbw=conv_b_w[j], alog=pad(gdn_a_log[j]),
        dtb=pad(gdn_dt_bias[j]), gnw=gdn_norm_w[j].reshape(1, DV), w_out=ab_out_w[j].astype(BF16))


_AB_PRE_NAMES = ("w_in", "w_ba", "caw", "cab", "gxw", "gxb", "gaw", "gab", "apar", "cbw", "alog", "dtb")


def _ab_scratch(nb, rows):
    return [pltpu.VMEM((3 * nb + rows, W_A), F32), pltpu.VMEM((QKV_B // LANE, 3 * nb + rows, LANE), F32),
            pltpu.VMEM((rows, W_A), F32), pltpu.VMEM((H_B, rows, LANE), F32),
            pltpu.VMEM((rows, W_A), F32), pltpu.VMEM((rows, W_A), F32),
            pltpu.VMEM((rows, LANE), F32), pltpu.VMEM((rows, LANE), F32), pltpu.VMEM((nb, W_A), F32)]


def _ab_prompt(x, mod, normw, w, nb, nt):
    rows = nb * nt
    n_chunks = x.shape[0] // rows
    pre = [w[k] for k in _AB_PRE_NAMES]
    return pl.pallas_call(
        functools.partial(_ab_prompt_kernel, nb, nt),
        grid=(n_chunks,),
        in_specs=[pl.BlockSpec((rows, D_MODEL), lambda c: (c, 0)), _const(mod.shape), _const(normw.shape)]
        + [_const(t.shape) for t in pre] + [_const(w["gnw"].shape), _const(w["w_out"].shape)],
        out_specs=[pl.BlockSpec((rows, D_MODEL), lambda c: (c, 0)), _full((3 * nb, W_A)), _full((nb, W_A)),
                   _full((3 * nb, QKV_B)), _full((nb, H_B, DK, DV))],
        out_shape=[jax.ShapeDtypeStruct(x.shape, F32), jax.ShapeDtypeStruct((3 * nb, W_A), F32),
                   jax.ShapeDtypeStruct((nb, W_A), F32), jax.ShapeDtypeStruct((3 * nb, QKV_B), F32),
                   jax.ShapeDtypeStruct((nb, H_B, DK, DV), F32)],
        scratch_shapes=_ab_scratch(nb, rows),
        compiler_params=_params(1),
        name="ab_prompt",
    )(x, mod, normw, *pre, w["gnw"], w["w_out"])


def _ab_sample(x, mod, normw, w, conv_a, lru_h, conv_b, delta):
    nb = x.shape[0]
    pre = [w[k] for k in _AB_PRE_NAMES]
    ins = [x, mod, normw, *pre, conv_a, lru_h, conv_b]
    outs = [(nb, QKV_B), (nb, LANE), (nb, LANE), (nb, W_B), (nb, W_A), (3 * nb, W_A), (nb, W_A),
            (3 * nb, QKV_B)]
    qkv, g, beta, zb, ya, ca_new, lru_new, cb_new = pl.pallas_call(
        functools.partial(_ab_sample_pre_kernel, nb),
        in_specs=[_full(t.shape) for t in ins],
        out_specs=[_full(s) for s in outs],
        out_shape=[jax.ShapeDtypeStruct(s, F32) for s in outs],
        scratch_shapes=_ab_scratch(nb, nb),
        compiler_params=pltpu.CompilerParams(vmem_limit_bytes=VMEM_LIMIT),
        name="ab_sample_pre",
    )(*ins)
    bb = 8
    row_block = lambda width: pl.BlockSpec((bb, 1, width), lambda i: (i, 0, 0))
    state_block = pl.BlockSpec((bb, H_B, DK, DV), lambda i: (i, 0, 0, 0))
    o, delta_new = pl.pallas_call(
        functools.partial(_gdn_step_kernel, bb),
        grid=(nb // bb,),
        in_specs=[row_block(QKV_B), row_block(LANE), row_block(LANE), row_block(W_B), _full((1, DV)),
                  state_block],
        out_specs=[row_block(W_B), state_block],
        out_shape=[jax.ShapeDtypeStruct((nb, 1, W_B), F32), jax.ShapeDtypeStruct(delta.shape, F32)],
        compiler_params=_params(1),
        name="gdn_step",
    )(qkv.reshape(nb, 1, QKV_B), g.reshape(nb, 1, LANE), beta.reshape(nb, 1, LANE), zb.reshape(nb, 1, W_B),
      w["gnw"], delta)
    o = o.reshape(nb, W_B)
    y = pl.pallas_call(
        functools.partial(_ab_sample_post_kernel, nb),
        in_specs=[_full(x.shape), _full(mod.shape), _full(ya.shape), _full(o.shape), _full(w["w_out"].shape)],
        out_specs=_full(x.shape),
        out_shape=jax.ShapeDtypeStruct(x.shape, F32),
        compiler_params=pltpu.CompilerParams(vmem_limit_bytes=VMEM_LIMIT),
        name="ab_sample_post",
    )(x, mod, ya, o, w["w_out"])
    return y, ca_new, lru_new, cb_new, delta_new


def _s5_prep(s5_a_re, s5_a_im, s5_log_dt, s5_b_re, s5_b_im):
    n = s5_a_re.shape[0]
    gp = pl.BlockSpec((None, G_C, P_C), lambda i: (i, 0, 0))
    gcp = pl.BlockSpec((None, G_C, CG, P_C), lambda i: (i, 0, 0, 0))
    return pl.pallas_call(
        _s5_prep_kernel,
        grid=(n,),
        in_specs=[gp, gp, pl.BlockSpec((None, G_C, 1), lambda i: (i, 0, 0)), gcp, gcp],
        out_specs=[gp, gp, gcp, gcp],
        out_shape=[jax.ShapeDtypeStruct((n, G_C, P_C), F32)] * 2
        + [jax.ShapeDtypeStruct((n, G_C, CG, P_C), F32)] * 2,
        compiler_params=_params(1),
        name="s5_prep",
    )(s5_a_re, s5_a_im, s5_log_dt.reshape(n, G_C, 1), jnp.swapaxes(s5_b_re, 2, 3), jnp.swapaxes(s5_b_im, 2, 3))


def _block_diag_in(t):
    gl = G_C // S5_LANE_BLOCKS
    t = t.reshape(S5_LANE_BLOCKS, gl, CG, P_C)
    eye = jnp.eye(gl, dtype=t.dtype)
    return jnp.einsum("jgcp,gh->jgchp", t, eye).reshape(S5_LANE_BLOCKS, gl * CG, gl * P_C)


def _block_diag_out(t):
    gl = G_C // S5_LANE_BLOCKS
    t = t.reshape(S5_LANE_BLOCKS, gl, CG, P_C)
    eye = jnp.eye(gl, dtype=t.dtype)
    return jnp.einsum("jgcp,gh->jgphc", t, eye).reshape(S5_LANE_BLOCKS, gl * P_C, gl * CG)


def _s5_layer(x, mod, normw, w, nb, nt, state, final_norm):
    rows = nb * nt
    n_chunks = x.shape[0] // rows
    consts = [mod, normw, w["w_in"], w["bre"], w["bim"], w["cre"], w["cim"], w["abr"], w["abi"], w["d"],
              w["gluw"], w["glub"], w["w_out"], w["fnw"]]
    ins = [x, *consts]
    in_specs = [pl.BlockSpec((rows, D_MODEL), lambda c: (c, 0))] + [_const(t.shape) for t in consts]
    if state is not None:
        ins += list(state)
        in_specs += [_const((nb, NS))] * 2
    return pl.pallas_call(
        functools.partial(_s5_kernel, nb, nt, state is not None, final_norm),
        grid=(n_chunks,),
        in_specs=in_specs,
        out_specs=[pl.BlockSpec((rows, D_MODEL), lambda c: (c, 0)), _full((nb, NS)), _full((nb, NS))],
        out_shape=[jax.ShapeDtypeStruct(x.shape, F32), jax.ShapeDtypeStruct((nb, NS), F32),
                   jax.ShapeDtypeStruct((nb, NS), F32)],
        scratch_shapes=[pltpu.VMEM((nb + rows, NS), F32), pltpu.VMEM((nb + rows, NS), F32),
                        pltpu.VMEM((rows, D_MODEL), F32)],
        compiler_params=_params(1),
        name="s5_layer",
    )(*ins)


def _to_time_major(t):
    b, l, c = t.shape
    return jnp.swapaxes(t, 0, 1).reshape(l * b, c)


def _from_time_major(t, b):
    lb, c = t.shape
    return jnp.swapaxes(t.reshape(lb // b, b, c), 0, 1)


def kernel(x_prompt, x_sample, c_prompt, c_sample, state_conv_a, state_lru, state_conv_b, state_delta, state_s5_re, state_s5_im, norm_w, mod_w, mod_b, ab_in_w, ab_out_w, conv_a_w, conv_a_b, lru_gx_w, lru_gx_b, lru_ga_w, lru_ga_b, lru_a_param, conv_b_w, gdn_a_log, gdn_dt_bias, gdn_norm_w, c_in_w, c_out_w, s5_a_re, s5_a_im, s5_b_re, s5_b_im, s5_c_re, s5_c_im, s5_d, s5_log_dt, glu_w, glu_b, final_norm_w):
    nbp, seq, _ = x_prompt.shape
    nbs = x_sample.shape[0]
    n_ab = ab_in_w.shape[0]
    n_c = c_in_w.shape[0]
    nt = GDN_BLOCK

    mods = _mod_all(jnp.concatenate([c_prompt, c_sample], axis=0), mod_w, mod_b)
    abr, abi, bbr, bbi = _s5_prep(s5_a_re, s5_a_im, s5_log_dt, s5_b_re, s5_b_im)
    fnw = final_norm_w.reshape(1, D_MODEL)

    xp = _to_time_major(x_prompt)
    xs = x_sample.reshape(nbs, D_MODEL)
    p_states = [[] for _ in range(6)]
    s_states = [[] for _ in range(6)]
    for i in range(DEPTH):
        j = i // 2
        normw = norm_w[i].reshape(1, D_MODEL)
        mod_p = mods[i, 0:nbp]
        mod_s = mods[i, nbp:nbp + nbs]
        if i % 2 == 0:
            w = _ab_weights(j, ab_in_w, ab_out_w, conv_a_w, conv_a_b, lru_gx_w, lru_gx_b, lru_ga_w, lru_ga_b,
                            lru_a_param, conv_b_w, gdn_a_log, gdn_dt_bias, gdn_norm_w)
            xp, ca, lh, cb, ds = _ab_prompt(xp, mod_p, normw, w, nbp, nt)
            for lst, val in zip(p_states[:4], (_from_time_major(ca, nbp), lh, _from_time_major(cb, nbp), ds)):
                lst.append(val)
            xs, ca, lh, cb, ds = _ab_sample(xs, mod_s, normw, w, _to_time_major(state_conv_a[j]),
                                            state_lru[j], _to_time_major(state_conv_b[j]), state_delta[j])
            for lst, val in zip(s_states[:4], (_from_time_major(ca, nbs), lh, _from_time_major(cb, nbs), ds)):
                lst.append(val)
        else:
            w = dict(
                w_in=c_in_w[j].astype(BF16), bre=_block_diag_in(bbr[j]).astype(BF16),
                bim=_block_diag_in(bbi[j]).astype(BF16), cre=_block_diag_out(s5_c_re[j]).astype(BF16),
                cim=_block_diag_out(s5_c_im[j]).astype(BF16), abr=abr[j].reshape(1, NS),
                abi=abi[j].reshape(1, NS), d=s5_d[j].reshape(1, D_MODEL), gluw=glu_w[j].astype(BF16),
                glub=glu_b[j].reshape(1, D_MODEL), w_out=c_out_w[j].astype(BF16), fnw=fnw)
            last = i == DEPTH - 1
            xp, sr, si = _s5_layer(xp, mod_p, normw, w, nbp, nt, None, last)
            p_states[4].append(sr.reshape(nbp, G_C, P_C))
            p_states[5].append(si.reshape(nbp, G_C, P_C))
            xs, sr, si = _s5_layer(xs, mod_s, normw, w, nbs, 1,
                                   (state_s5_re[j].reshape(nbs, NS), state_s5_im[j].reshape(nbs, NS)), last)
            s_states[4].append(sr.reshape(nbs, G_C, P_C))
            s_states[5].append(si.reshape(nbs, G_C, P_C))
    y_prompt = _from_time_major(xp, nbp)
    y_sample = xs.reshape(nbs, 1, D_MODEL)
    stack = lambda lists: tuple(jnp.stack(l) for l in lists)
    return (y_prompt, y_sample) + stack(p_states) + stack(s_states)
```

```python
import functools

import jax
import jax.numpy as jnp
from jax import lax
from jax.experimental import pallas as pl
from jax.experimental.pallas import tpu as pltpu

F32 = jnp.float32
BF16 = jnp.bfloat16

D_MODEL = 1024
DEPTH = 4
CONV_W = 4
W_A = 1024
H_A = 8
BW_A = 128
LRU_C = 8.0
H_B = 8
DK = 128
DV = 128
W_B = H_B * DV
QKV_B = 3 * W_B
CG = 16
G_C = 64
P_C = 64
NS = G_C * P_C
EPS = 1e-6
LANE = 128
GDN_BLOCK = 64
S5_LANE_BLOCKS = D_MODEL // LANE
S5_BLOCK_STATE = NS // S5_LANE_BLOCKS
VMEM_LIMIT = 58 * 1024 * 1024


def _dot(a, b):
    return jnp.dot(a.astype(BF16), b.astype(BF16), preferred_element_type=F32)


def _dot_nt(a, b):
    return lax.dot_general(a.astype(BF16), b.astype(BF16), (((1,), (1,)), ((), ())),
                           preferred_element_type=F32)


def _silu(x):
    return x * jax.nn.sigmoid(x)


def _softplus(x):
    return jnp.maximum(x, 0.0) + jnp.log1p(jnp.exp(-jnp.abs(x)))


def _expm1(x):
    return jnp.tanh(0.5 * x) * (jnp.exp(x) + 1.0)


def _norm_mod(x, normw, mod_ref, nb):
    rows = x.shape[0]
    ms = jnp.mean(x * x, axis=-1, keepdims=True)
    y = x * lax.rsqrt(ms + EPS) * normw
    shift = mod_ref[:, 0:D_MODEL]
    scale = mod_ref[:, D_MODEL:2 * D_MODEL]
    y3 = y.reshape(rows // nb, nb, D_MODEL)
    return (y3 * (1.0 + scale)[None] + shift[None]).reshape(rows, D_MODEL)


def _residual(x, out, mod_ref, nb):
    rows = x.shape[0]
    gate = mod_ref[:, 2 * D_MODEL:3 * D_MODEL]
    return x + (out.reshape(rows // nb, nb, D_MODEL) * gate[None]).reshape(rows, D_MODEL)


def _conv_inplace(ext, lane0, w_ref, w_lane0, nb, rows, post):
    sl = slice(lane0, lane0 + LANE)
    wl = slice(w_lane0, w_lane0 + LANE)
    acc = ext[0:rows, sl] * w_ref[0:1, wl]
    for j in range(1, CONV_W):
        acc = acc + ext[j * nb:j * nb + rows, sl] * w_ref[j:j + 1, wl]
    tail = ext[rows:rows + 3 * nb, sl]
    ext[3 * nb:3 * nb + rows, sl] = post(acc)
    ext[0:3 * nb, sl] = tail


def _l2norm(t):
    return t * lax.rsqrt(jnp.sum(t * t, axis=-1, keepdims=True) + EPS)


def _ab_pre(first, nb, nt, reset_first, x, mod_ref, normw_ref, w_in_ref, w_ba_ref, caw_ref, cab_ref,
            gxw_ref, gxb_ref, gaw_ref, gab_ref, apar_ref, cbw_ref, alog_ref, dtb_ref,
            ext_a, ext_b, za, zbs, la, lb, gsc, bsc, h_s):
    rows = nb * nt
    hb = _norm_mod(x, normw_ref[...], mod_ref, nb).astype(BF16)
    ext_a[3 * nb:3 * nb + rows, :] = _dot(hb, w_in_ref[:, 0:W_A])
    za[...] = _dot(hb, w_in_ref[:, W_A:2 * W_A])
    for n in range(QKV_B // 1024):
        part = _dot(hb, w_in_ref[:, 2 * W_A + n * 1024:2 * W_A + (n + 1) * 1024])
        for m in range(1024 // LANE):
            ext_b[n * (1024 // LANE) + m, 3 * nb:3 * nb + rows, :] = part[:, m * LANE:(m + 1) * LANE]
    part = _dot(hb, w_in_ref[:, 2 * W_A + QKV_B:2 * W_A + QKV_B + W_B])
    for m in range(H_B):
        zbs[m] = part[:, m * LANE:(m + 1) * LANE]
    ba = _dot(hb, w_ba_ref[...])
    bsc[...] = jax.nn.sigmoid(ba[:, 0:LANE])
    gsc[...] = -jnp.exp(alog_ref[...]) * _softplus(ba[:, LANE:2 * LANE] + dtb_ref[...])

    for blk in range(H_A):
        sl = slice(blk * BW_A, (blk + 1) * BW_A)
        _conv_inplace(ext_a, blk * BW_A, caw_ref, blk * BW_A, nb, rows,
                      lambda acc, sl=sl: acc + cab_ref[:, sl])
        xb = ext_a[3 * nb:3 * nb + rows, sl]
        xbb = xb.astype(BF16)
        gate_x = jax.nn.sigmoid(_dot(xbb, gxw_ref[blk]) + gxb_ref[:, sl])
        gate_a = jax.nn.sigmoid(_dot(xbb, gaw_ref[blk]) + gab_ref[:, sl])
        log_a = -LRU_C * gate_a * _softplus(-apar_ref[:, sl])
        mult = jnp.sqrt(-_expm1(2.0 * log_a))
        if reset_first:
            row = lax.broadcasted_iota(jnp.int32, (rows, BW_A), 0)
            mult = jnp.where(jnp.logical_and(first, row < nb), 1.0, mult)
        la[:, sl] = jnp.exp(log_a)
        lb[:, sl] = mult * gate_x * xb

    def lru_step(t, h):
        r = pl.multiple_of(t * nb, nb)
        h = la[pl.ds(r, nb), :] * h + lb[pl.ds(r, nb), :]
        lb[pl.ds(r, nb), :] = h
        return h

    if nt == 1:
        h = la[...] * h_s[...] + lb[...]
        lb[...] = h
        h_s[...] = h
    else:
        h_s[...] = lax.fori_loop(0, nt, lru_step, h_s[...])

    for n in range(QKV_B // LANE):
        if n < H_B:
            post = lambda acc: _l2norm(_silu(acc)) * (DK ** -0.5)
        elif n < 2 * H_B:
            post = lambda acc: _l2norm(_silu(acc))
        else:
            post = _silu
        _conv_inplace(ext_b.at[n], 0, cbw_ref, n * LANE, nb, rows, post)


def _gdn_block_masks():
    n = 2 * GDN_BLOCK
    ri = lax.broadcasted_iota(jnp.int32, (n, n), 0)
    ci = lax.broadcasted_iota(jnp.int32, (n, n), 1)
    same = (ri >= GDN_BLOCK) == (ci >= GDN_BLOCK)
    tri = jnp.where(jnp.logical_and(same, ri >= ci), 1.0, 0.0).astype(F32)
    strict = jnp.where(jnp.logical_and(same, ri > ci), 1.0, 0.0).astype(F32)
    eye = jnp.where(ri == ci, 1.0, 0.0).astype(F32)
    levels = []
    for l in range(6):
        rb = ri >> l
        sub = jnp.logical_and((rb & 1) == 1, (ci >> l) == rb - 1)
        levels.append(jnp.where(jnp.logical_and(same, sub), 1.0, 0.0).astype(F32))
    return tri, strict, eye, levels


def _gdn_block(seqs, nb, ext_b, zbs, gsc, bsc, s_ref, gnw_ref, masks):
    c_len = GDN_BLOCK
    top = lax.broadcasted_iota(jnp.int32, (2 * c_len, LANE), 0) < c_len
    tri, strict, eye, levels = masks
    chains = [(si, p) for si in range(len(seqs)) for p in range(H_B // 2)]
    n = range(len(chains))

    def cat(a0, a1):
        return jnp.concatenate([a0, a1], axis=0)

    rows = [pl.ds(3 * nb + b, c_len, stride=nb) for b in seqs]
    zrows = [pl.ds(b, c_len, stride=nb) for b in seqs]

    def head_pair(base, si, p):
        return cat(ext_b[base + 2 * p, rows[si], :], ext_b[base + 2 * p + 1, rows[si], :])

    def col_pair(t, p):
        return cat(t[:, 2 * p:2 * p + 1], t[:, 2 * p + 1:2 * p + 2])

    gcb = [gsc[zr, :] for zr in zrows]
    betab = [bsc[zr, :] for zr in zrows]
    glast = [gsc[pl.ds((c_len - 1) * nb + b, 1), :] for b in seqs]
    q = [head_pair(0, si, p) for si, p in chains]
    k = [head_pair(H_B, si, p) for si, p in chains]
    v = [head_pair(2 * H_B, si, p) for si, p in chains]
    c = [col_pair(gcb[si], p) for si, p in chains]
    bcol = [col_pair(betab[si], p) for si, p in chains]
    gl = [col_pair(jnp.broadcast_to(glast[si], (c_len, LANE)), p) for si, p in chains]
    decay = []
    for i in n:
        cm = jnp.broadcast_to(c[i], (2 * c_len, 2 * c_len))
        decay.append(jnp.exp((cm - cm.T) * tri) * tri)
    kb = [k[i] * bcol[i] for i in n]
    a_mat = [_dot_nt(kb[i], k[i]) * decay[i] * strict for i in n]
    qk = [_dot_nt(q[i], k[i]) * decay[i] for i in n]
    x = [eye - a_mat[i] * levels[0] for i in n]
    for l in range(1, 6):
        t = [_dot(a_mat[i] * levels[l], x[i]) for i in n]
        x = [x[i] - _dot(x[i], t[i]) for i in n]
    sol = [_dot(x[i], jnp.concatenate([v[i] * bcol[i], kb[i] * jnp.exp(c[i])], axis=1)) for i in n]
    s0 = [s_ref[seqs[si], 2 * p] for si, p in chains]
    s1 = [s_ref[seqs[si], 2 * p + 1] for si, p in chains]

    def per_head(lhs, i):
        return cat(_dot(lhs[0:c_len], s0[i]), _dot(lhs[c_len:2 * c_len], s1[i]))

    ws = [per_head(sol[i][:, DV:2 * DV], i) for i in n]
    qs = [per_head(q[i] * jnp.exp(c[i]), i) for i in n]
    v_new = [sol[i][:, 0:DV] - ws[i] for i in n]
    o = [qs[i] + _dot(qk[i], v_new[i]) for i in n]
    upd = []
    for i in n:
        kdec = k[i] * jnp.exp(gl[i] - c[i])
        vblk = jnp.concatenate([jnp.where(top, v_new[i], 0.0), jnp.where(top, 0.0, v_new[i])], axis=1)
        upd.append(_dot(kdec.T, vblk))
    for i, (si, p) in enumerate(chains):
        b = seqs[si]
        h0, h1 = 2 * p, 2 * p + 1
        s_ref[b, h0] = s0[i] * jnp.exp(glast[si][:, h0:h0 + 1]) + upd[i][:, 0:DV]
        s_ref[b, h1] = s1[i] * jnp.exp(glast[si][:, h1:h1 + 1]) + upd[i][:, DV:2 * DV]
        on = o[i] * lax.rsqrt(jnp.mean(o[i] * o[i], axis=-1, keepdims=True) + EPS) * gnw_ref[...]
        zb = cat(zbs[h0, zrows[si], :], zbs[h1, zrows[si], :])
        og = on * _silu(zb)
        ext_b[h0, rows[si], :] = og[0:c_len]
        ext_b[h1, rows[si], :] = og[c_len:2 * c_len]


def _ab_post(x, nb, mod_ref, ya, o, w_out_ref):
    out = _dot(ya.astype(BF16), w_out_ref[0:W_A, :]) + _dot(o.astype(BF16), w_out_ref[W_A:W_A + W_B, :])
    return _residual(x, out, mod_ref, nb)


def _to_time_major_rows(x_ref, x_tm, nb, nt):
    for b in range(nb):
        for m in range(D_MODEL // LANE):
            x_tm[m, pl.ds(b, nt, stride=nb), :] = x_ref[b, :, m * LANE:(m + 1) * LANE]
    return jnp.concatenate([x_tm[m] for m in range(D_MODEL // LANE)], axis=1)


def _from_time_major_rows(y, y_ref, y_tm, nb, nt):
    for m in range(D_MODEL // LANE):
        y_tm[m] = y[:, m * LANE:(m + 1) * LANE]
    for b in range(nb):
        for m in range(D_MODEL // LANE):
            y_ref[b, :, m * LANE:(m + 1) * LANE] = y_tm[m, pl.ds(b, nt, stride=nb), :]


def _ab_prompt_kernel(nb, nt, batch_major_in, x_ref, mod_ref, normw_ref, w_in_ref, w_ba_ref, caw_ref, cab_ref,
                      gxw_ref, gxb_ref, gaw_ref, gab_ref, apar_ref, cbw_ref, alog_ref, dtb_ref, gnw_ref,
                      w_out_ref, y_ref, ca_out, lru_out, cb_out, s_ref,
                      ext_a, ext_b, za, zbs, la, lb, gsc, bsc, h_s, *x_tm):
    c = pl.program_id(0)
    rows = nb * nt

    @pl.when(c == 0)
    def _():
        ext_a[0:3 * nb, :] = jnp.zeros((3 * nb, W_A), F32)
        ext_b[:, 0:3 * nb, :] = jnp.zeros((QKV_B // LANE, 3 * nb, LANE), F32)
        h_s[...] = jnp.zeros(h_s.shape, F32)
        s_ref[...] = jnp.zeros(s_ref.shape, F32)

    x = _to_time_major_rows(x_ref, x_tm[0], nb, nt) if batch_major_in else x_ref[...]
    _ab_pre(c == 0, nb, nt, True, x, mod_ref, normw_ref, w_in_ref, w_ba_ref, caw_ref, cab_ref, gxw_ref,
            gxb_ref, gaw_ref, gab_ref, apar_ref, cbw_ref, alog_ref, dtb_ref,
            ext_a, ext_b, za, zbs, la, lb, gsc, bsc, h_s)

    def cum_step(t, acc):
        r = pl.multiple_of(t * nb, nb)
        acc = acc + gsc[pl.ds(r, nb), :]
        gsc[pl.ds(r, nb), :] = acc
        return acc

    lax.fori_loop(0, nt, cum_step, jnp.zeros((nb, LANE), F32))
    masks = _gdn_block_masks()

    def per_seq_pair(i, carry):
        _gdn_block([2 * i, 2 * i + 1], nb, ext_b, zbs, gsc, bsc, s_ref, gnw_ref, masks)
        return carry

    lax.fori_loop(0, nb // 2, per_seq_pair, 0)

    ya = lb[...] * _silu(za[...])
    o = jnp.concatenate([ext_b[h, 3 * nb:3 * nb + rows, :] for h in range(H_B)], axis=1)
    y_ref[...] = _ab_post(x, nb, mod_ref, ya, o, w_out_ref)

    @pl.when(c == pl.num_programs(0) - 1)
    def _():
        ca_out[...] = ext_a[0:3 * nb, :]
        for n in range(QKV_B // LANE):
            cb_out[:, n * LANE:(n + 1) * LANE] = ext_b[n, 0:3 * nb, :]
        lru_out[...] = h_s[...]


def _ab_sample_pre_kernel(nb, x_ref, mod_ref, normw_ref, w_in_ref, w_ba_ref, caw_ref, cab_ref, gxw_ref,
                          gxb_ref, gaw_ref, gab_ref, apar_ref, cbw_ref, alog_ref, dtb_ref,
                          ca_in, lru_in, cb_in,
                          qkv_out, g_out, beta_out, zb_out, ya_out, ca_out, lru_out, cb_out,
                          ext_a, ext_b, za, zbs, la, lb, gsc, bsc, h_s):
    ext_a[0:3 * nb, :] = ca_in[...]
    for n in range(QKV_B // LANE):
        ext_b[n, 0:3 * nb, :] = cb_in[:, n * LANE:(n + 1) * LANE]
    h_s[...] = lru_in[...]
    _ab_pre(False, nb, 1, False, x_ref[...], mod_ref, normw_ref, w_in_ref, w_ba_ref, caw_ref, cab_ref,
            gxw_ref, gxb_ref, gaw_ref, gab_ref, apar_ref, cbw_ref, alog_ref, dtb_ref,
            ext_a, ext_b, za, zbs, la, lb, gsc, bsc, h_s)
    for n in range(QKV_B // LANE):
        qkv_out[:, n * LANE:(n + 1) * LANE] = ext_b[n, 3 * nb:4 * nb, :]
        cb_out[:, n * LANE:(n + 1) * LANE] = ext_b[n, 0:3 * nb, :]
    g_out[...] = gsc[...]
    beta_out[...] = bsc[...]
    for h in range(H_B):
        zb_out[:, h * LANE:(h + 1) * LANE] = zbs[h]
    ya_out[...] = lb[...] * _silu(za[...])
    ca_out[...] = ext_a[0:3 * nb, :]
    lru_out[...] = h_s[...]


def _gdn_step_kernel(bb, layer, n_layers, qkv_ref, g_ref, beta_ref, zb_ref, gnw_ref, s_in, *rest):
    if layer == 0:
        o_ref, s_all = rest
        for l in range(1, n_layers):
            s_all[l] = jnp.zeros(s_all.shape[1:], F32)
        s_out = s_all.at[0]
    else:
        _, o_ref, s_out = rest

    def per_seq(i, carry):
        g_row = g_ref[i]
        beta_row = beta_ref[i]
        for h in range(H_B):
            q = qkv_ref[i, :, h * LANE:(h + 1) * LANE]
            k = qkv_ref[i, :, H_B * DK + h * LANE:H_B * DK + (h + 1) * LANE]
            v = qkv_ref[i, :, 2 * H_B * DK + h * LANE:2 * H_B * DK + (h + 1) * LANE]
            eg = jnp.exp(g_row[:, h:h + 1])
            beta = beta_row[:, h:h + 1]
            kcol = jnp.broadcast_to(k, (DK, DK)).T
            qcol = jnp.broadcast_to(q, (DK, DK)).T
            s = s_in[i, h]
            ks = jnp.sum(kcol * s, axis=0, keepdims=True)
            v_new = beta * (v - eg * ks)
            s_new = eg * s + kcol * v_new
            s_out[i, h] = s_new
            o = jnp.sum(qcol * s_new, axis=0, keepdims=True)
            on = o * lax.rsqrt(jnp.mean(o * o, axis=-1, keepdims=True) + EPS) * gnw_ref[...]
            zb = zb_ref[i, :, h * LANE:(h + 1) * LANE]
            o_ref[i, :, h * LANE:(h + 1) * LANE] = on * _silu(zb)
        return carry

    lax.fori_loop(0, bb, per_seq, 0)


def _ab_sample_post_kernel(nb, x_ref, mod_ref, ya_ref, o_ref, w_out_ref, y_ref):
    y_ref[...] = _ab_post(x_ref[...], nb, mod_ref, ya_ref[...], o_ref[...], w_out_ref)


def _s5_kernel(nb, nt, has_state, final_norm, batch_major_out, *refs):
    (x_ref, mod_ref, normw_ref, w_in_ref, bre_ref, bim_ref, cre_ref, cim_ref, abr_ref, abi_ref, d_ref,
     gluw_ref, glub_ref, w_out_ref, fnw_ref) = refs[:15]
    refs = refs[15:]
    if has_state:
        sre_in, sim_in = refs[:2]
        refs = refs[2:]
    y_ref, sre_out, sim_out, xs_re, xs_im, y_s = refs[:6]
    c = pl.program_id(0)
    rows = nb * nt

    @pl.when(c == 0)
    def _():
        if has_state:
            xs_re[0:nb, :] = sre_in[...]
            xs_im[0:nb, :] = sim_in[...]
        else:
            xs_re[0:nb, :] = jnp.zeros((nb, NS), F32)
            xs_im[0:nb, :] = jnp.zeros((nb, NS), F32)

    x = x_ref[...]
    hb = _norm_mod(x, normw_ref[...], mod_ref, nb).astype(BF16)
    uz = _dot(hb, w_in_ref[...])
    u = uz[:, 0:D_MODEL]
    z = uz[:, D_MODEL:2 * D_MODEL]
    ub = u.astype(BF16)
    for j in range(S5_LANE_BLOCKS):
        sl = slice(j * S5_BLOCK_STATE, (j + 1) * S5_BLOCK_STATE)
        uj = ub[:, j * LANE:(j + 1) * LANE]
        xs_re[nb:nb + rows, sl] = _dot(uj, bre_ref[j])
        xs_im[nb:nb + rows, sl] = _dot(uj, bim_ref[j])

    for j in range(S5_LANE_BLOCKS):
        sl = slice(j * S5_BLOCK_STATE, (j + 1) * S5_BLOCK_STATE)
        ar = jnp.broadcast_to(abr_ref[:, sl], (nb, S5_BLOCK_STATE))
        ai = jnp.broadcast_to(abi_ref[:, sl], (nb, S5_BLOCK_STATE))

        def step(t, carry, sl=sl, ar=ar, ai=ai):
            sr, si = carry
            r = pl.multiple_of(nb + t * nb, nb)
            nr = ar * sr - ai * si + xs_re[pl.ds(r, nb), sl]
            ni = ar * si + ai * sr + xs_im[pl.ds(r, nb), sl]
            xs_re[pl.ds(r, nb), sl] = nr
            xs_im[pl.ds(r, nb), sl] = ni
            return nr, ni

        carry = (xs_re[0:nb, sl], xs_im[0:nb, sl])
        if nt == 1:
            sr, si = carry
            nr = ar * sr - ai * si + xs_re[nb:2 * nb, sl]
            ni = ar * si + ai * sr + xs_im[nb:2 * nb, sl]
            xs_re[nb:2 * nb, sl] = nr
            xs_im[nb:2 * nb, sl] = ni
        else:
            nr, ni = lax.fori_loop(0, nt, step, carry)
        xs_re[0:nb, sl] = nr
        xs_im[0:nb, sl] = ni

    for j in range(S5_LANE_BLOCKS):
        sl = slice(j * S5_BLOCK_STATE, (j + 1) * S5_BLOCK_STATE)
        cl = slice(j * LANE, (j + 1) * LANE)
        yj = (_dot(xs_re[nb:nb + rows, sl].astype(BF16), cre_ref[j])
              - _dot(xs_im[nb:nb + rows, sl].astype(BF16), cim_ref[j]))
        yj = yj + d_ref[:, cl] * u[:, cl]
        cdf = 0.5 * (1.0 + jnp.tanh(0.7978845608028654 * (yj + 0.044715 * (yj * yj * yj))))
        y_s[:, cl] = yj * cdf
    y = y_s[...]
    y = y * jax.nn.sigmoid(_dot(y.astype(BF16), gluw_ref[...]) + glub_ref[...])
    y = y * _silu(z)
    xn = _residual(x, _dot(y.astype(BF16), w_out_ref[...]), mod_ref, nb)
    if final_norm:
        xn = xn * lax.rsqrt(jnp.mean(xn * xn, axis=-1, keepdims=True) + EPS) * fnw_ref[...]
    if batch_major_out:
        _from_time_major_rows(xn, y_ref, refs[6], nb, nt)
    else:
        y_ref[...] = xn

    @pl.when(c == pl.num_programs(0) - 1)
    def _():
        sre_out[...] = xs_re[0:nb, :]
        sim_out[...] = xs_im[0:nb, :]


def _s5_prep_kernel(are_ref, aim_ref, ldt_ref, bre_ref, bim_ref, abr_out, abi_out, bbr_out, bbi_out):
    a_re = are_ref[...]
    a_im = aim_ref[...]
    dt = jnp.exp(ldt_ref[...])
    mag = jnp.exp(a_re * dt)
    abr = mag * jnp.cos(a_im * dt)
    abi = mag * jnp.sin(a_im * dt)
    abr_out[...] = abr
    abi_out[...] = abi
    den = a_re * a_re + a_im * a_im
    nr = abr - 1.0
    cr = (nr * a_re + abi * a_im) / den
    ci = (abi * a_re - nr * a_im) / den
    b_re = bre_ref[...]
    b_im = bim_ref[...]
    bbr_out[...] = cr[:, None, :] * b_re - ci[:, None, :] * b_im
    bbi_out[...] = cr[:, None, :] * b_im + ci[:, None, :] * b_re


def _mod_kernel(c_ref, w_ref, b_ref, o_ref):
    c = c_ref[...]
    o_ref[...] = _dot(_silu(c).astype(BF16), w_ref[...].astype(BF16)) + b_ref[...]


def _full(shape):
    n = len(shape)
    return pl.BlockSpec(shape, lambda *_: (0,) * n)


def _const(shape):
    n = len(shape)
    return pl.BlockSpec(shape, lambda *_: (0,) * n, pipeline_mode=pl.Buffered(1))


def _params(n_grid):
    return pltpu.CompilerParams(dimension_semantics=("arbitrary",) * n_grid, vmem_limit_bytes=VMEM_LIMIT)


def _mod_all(c_all, mod_w, mod_b):
    n = c_all.shape[0]
    return pl.pallas_call(
        _mod_kernel,
        grid=(DEPTH, 3),
        in_specs=[pl.BlockSpec((n, D_MODEL), lambda i, j: (0, 0)),
                  pl.BlockSpec((None, D_MODEL, D_MODEL), lambda i, j: (i, 0, j)),
                  pl.BlockSpec((None, 1, D_MODEL), lambda i, j: (i, 0, j))],
        out_specs=pl.BlockSpec((None, n, D_MODEL), lambda i, j: (i, 0, j)),
        out_shape=jax.ShapeDtypeStruct((DEPTH, n, 3 * D_MODEL), F32),
        compiler_params=_params(2),
        name="mod_all",
    )(c_all, mod_w, mod_b.reshape(DEPTH, 1, 3 * D_MODEL))


def _ab_weights(j, ab_in_w, ab_out_w, conv_a_w, conv_a_b, lru_gx_w, lru_gx_b, lru_ga_w, lru_ga_b,
                lru_a_param, conv_b_w, gdn_a_log, gdn_dt_bias, gdn_norm_w):
    n_main = 2 * W_A + QKV_B + W_B
    w_in = ab_in_w[j, :, 0:n_main].astype(BF16)
    w_ba = jnp.zeros((D_MODEL, 2 * LANE), F32)
    w_ba = w_ba.at[:, 0:H_B].set(ab_in_w[j, :, n_main:n_main + H_B])
    w_ba = w_ba.at[:, LANE:LANE + H_B].set(ab_in_w[j, :, n_main + H_B:n_main + 2 * H_B]).astype(BF16)
    pad = lambda t: jnp.zeros((1, LANE), F32).at[0, 0:H_B].set(t)
    return dict(
        w_in=w_in, w_ba=w_ba, caw=conv_a_w[j], cab=conv_a_b[j].reshape(1, W_A),
        gxw=lru_gx_w[j].astype(BF16), gxb=lru_gx_b[j].reshape(1, W_A),
        gaw=lru_ga_w[j].astype(BF16), gab=lru_ga_b[j].reshape(1, W_A),
        apar=lru_a_param[j].reshape(1, W_A), cbw=conv_b_w[j], alog=pad(gdn_a_log[j]),
        dtb=pad(gdn_dt_bias[j]), gnw=gdn_norm_w[j].reshape(1, DV), w_out=ab_out_w[j].astype(BF16))


_AB_PRE_NAMES = ("w_in", "w_ba", "caw", "cab", "gxw", "gxb", "gaw", "gab", "apar", "cbw", "alog", "dtb")


def _ab_scratch(nb, rows):
    return [pltpu.VMEM((3 * nb + rows, W_A), F32), pltpu.VMEM((QKV_B // LANE, 3 * nb + rows, LANE), F32),
            pltpu.VMEM((rows, W_A), F32), pltpu.VMEM((H_B, rows, LANE), F32),
            pltpu.VMEM((rows, W_A), F32), pltpu.VMEM((rows, W_A), F32),
            pltpu.VMEM((rows, LANE), F32), pltpu.VMEM((rows, LANE), F32), pltpu.VMEM((nb, W_A), F32)]


def _ab_prompt(x, mod, normw, w, nb, nt):
    rows = nb * nt
    batch_major = x.ndim == 3
    n_rows = x.shape[0] * x.shape[1] if batch_major else x.shape[0]
    n_chunks = n_rows // rows
    pre = [w[k] for k in _AB_PRE_NAMES]
    if batch_major:
        x_spec = pl.BlockSpec((nb, nt, D_MODEL), lambda c: (0, c, 0))
        extra = [pltpu.VMEM((D_MODEL // LANE, rows, LANE), F32)]
    else:
        x_spec = pl.BlockSpec((rows, D_MODEL), lambda c: (c, 0))
        extra = []
    return pl.pallas_call(
        functools.partial(_ab_prompt_kernel, nb, nt, batch_major),
        grid=(n_chunks,),
        in_specs=[x_spec, _const(mod.shape), _const(normw.shape)]
        + [_const(t.shape) for t in pre] + [_const(w["gnw"].shape), _const(w["w_out"].shape)],
        out_specs=[pl.BlockSpec((rows, D_MODEL), lambda c: (c, 0)), _full((3 * nb, W_A)), _full((nb, W_A)),
                   _full((3 * nb, QKV_B)), _full((nb, H_B, DK, DV))],
        out_shape=[jax.ShapeDtypeStruct((n_rows, D_MODEL), F32), jax.ShapeDtypeStruct((3 * nb, W_A), F32),
                   jax.ShapeDtypeStruct((nb, W_A), F32), jax.ShapeDtypeStruct((3 * nb, QKV_B), F32),
                   jax.ShapeDtypeStruct((nb, H_B, DK, DV), F32)],
        scratch_shapes=_ab_scratch(nb, rows) + extra,
        compiler_params=_params(1),
        name="ab_prompt",
    )(x, mod, normw, *pre, w["gnw"], w["w_out"])


def _ab_sample(x, mod, normw, w, conv_a, lru_h, conv_b, delta_all, layer, delta_new_all):
    nb = x.shape[0]
    pre = [w[k] for k in _AB_PRE_NAMES]
    ins = [x, mod, normw, *pre, conv_a, lru_h, conv_b]
    outs = [(nb, QKV_B), (nb, LANE), (nb, LANE), (nb, W_B), (nb, W_A), (3 * nb, W_A), (nb, W_A),
            (3 * nb, QKV_B)]
    qkv, g, beta, zb, ya, ca_new, lru_new, cb_new = pl.pallas_call(
        functools.partial(_ab_sample_pre_kernel, nb),
        in_specs=[_full(t.shape) for t in ins],
        out_specs=[_full(s) for s in outs],
        out_shape=[jax.ShapeDtypeStruct(s, F32) for s in outs],
        scratch_shapes=_ab_scratch(nb, nb),
        compiler_params=pltpu.CompilerParams(vmem_limit_bytes=VMEM_LIMIT),
        name="ab_sample_pre",
    )(*ins)
    bb = 8
    row_block = lambda width: pl.BlockSpec((bb, 1, width), lambda i: (i, 0, 0))
    n_layers = delta_all.shape[0]
    state_block = pl.BlockSpec((None, bb, H_B, DK, DV), lambda i: (layer, i, 0, 0, 0))
    step_ins = [qkv.reshape(nb, 1, QKV_B), g.reshape(nb, 1, LANE), beta.reshape(nb, 1, LANE),
                zb.reshape(nb, 1, W_B), w["gnw"], delta_all]
    step_specs = [row_block(QKV_B), row_block(LANE), row_block(LANE), row_block(W_B), _full((1, DV)),
                  state_block]
    if layer == 0:
        out_state_block = pl.BlockSpec((n_layers, bb, H_B, DK, DV), lambda i: (0, i, 0, 0, 0))
        aliases = {}
    else:
        step_ins.append(delta_new_all)
        step_specs.append(pl.BlockSpec(memory_space=pl.ANY))
        out_state_block = state_block
        aliases = {len(step_ins) - 1: 1}
    o, delta_new_all = pl.pallas_call(
        functools.partial(_gdn_step_kernel, bb, layer, n_layers),
        grid=(nb // bb,),
        in_specs=step_specs,
        out_specs=[row_block(W_B), out_state_block],
        out_shape=[jax.ShapeDtypeStruct((nb, 1, W_B), F32), jax.ShapeDtypeStruct(delta_all.shape, F32)],
        input_output_aliases=aliases,
        compiler_params=_params(1),
        name="gdn_step",
    )(*step_ins)
    o = o.reshape(nb, W_B)
    y = pl.pallas_call(
        functools.partial(_ab_sample_post_kernel, nb),
        in_specs=[_full(x.shape), _full(mod.shape), _full(ya.shape), _full(o.shape), _full(w["w_out"].shape)],
        out_specs=_full(x.shape),
        out_shape=jax.ShapeDtypeStruct(x.shape, F32),
        compiler_params=pltpu.CompilerParams(vmem_limit_bytes=VMEM_LIMIT),
        name="ab_sample_post",
    )(x, mod, ya, o, w["w_out"])
    return y, ca_new, lru_new, cb_new, delta_new_all


def _s5_prep(s5_a_re, s5_a_im, s5_log_dt, s5_b_re, s5_b_im):
    n = s5_a_re.shape[0]
    gp = pl.BlockSpec((None, G_C, P_C), lambda i: (i, 0, 0))
    gcp = pl.BlockSpec((None, G_C, CG, P_C), lambda i: (i, 0, 0, 0))
    return pl.pallas_call(
        _s5_prep_kernel,
        grid=(n,),
        in_specs=[gp, gp, pl.BlockSpec((None, G_C, 1), lambda i: (i, 0, 0)), gcp, gcp],
        out_specs=[gp, gp, gcp, gcp],
        out_shape=[jax.ShapeDtypeStruct((n, G_C, P_C), F32)] * 2
        + [jax.ShapeDtypeStruct((n, G_C, CG, P_C), F32)] * 2,
        compiler_params=_params(1),
        name="s5_prep",
    )(s5_a_re, s5_a_im, s5_log_dt.reshape(n, G_C, 1), jnp.swapaxes(s5_b_re, 2, 3), jnp.swapaxes(s5_b_im, 2, 3))


def _block_diag_in(t):
    gl = G_C // S5_LANE_BLOCKS
    t = t.reshape(S5_LANE_BLOCKS, gl, CG, P_C)
    eye = jnp.eye(gl, dtype=t.dtype)
    return jnp.einsum("jgcp,gh->jgchp", t, eye).reshape(S5_LANE_BLOCKS, gl * CG, gl * P_C)


def _block_diag_out(t):
    gl = G_C // S5_LANE_BLOCKS
    t = t.reshape(S5_LANE_BLOCKS, gl, CG, P_C)
    eye = jnp.eye(gl, dtype=t.dtype)
    return jnp.einsum("jgcp,gh->jgphc", t, eye).reshape(S5_LANE_BLOCKS, gl * P_C, gl * CG)


def _s5_layer(x, mod, normw, w, nb, nt, state, final_norm, batch_major_out=False):
    rows = nb * nt
    n_chunks = x.shape[0] // rows
    consts = [mod, normw, w["w_in"], w["bre"], w["bim"], w["cre"], w["cim"], w["abr"], w["abi"], w["d"],
              w["gluw"], w["glub"], w["w_out"], w["fnw"]]
    ins = [x, *consts]
    in_specs = [pl.BlockSpec((rows, D_MODEL), lambda c: (c, 0))] + [_const(t.shape) for t in consts]
    if state is not None:
        ins += list(state)
        in_specs += [_const((nb, NS))] * 2
    if batch_major_out:
        y_spec = pl.BlockSpec((nb, nt, D_MODEL), lambda c: (0, c, 0))
        y_shape = (nb, x.shape[0] // nb, D_MODEL)
        extra = [pltpu.VMEM((D_MODEL // LANE, rows, LANE), F32)]
    else:
        y_spec = pl.BlockSpec((rows, D_MODEL), lambda c: (c, 0))
        y_shape = x.shape
        extra = []
    return pl.pallas_call(
        functools.partial(_s5_kernel, nb, nt, state is not None, final_norm, batch_major_out),
        grid=(n_chunks,),
        in_specs=in_specs,
        out_specs=[y_spec, _full((nb, NS)), _full((nb, NS))],
        out_shape=[jax.ShapeDtypeStruct(y_shape, F32), jax.ShapeDtypeStruct((nb, NS), F32),
                   jax.ShapeDtypeStruct((nb, NS), F32)],
        scratch_shapes=[pltpu.VMEM((nb + rows, NS), F32), pltpu.VMEM((nb + rows, NS), F32),
                        pltpu.VMEM((rows, D_MODEL), F32)] + extra,
        compiler_params=_params(1),
        name="s5_layer",
    )(*ins)


def _to_time_major(t):
    b, l, c = t.shape
    return jnp.swapaxes(t, 0, 1).reshape(l * b, c)


def _from_time_major(t, b):
    lb, c = t.shape
    return jnp.swapaxes(t.reshape(lb // b, b, c), 0, 1)


def kernel(x_prompt, x_sample, c_prompt, c_sample, state_conv_a, state_lru, state_conv_b, state_delta, state_s5_re, state_s5_im, norm_w, mod_w, mod_b, ab_in_w, ab_out_w, conv_a_w, conv_a_b, lru_gx_w, lru_gx_b, lru_ga_w, lru_ga_b, lru_a_param, conv_b_w, gdn_a_log, gdn_dt_bias, gdn_norm_w, c_in_w, c_out_w, s5_a_re, s5_a_im, s5_b_re, s5_b_im, s5_c_re, s5_c_im, s5_d, s5_log_dt, glu_w, glu_b, final_norm_w):
    nbp, seq, _ = x_prompt.shape
    nbs = x_sample.shape[0]
    n_ab = ab_in_w.shape[0]
    n_c = c_in_w.shape[0]
    nt = GDN_BLOCK

    mods = _mod_all(jnp.concatenate([c_prompt, c_sample], axis=0), mod_w, mod_b)
    abr, abi, bbr, bbi = _s5_prep(s5_a_re, s5_a_im, s5_log_dt, s5_b_re, s5_b_im)
    fnw = final_norm_w.reshape(1, D_MODEL)

    xp = x_prompt
    xs = x_sample.reshape(nbs, D_MODEL)
    p_states = [[] for _ in range(6)]
    s_states = [[] for _ in range(6)]
    s_delta = None
    for i in range(DEPTH):
        j = i // 2
        normw = norm_w[i].reshape(1, D_MODEL)
        mod_p = mods[i, 0:nbp]
        mod_s = mods[i, nbp:nbp + nbs]
        if i % 2 == 0:
            w = _ab_weights(j, ab_in_w, ab_out_w, conv_a_w, conv_a_b, lru_gx_w, lru_gx_b, lru_ga_w, lru_ga_b,
                            lru_a_param, conv_b_w, gdn_a_log, gdn_dt_bias, gdn_norm_w)
            xp, ca, lh, cb, ds = _ab_prompt(xp, mod_p, normw, w, nbp, nt)
            for lst, val in zip(p_states[:4], (_from_time_major(ca, nbp), lh, _from_time_major(cb, nbp), ds)):
                lst.append(val)
            xs, ca, lh, cb, s_delta = _ab_sample(xs, mod_s, normw, w, _to_time_major(state_conv_a[j]),
                                                 state_lru[j], _to_time_major(state_conv_b[j]), state_delta, j,
                                                 s_delta)
            for lst, val in zip(s_states[:3], (_from_time_major(ca, nbs), lh, _from_time_major(cb, nbs))):
                lst.append(val)
        else:
            w = dict(
                w_in=c_in_w[j].astype(BF16), bre=_block_diag_in(bbr[j]).astype(BF16),
                bim=_block_diag_in(bbi[j]).astype(BF16), cre=_block_diag_out(s5_c_re[j]).astype(BF16),
                cim=_block_diag_out(s5_c_im[j]).astype(BF16), abr=abr[j].reshape(1, NS),
                abi=abi[j].reshape(1, NS), d=s5_d[j].reshape(1, D_MODEL), gluw=glu_w[j].astype(BF16),
                glub=glu_b[j].reshape(1, D_MODEL), w_out=c_out_w[j].astype(BF16), fnw=fnw)
            last = i == DEPTH - 1
            xp, sr, si = _s5_layer(xp, mod_p, normw, w, nbp, nt, None, last, batch_major_out=last)
            p_states[4].append(sr.reshape(nbp, G_C, P_C))
            p_states[5].append(si.reshape(nbp, G_C, P_C))
            xs, sr, si = _s5_layer(xs, mod_s, normw, w, nbs, 1,
                                   (state_s5_re[j].reshape(nbs, NS), state_s5_im[j].reshape(nbs, NS)), last)
            s_states[4].append(sr.reshape(nbs, G_C, P_C))
            s_states[5].append(si.reshape(nbs, G_C, P_C))
    y_prompt = xp
    y_sample = xs.reshape(nbs, 1, D_MODEL)
    stack = lambda lists: tuple(jnp.stack(l) for l in lists)
    s_out = stack(s_states[:3]) + (s_delta,) + stack(s_states[4:])
    return (y_prompt, y_sample) + stack(p_states) + s_out
```

```python
import functools

import jax
import jax.numpy as jnp
from jax import lax
from jax.experimental import pallas as pl
from jax.experimental.pallas import tpu as pltpu

F32 = jnp.float32
BF16 = jnp.bfloat16

D_MODEL = 1024
DEPTH = 4
CONV_W = 4
W_A = 1024
H_A = 8
BW_A = 128
LRU_C = 8.0
H_B = 8
DK = 128
DV = 128
W_B = H_B * DV
QKV_B = 3 * W_B
CG = 16
G_C = 64
P_C = 64
NS = G_C * P_C
EPS = 1e-6
LANE = 128
GDN_BLOCK = 64
GDN_SEQS = 4
S5_LANE_BLOCKS = D_MODEL // LANE
S5_BLOCK_STATE = NS // S5_LANE_BLOCKS
VMEM_LIMIT = 60 * 1024 * 1024


def _dot(a, b):
    return jnp.dot(a.astype(BF16), b.astype(BF16), preferred_element_type=F32)


def _dot_nt(a, b):
    return lax.dot_general(a.astype(BF16), b.astype(BF16), (((1,), (1,)), ((), ())),
                           preferred_element_type=F32)


def _silu(x):
    return x * jax.nn.sigmoid(x)


def _softplus(x):
    return jnp.maximum(x, 0.0) + jnp.log1p(jnp.exp(-jnp.abs(x)))


def _expm1(x):
    return jnp.tanh(0.5 * x) * (jnp.exp(x) + 1.0)


def _norm_mod(x, normw, mod_ref, nb):
    rows = x.shape[0]
    ms = jnp.mean(x * x, axis=-1, keepdims=True)
    y = x * lax.rsqrt(ms + EPS) * normw
    shift = mod_ref[:, 0:D_MODEL]
    scale = mod_ref[:, D_MODEL:2 * D_MODEL]
    y3 = y.reshape(rows // nb, nb, D_MODEL)
    return (y3 * (1.0 + scale)[None] + shift[None]).reshape(rows, D_MODEL)


def _residual(x, out, mod_ref, nb):
    rows = x.shape[0]
    gate = mod_ref[:, 2 * D_MODEL:3 * D_MODEL]
    return x + (out.reshape(rows // nb, nb, D_MODEL) * gate[None]).reshape(rows, D_MODEL)


def _conv_inplace(ext, lane0, w_ref, w_lane0, nb, rows, post):
    sl = slice(lane0, lane0 + LANE)
    wl = slice(w_lane0, w_lane0 + LANE)
    acc = ext[0:rows, sl] * w_ref[0:1, wl]
    for j in range(1, CONV_W):
        acc = acc + ext[j * nb:j * nb + rows, sl] * w_ref[j:j + 1, wl]
    tail = ext[rows:rows + 3 * nb, sl]
    ext[3 * nb:3 * nb + rows, sl] = post(acc)
    ext[0:3 * nb, sl] = tail


def _l2norm(t):
    return t * lax.rsqrt(jnp.sum(t * t, axis=-1, keepdims=True) + EPS)


def _ab_pre(first, nb, nt, reset_first, x, mod_ref, normw_ref, w_in_ref, w_ba_ref, caw_ref, cab_ref,
            gxw_ref, gxb_ref, gaw_ref, gab_ref, apar_ref, cbw_ref, alog_ref, dtb_ref,
            ext_a, ext_b, za, zbs, la, lb, gsc, bsc, h_s):
    rows = nb * nt
    hb = _norm_mod(x, normw_ref[...], mod_ref, nb).astype(BF16)
    ext_a[3 * nb:3 * nb + rows, :] = _dot(hb, w_in_ref[:, 0:W_A])
    za[...] = _dot(hb, w_in_ref[:, W_A:2 * W_A])
    for n in range(QKV_B // 1024):
        part = _dot(hb, w_in_ref[:, 2 * W_A + n * 1024:2 * W_A + (n + 1) * 1024])
        for m in range(1024 // LANE):
            ext_b[n * (1024 // LANE) + m, 3 * nb:3 * nb + rows, :] = part[:, m * LANE:(m + 1) * LANE]
    part = _dot(hb, w_in_ref[:, 2 * W_A + QKV_B:2 * W_A + QKV_B + W_B])
    for m in range(H_B):
        zbs[m] = part[:, m * LANE:(m + 1) * LANE]
    ba = _dot(hb, w_ba_ref[...])
    bsc[...] = jax.nn.sigmoid(ba[:, 0:LANE])
    gsc[...] = -jnp.exp(alog_ref[...]) * _softplus(ba[:, LANE:2 * LANE] + dtb_ref[...])

    for blk in range(H_A):
        sl = slice(blk * BW_A, (blk + 1) * BW_A)
        _conv_inplace(ext_a, blk * BW_A, caw_ref, blk * BW_A, nb, rows,
                      lambda acc, sl=sl: acc + cab_ref[:, sl])
        xb = ext_a[3 * nb:3 * nb + rows, sl]
        xbb = xb.astype(BF16)
        gate_x = jax.nn.sigmoid(_dot(xbb, gxw_ref[blk]) + gxb_ref[:, sl])
        gate_a = jax.nn.sigmoid(_dot(xbb, gaw_ref[blk]) + gab_ref[:, sl])
        log_a = -LRU_C * gate_a * _softplus(-apar_ref[:, sl])
        mult = jnp.sqrt(-_expm1(2.0 * log_a))
        if reset_first:
            row = lax.broadcasted_iota(jnp.int32, (rows, BW_A), 0)
            mult = jnp.where(jnp.logical_and(first, row < nb), 1.0, mult)
        la[:, sl] = jnp.exp(log_a)
        lb[:, sl] = mult * gate_x * xb

    def lru_step(t, h):
        r = pl.multiple_of(t * nb, nb)
        h = la[pl.ds(r, nb), :] * h + lb[pl.ds(r, nb), :]
        lb[pl.ds(r, nb), :] = h
        return h

    if nt == 1:
        h = la[...] * h_s[...] + lb[...]
        lb[...] = h
        h_s[...] = h
    else:
        h_s[...] = lax.fori_loop(0, nt, lru_step, h_s[...])

    for n in range(QKV_B // LANE):
        if n < H_B:
            post = lambda acc: _l2norm(_silu(acc)) * (DK ** -0.5)
        elif n < 2 * H_B:
            post = lambda acc: _l2norm(_silu(acc))
        else:
            post = _silu
        _conv_inplace(ext_b.at[n], 0, cbw_ref, n * LANE, nb, rows, post)


def _gdn_block_masks():
    n = 2 * GDN_BLOCK
    ri = lax.broadcasted_iota(jnp.int32, (n, n), 0)
    ci = lax.broadcasted_iota(jnp.int32, (n, n), 1)
    same = (ri >= GDN_BLOCK) == (ci >= GDN_BLOCK)
    tri = jnp.where(jnp.logical_and(same, ri >= ci), 1.0, 0.0).astype(F32)
    strict = jnp.where(jnp.logical_and(same, ri > ci), 1.0, 0.0).astype(F32)
    eye = jnp.where(ri == ci, 1.0, 0.0).astype(F32)
    levels = []
    for l in range(6):
        rb = ri >> l
        sub = jnp.logical_and((rb & 1) == 1, (ci >> l) == rb - 1)
        levels.append(jnp.where(jnp.logical_and(same, sub), 1.0, 0.0).astype(F32))
    return tri, strict, eye, levels


def _gdn_block(seqs, nb, ext_b, zbs, gsc, bsc, s_ref, gnw_ref, masks):
    c_len = GDN_BLOCK
    top = lax.broadcasted_iota(jnp.int32, (2 * c_len, LANE), 0) < c_len
    tri, strict, eye, levels = masks
    chains = [(si, p) for si in range(len(seqs)) for p in range(H_B // 2)]
    n = range(len(chains))

    def cat(a0, a1):
        return jnp.concatenate([a0, a1], axis=0)

    rows = [pl.ds(3 * nb + b, c_len, stride=nb) for b in seqs]
    zrows = [pl.ds(b, c_len, stride=nb) for b in seqs]

    def head_pair(base, si, p):
        return cat(ext_b[base + 2 * p, rows[si], :], ext_b[base + 2 * p + 1, rows[si], :])

    def col_pair(t, p):
        return cat(t[:, 2 * p:2 * p + 1], t[:, 2 * p + 1:2 * p + 2])

    gcb = [gsc[zr, :] for zr in zrows]
    betab = [bsc[zr, :] for zr in zrows]
    glast = [gsc[pl.ds((c_len - 1) * nb + b, 1), :] for b in seqs]
    q = [head_pair(0, si, p) for si, p in chains]
    k = [head_pair(H_B, si, p) for si, p in chains]
    v = [head_pair(2 * H_B, si, p) for si, p in chains]
    c = [col_pair(gcb[si], p) for si, p in chains]
    bcol = [col_pair(betab[si], p) for si, p in chains]
    gl = [col_pair(jnp.broadcast_to(glast[si], (c_len, LANE)), p) for si, p in chains]
    decay = []
    for i in n:
        cm = jnp.broadcast_to(c[i], (2 * c_len, 2 * c_len))
        decay.append(jnp.exp((cm - cm.T) * tri) * tri)
    kb = [k[i] * bcol[i] for i in n]
    a_mat = [_dot_nt(kb[i], k[i]) * decay[i] * strict for i in n]
    qk = [_dot_nt(q[i], k[i]) * decay[i] for i in n]
    x = [eye - a_mat[i] * levels[0] for i in n]
    for l in range(1, 6):
        t = [_dot(a_mat[i] * levels[l], x[i]) for i in n]
        x = [x[i] - _dot(x[i], t[i]) for i in n]
    sol = [_dot(x[i], jnp.concatenate([v[i] * bcol[i], kb[i] * jnp.exp(c[i])], axis=1)) for i in n]
    s0 = [s_ref[seqs[si], 2 * p] for si, p in chains]
    s1 = [s_ref[seqs[si], 2 * p + 1] for si, p in chains]

    def per_head(lhs, i):
        return cat(_dot(lhs[0:c_len], s0[i]), _dot(lhs[c_len:2 * c_len], s1[i]))

    ws = [per_head(sol[i][:, DV:2 * DV], i) for i in n]
    qs = [per_head(q[i] * jnp.exp(c[i]), i) for i in n]
    v_new = [sol[i][:, 0:DV] - ws[i] for i in n]
    o = [qs[i] + _dot(qk[i], v_new[i]) for i in n]
    upd = []
    for i in n:
        kdec = k[i] * jnp.exp(gl[i] - c[i])
        vblk = jnp.concatenate([jnp.where(top, v_new[i], 0.0), jnp.where(top, 0.0, v_new[i])], axis=1)
        upd.append(_dot(kdec.T, vblk))
    for i, (si, p) in enumerate(chains):
        b = seqs[si]
        h0, h1 = 2 * p, 2 * p + 1
        s_ref[b, h0] = s0[i] * jnp.exp(glast[si][:, h0:h0 + 1]) + upd[i][:, 0:DV]
        s_ref[b, h1] = s1[i] * jnp.exp(glast[si][:, h1:h1 + 1]) + upd[i][:, DV:2 * DV]
        on = o[i] * lax.rsqrt(jnp.mean(o[i] * o[i], axis=-1, keepdims=True) + EPS) * gnw_ref[...]
        zb = cat(zbs[h0, zrows[si], :], zbs[h1, zrows[si], :])
        og = on * _silu(zb)
        ext_b[h0, rows[si], :] = og[0:c_len]
        ext_b[h1, rows[si], :] = og[c_len:2 * c_len]


def _ab_post(x, nb, mod_ref, ya, o, w_out_ref):
    out = _dot(ya.astype(BF16), w_out_ref[0:W_A, :]) + _dot(o.astype(BF16), w_out_ref[W_A:W_A + W_B, :])
    return _residual(x, out, mod_ref, nb)


def _to_time_major_rows(x_ref, x_tm, nb, nt):
    for b in range(nb):
        for m in range(D_MODEL // LANE):
            x_tm[m, pl.ds(b, nt, stride=nb), :] = x_ref[b, :, m * LANE:(m + 1) * LANE]
    return jnp.concatenate([x_tm[m] for m in range(D_MODEL // LANE)], axis=1)


def _from_time_major_rows(y, y_ref, y_tm, nb, nt):
    for m in range(D_MODEL // LANE):
        y_tm[m] = y[:, m * LANE:(m + 1) * LANE]
    for b in range(nb):
        for m in range(D_MODEL // LANE):
            y_ref[b, :, m * LANE:(m + 1) * LANE] = y_tm[m, pl.ds(b, nt, stride=nb), :]


def _ab_prompt_kernel(nb, nt, batch_major_in, x_ref, mod_ref, normw_ref, w_in_ref, w_ba_ref, caw_ref, cab_ref,
                      gxw_ref, gxb_ref, gaw_ref, gab_ref, apar_ref, cbw_ref, alog_ref, dtb_ref, gnw_ref,
                      w_out_ref, y_ref, ca_out, lru_out, cb_out, s_ref,
                      ext_a, ext_b, za, zbs, la, lb, gsc, bsc, h_s, *x_tm):
    c = pl.program_id(0)
    rows = nb * nt

    @pl.when(c == 0)
    def _():
        ext_a[0:3 * nb, :] = jnp.zeros((3 * nb, W_A), F32)
        ext_b[:, 0:3 * nb, :] = jnp.zeros((QKV_B // LANE, 3 * nb, LANE), F32)
        h_s[...] = jnp.zeros(h_s.shape, F32)
        s_ref[...] = jnp.zeros(s_ref.shape, F32)

    x = _to_time_major_rows(x_ref, x_tm[0], nb, nt) if batch_major_in else x_ref[...]
    _ab_pre(c == 0, nb, nt, True, x, mod_ref, normw_ref, w_in_ref, w_ba_ref, caw_ref, cab_ref, gxw_ref,
            gxb_ref, gaw_ref, gab_ref, apar_ref, cbw_ref, alog_ref, dtb_ref,
            ext_a, ext_b, za, zbs, la, lb, gsc, bsc, h_s)

    def cum_step(t, acc):
        r = pl.multiple_of(t * nb, nb)
        acc = acc + gsc[pl.ds(r, nb), :]
        gsc[pl.ds(r, nb), :] = acc
        return acc

    lax.fori_loop(0, nt, cum_step, jnp.zeros((nb, LANE), F32))
    masks = _gdn_block_masks()

    def per_seq_group(i, carry):
        _gdn_block([GDN_SEQS * i + s for s in range(GDN_SEQS)], nb, ext_b, zbs, gsc, bsc, s_ref, gnw_ref, masks)
        return carry

    lax.fori_loop(0, nb // GDN_SEQS, per_seq_group, 0)

    ya = lb[...] * _silu(za[...])
    o = jnp.concatenate([ext_b[h, 3 * nb:3 * nb + rows, :] for h in range(H_B)], axis=1)
    if batch_major_in:
        x = jnp.concatenate([x_tm[0][m] for m in range(D_MODEL // LANE)], axis=1)
    else:
        x = x_ref[...]
    y_ref[...] = _ab_post(x, nb, mod_ref, ya, o, w_out_ref)

    @pl.when(c == pl.num_programs(0) - 1)
    def _():
        ca_out[...] = ext_a[0:3 * nb, :]
        for n in range(QKV_B // LANE):
            cb_out[:, n * LANE:(n + 1) * LANE] = ext_b[n, 0:3 * nb, :]
        lru_out[...] = h_s[...]


def _ab_sample_pre_kernel(nb, x_ref, mod_ref, normw_ref, w_in_ref, w_ba_ref, caw_ref, cab_ref, gxw_ref,
                          gxb_ref, gaw_ref, gab_ref, apar_ref, cbw_ref, alog_ref, dtb_ref,
                          ca_in, lru_in, cb_in,
                          qkv_out, g_out, beta_out, zb_out, ya_out, ca_out, lru_out, cb_out,
                          ext_a, ext_b, za, zbs, la, lb, gsc, bsc, h_s):
    ext_a[0:3 * nb, :] = ca_in[...]
    for n in range(QKV_B // LANE):
        ext_b[n, 0:3 * nb, :] = cb_in[:, n * LANE:(n + 1) * LANE]
    h_s[...] = lru_in[...]
    _ab_pre(False, nb, 1, False, x_ref[...], mod_ref, normw_ref, w_in_ref, w_ba_ref, caw_ref, cab_ref,
            gxw_ref, gxb_ref, gaw_ref, gab_ref, apar_ref, cbw_ref, alog_ref, dtb_ref,
            ext_a, ext_b, za, zbs, la, lb, gsc, bsc, h_s)
    for n in range(QKV_B // LANE):
        qkv_out[:, n * LANE:(n + 1) * LANE] = ext_b[n, 3 * nb:4 * nb, :]
        cb_out[:, n * LANE:(n + 1) * LANE] = ext_b[n, 0:3 * nb, :]
    g_out[...] = gsc[...]
    beta_out[...] = bsc[...]
    for h in range(H_B):
        zb_out[:, h * LANE:(h + 1) * LANE] = zbs[h]
    ya_out[...] = lb[...] * _silu(za[...])
    ca_out[...] = ext_a[0:3 * nb, :]
    lru_out[...] = h_s[...]


def _gdn_step_kernel(bb, layer, n_layers, qkv_ref, g_ref, beta_ref, zb_ref, gnw_ref, s_in, *rest):
    if layer == 0:
        o_ref, s_all = rest
        for l in range(1, n_layers):
            s_all[l] = jnp.zeros(s_all.shape[1:], F32)
        s_out = s_all.at[0]
    else:
        _, o_ref, s_out = rest

    def per_seq(i, carry):
        g_row = g_ref[i]
        beta_row = beta_ref[i]
        for h in range(H_B):
            q = qkv_ref[i, :, h * LANE:(h + 1) * LANE]
            k = qkv_ref[i, :, H_B * DK + h * LANE:H_B * DK + (h + 1) * LANE]
            v = qkv_ref[i, :, 2 * H_B * DK + h * LANE:2 * H_B * DK + (h + 1) * LANE]
            eg = jnp.exp(g_row[:, h:h + 1])
            beta = beta_row[:, h:h + 1]
            kcol = jnp.broadcast_to(k, (DK, DK)).T
            qcol = jnp.broadcast_to(q, (DK, DK)).T
            s = s_in[i, h]
            ks = jnp.sum(kcol * s, axis=0, keepdims=True)
            v_new = beta * (v - eg * ks)
            s_new = eg * s + kcol * v_new
            s_out[i, h] = s_new
            o = jnp.sum(qcol * s_new, axis=0, keepdims=True)
            on = o * lax.rsqrt(jnp.mean(o * o, axis=-1, keepdims=True) + EPS) * gnw_ref[...]
            zb = zb_ref[i, :, h * LANE:(h + 1) * LANE]
            o_ref[i, :, h * LANE:(h + 1) * LANE] = on * _silu(zb)
        return carry

    lax.fori_loop(0, bb, per_seq, 0)


def _ab_sample_post_kernel(nb, x_ref, mod_ref, ya_ref, o_ref, w_out_ref, y_ref):
    y_ref[...] = _ab_post(x_ref[...], nb, mod_ref, ya_ref[...], o_ref[...], w_out_ref)


def _s5_kernel(nb, nt, has_state, final_norm, batch_major_out, *refs):
    (x_ref, mod_ref, normw_ref, w_in_ref, bre_ref, bim_ref, cre_ref, cim_ref, abr_ref, abi_ref, d_ref,
     gluw_ref, glub_ref, w_out_ref, fnw_ref) = refs[:15]
    refs = refs[15:]
    if has_state:
        sre_in, sim_in = refs[:2]
        refs = refs[2:]
    y_ref, sre_out, sim_out, xs_re, xs_im, y_s = refs[:6]
    c = pl.program_id(0)
    rows = nb * nt

    @pl.when(c == 0)
    def _():
        if has_state:
            xs_re[0:nb, :] = sre_in[...]
            xs_im[0:nb, :] = sim_in[...]
        else:
            xs_re[0:nb, :] = jnp.zeros((nb, NS), F32)
            xs_im[0:nb, :] = jnp.zeros((nb, NS), F32)

    x = x_ref[...]
    hb = _norm_mod(x, normw_ref[...], mod_ref, nb).astype(BF16)
    uz = _dot(hb, w_in_ref[...])
    u = uz[:, 0:D_MODEL]
    z = uz[:, D_MODEL:2 * D_MODEL]
    ub = u.astype(BF16)
    for j in range(S5_LANE_BLOCKS):
        sl = slice(j * S5_BLOCK_STATE, (j + 1) * S5_BLOCK_STATE)
        cl = slice(j * LANE, (j + 1) * LANE)
        uj = ub[:, cl]
        xs_re[nb:nb + rows, sl] = _dot(uj, bre_ref[j])
        xs_im[nb:nb + rows, sl] = _dot(uj, bim_ref[j])
        ar = jnp.broadcast_to(abr_ref[:, sl], (nb, S5_BLOCK_STATE))
        ai = jnp.broadcast_to(abi_ref[:, sl], (nb, S5_BLOCK_STATE))
        sr, si = xs_re[0:nb, sl], xs_im[0:nb, sl]
        for t in range(nt):
            r = slice(nb + t * nb, 2 * nb + t * nb)
            sr, si = ar * sr - ai * si + xs_re[r, sl], ar * si + ai * sr + xs_im[r, sl]
            xs_re[r, sl] = sr
            xs_im[r, sl] = si
        xs_re[0:nb, sl] = sr
        xs_im[0:nb, sl] = si
        yj = (_dot(xs_re[nb:nb + rows, sl].astype(BF16), cre_ref[j])
              - _dot(xs_im[nb:nb + rows, sl].astype(BF16), cim_ref[j]))
        yj = yj + d_ref[:, cl] * u[:, cl]
        cdf = 0.5 * (1.0 + jnp.tanh(0.7978845608028654 * (yj + 0.044715 * (yj * yj * yj))))
        y_s[:, cl] = yj * cdf
    y = y_s[...]
    y = y * jax.nn.sigmoid(_dot(y.astype(BF16), gluw_ref[...]) + glub_ref[...])
    y = y * _silu(z)
    xn = _residual(x, _dot(y.astype(BF16), w_out_ref[...]), mod_ref, nb)
    if final_norm:
        xn = xn * lax.rsqrt(jnp.mean(xn * xn, axis=-1, keepdims=True) + EPS) * fnw_ref[...]
    if batch_major_out:
        _from_time_major_rows(xn, y_ref, refs[6], nb, nt)
    else:
        y_ref[...] = xn

    @pl.when(c == pl.num_programs(0) - 1)
    def _():
        sre_out[...] = xs_re[0:nb, :]
        sim_out[...] = xs_im[0:nb, :]


def _s5_prep_kernel(are_ref, aim_ref, ldt_ref, bre_ref, bim_ref, abr_out, abi_out, bbr_out, bbi_out):
    a_re = are_ref[...]
    a_im = aim_ref[...]
    dt = jnp.exp(ldt_ref[...])
    mag = jnp.exp(a_re * dt)
    abr = mag * jnp.cos(a_im * dt)
    abi = mag * jnp.sin(a_im * dt)
    abr_out[...] = abr
    abi_out[...] = abi
    den = a_re * a_re + a_im * a_im
    nr = abr - 1.0
    cr = (nr * a_re + abi * a_im) / den
    ci = (abi * a_re - nr * a_im) / den
    b_re = bre_ref[...]
    b_im = bim_ref[...]
    bbr_out[...] = cr[:, None, :] * b_re - ci[:, None, :] * b_im
    bbi_out[...] = cr[:, None, :] * b_im + ci[:, None, :] * b_re


def _mod_kernel(c_ref, w_ref, b_ref, o_ref):
    c = c_ref[...]
    o_ref[...] = _dot(_silu(c).astype(BF16), w_ref[...].astype(BF16)) + b_ref[...]


def _full(shape):
    n = len(shape)
    return pl.BlockSpec(shape, lambda *_: (0,) * n)


def _const(shape):
    n = len(shape)
    return pl.BlockSpec(shape, lambda *_: (0,) * n, pipeline_mode=pl.Buffered(1))


def _params(n_grid):
    return pltpu.CompilerParams(dimension_semantics=("arbitrary",) * n_grid, vmem_limit_bytes=VMEM_LIMIT)


def _mod_all(c_all, mod_w, mod_b):
    n = c_all.shape[0]
    return pl.pallas_call(
        _mod_kernel,
        grid=(DEPTH, 3),
        in_specs=[pl.BlockSpec((n, D_MODEL), lambda i, j: (0, 0)),
                  pl.BlockSpec((None, D_MODEL, D_MODEL), lambda i, j: (i, 0, j)),
                  pl.BlockSpec((None, 1, D_MODEL), lambda i, j: (i, 0, j))],
        out_specs=pl.BlockSpec((None, n, D_MODEL), lambda i, j: (i, 0, j)),
        out_shape=jax.ShapeDtypeStruct((DEPTH, n, 3 * D_MODEL), F32),
        compiler_params=_params(2),
        name="mod_all",
    )(c_all, mod_w, mod_b.reshape(DEPTH, 1, 3 * D_MODEL))


def _ab_weights(j, ab_in_w, ab_out_w, conv_a_w, conv_a_b, lru_gx_w, lru_gx_b, lru_ga_w, lru_ga_b,
                lru_a_param, conv_b_w, gdn_a_log, gdn_dt_bias, gdn_norm_w):
    n_main = 2 * W_A + QKV_B + W_B
    w_in = ab_in_w[j, :, 0:n_main].astype(BF16)
    w_ba = jnp.zeros((D_MODEL, 2 * LANE), F32)
    w_ba = w_ba.at[:, 0:H_B].set(ab_in_w[j, :, n_main:n_main + H_B])
    w_ba = w_ba.at[:, LANE:LANE + H_B].set(ab_in_w[j, :, n_main + H_B:n_main + 2 * H_B]).astype(BF16)
    pad = lambda t: jnp.zeros((1, LANE), F32).at[0, 0:H_B].set(t)
    return dict(
        w_in=w_in, w_ba=w_ba, caw=conv_a_w[j], cab=conv_a_b[j].reshape(1, W_A),
        gxw=lru_gx_w[j].astype(BF16), gxb=lru_gx_b[j].reshape(1, W_A),
        gaw=lru_ga_w[j].astype(BF16), gab=lru_ga_b[j].reshape(1, W_A),
        apar=lru_a_param[j].reshape(1, W_A), cbw=conv_b_w[j], alog=pad(gdn_a_log[j]),
        dtb=pad(gdn_dt_bias[j]), gnw=gdn_norm_w[j].reshape(1, DV), w_out=ab_out_w[j].astype(BF16))


_AB_PRE_NAMES = ("w_in", "w_ba", "caw", "cab", "gxw", "gxb", "gaw", "gab", "apar", "cbw", "alog", "dtb")


def _ab_scratch(nb, rows):
    return [pltpu.VMEM((3 * nb + rows, W_A), F32), pltpu.VMEM((QKV_B // LANE, 3 * nb + rows, LANE), F32),
            pltpu.VMEM((rows, W_A), F32), pltpu.VMEM((H_B, rows, LANE), F32),
            pltpu.VMEM((rows, W_A), F32), pltpu.VMEM((rows, W_A), F32),
            pltpu.VMEM((rows, LANE), F32), pltpu.VMEM((rows, LANE), F32), pltpu.VMEM((nb, W_A), F32)]


def _ab_prompt(x, mod, normw, w, nb, nt):
    rows = nb * nt
    batch_major = x.ndim == 3
    n_rows = x.shape[0] * x.shape[1] if batch_major else x.shape[0]
    n_chunks = n_rows // rows
    pre = [w[k] for k in _AB_PRE_NAMES]
    if batch_major:
        x_spec = pl.BlockSpec((nb, nt, D_MODEL), lambda c: (0, c, 0))
        extra = [pltpu.VMEM((D_MODEL // LANE, rows, LANE), F32)]
    else:
        x_spec = pl.BlockSpec((rows, D_MODEL), lambda c: (c, 0))
        extra = []
    return pl.pallas_call(
        functools.partial(_ab_prompt_kernel, nb, nt, batch_major),
        grid=(n_chunks,),
        in_specs=[x_spec, _const(mod.shape), _const(normw.shape)]
        + [_const(t.shape) for t in pre] + [_const(w["gnw"].shape), _const(w["w_out"].shape)],
        out_specs=[pl.BlockSpec((rows, D_MODEL), lambda c: (c, 0)), _full((3 * nb, W_A)), _full((nb, W_A)),
                   _full((3 * nb, QKV_B)), _const((nb, H_B, DK, DV))],
        out_shape=[jax.ShapeDtypeStruct((n_rows, D_MODEL), F32), jax.ShapeDtypeStruct((3 * nb, W_A), F32),
                   jax.ShapeDtypeStruct((nb, W_A), F32), jax.ShapeDtypeStruct((3 * nb, QKV_B), F32),
                   jax.ShapeDtypeStruct((nb, H_B, DK, DV), F32)],
        scratch_shapes=_ab_scratch(nb, rows) + extra,
        compiler_params=_params(1),
        name="ab_prompt",
    )(x, mod, normw, *pre, w["gnw"], w["w_out"])


def _ab_sample(x, mod, normw, w, conv_a, lru_h, conv_b, delta_all, layer, delta_new_all):
    nb = x.shape[0]
    pre = [w[k] for k in _AB_PRE_NAMES]
    ins = [x, mod, normw, *pre, conv_a, lru_h, conv_b]
    outs = [(nb, QKV_B), (nb, LANE), (nb, LANE), (nb, W_B), (nb, W_A), (3 * nb, W_A), (nb, W_A),
            (3 * nb, QKV_B)]
    qkv, g, beta, zb, ya, ca_new, lru_new, cb_new = pl.pallas_call(
        functools.partial(_ab_sample_pre_kernel, nb),
        in_specs=[_full(t.shape) for t in ins],
        out_specs=[_full(s) for s in outs],
        out_shape=[jax.ShapeDtypeStruct(s, F32) for s in outs],
        scratch_shapes=_ab_scratch(nb, nb),
        compiler_params=pltpu.CompilerParams(vmem_limit_bytes=VMEM_LIMIT),
        name="ab_sample_pre",
    )(*ins)
    bb = 8
    row_block = lambda width: pl.BlockSpec((bb, 1, width), lambda i: (i, 0, 0))
    n_layers = delta_all.shape[0]
    state_block = pl.BlockSpec((None, bb, H_B, DK, DV), lambda i: (layer, i, 0, 0, 0))
    step_ins = [qkv.reshape(nb, 1, QKV_B), g.reshape(nb, 1, LANE), beta.reshape(nb, 1, LANE),
                zb.reshape(nb, 1, W_B), w["gnw"], delta_all]
    step_specs = [row_block(QKV_B), row_block(LANE), row_block(LANE), row_block(W_B), _full((1, DV)),
                  state_block]
    if layer == 0:
        out_state_block = pl.BlockSpec((n_layers, bb, H_B, DK, DV), lambda i: (0, i, 0, 0, 0))
        aliases = {}
    else:
        step_ins.append(delta_new_all)
        step_specs.append(pl.BlockSpec(memory_space=pl.ANY))
        out_state_block = state_block
        aliases = {len(step_ins) - 1: 1}
    o, delta_new_all = pl.pallas_call(
        functools.partial(_gdn_step_kernel, bb, layer, n_layers),
        grid=(nb // bb,),
        in_specs=step_specs,
        out_specs=[row_block(W_B), out_state_block],
        out_shape=[jax.ShapeDtypeStruct((nb, 1, W_B), F32), jax.ShapeDtypeStruct(delta_all.shape, F32)],
        input_output_aliases=aliases,
        compiler_params=_params(1),
        name="gdn_step",
    )(*step_ins)
    o = o.reshape(nb, W_B)
    y = pl.pallas_call(
        functools.partial(_ab_sample_post_kernel, nb),
        in_specs=[_full(x.shape), _full(mod.shape), _full(ya.shape), _full(o.shape), _full(w["w_out"].shape)],
        out_specs=_full(x.shape),
        out_shape=jax.ShapeDtypeStruct(x.shape, F32),
        compiler_params=pltpu.CompilerParams(vmem_limit_bytes=VMEM_LIMIT),
        name="ab_sample_post",
    )(x, mod, ya, o, w["w_out"])
    return y, ca_new, lru_new, cb_new, delta_new_all


def _s5_prep(s5_a_re, s5_a_im, s5_log_dt, s5_b_re, s5_b_im):
    n = s5_a_re.shape[0]
    gp = pl.BlockSpec((None, G_C, P_C), lambda i: (i, 0, 0))
    gcp = pl.BlockSpec((None, G_C, CG, P_C), lambda i: (i, 0, 0, 0))
    return pl.pallas_call(
        _s5_prep_kernel,
        grid=(n,),
        in_specs=[gp, gp, pl.BlockSpec((None, G_C, 1), lambda i: (i, 0, 0)), gcp, gcp],
        out_specs=[gp, gp, gcp, gcp],
        out_shape=[jax.ShapeDtypeStruct((n, G_C, P_C), F32)] * 2
        + [jax.ShapeDtypeStruct((n, G_C, CG, P_C), F32)] * 2,
        compiler_params=_params(1),
        name="s5_prep",
    )(s5_a_re, s5_a_im, s5_log_dt.reshape(n, G_C, 1), jnp.swapaxes(s5_b_re, 2, 3), jnp.swapaxes(s5_b_im, 2, 3))


def _block_diag_in(t):
    gl = G_C // S5_LANE_BLOCKS
    t = t.reshape(S5_LANE_BLOCKS, gl, CG, P_C)
    eye = jnp.eye(gl, dtype=t.dtype)
    return jnp.einsum("jgcp,gh->jgchp", t, eye).reshape(S5_LANE_BLOCKS, gl * CG, gl * P_C)


def _block_diag_out(t):
    gl = G_C // S5_LANE_BLOCKS
    t = t.reshape(S5_LANE_BLOCKS, gl, CG, P_C)
    eye = jnp.eye(gl, dtype=t.dtype)
    return jnp.einsum("jgcp,gh->jgphc", t, eye).reshape(S5_LANE_BLOCKS, gl * P_C, gl * CG)


def _s5_layer(x, mod, normw, w, nb, nt, state, final_norm, batch_major_out=False):
    rows = nb * nt
    n_chunks = x.shape[0] // rows
    consts = [mod, normw, w["w_in"], w["bre"], w["bim"], w["cre"], w["cim"], w["abr"], w["abi"], w["d"],
              w["gluw"], w["glub"], w["w_out"], w["fnw"]]
    ins = [x, *consts]
    in_specs = [pl.BlockSpec((rows, D_MODEL), lambda c: (c, 0))] + [_const(t.shape) for t in consts]
    if state is not None:
        ins += list(state)
        in_specs += [_const((nb, NS))] * 2
    if batch_major_out:
        y_spec = pl.BlockSpec((nb, nt, D_MODEL), lambda c: (0, c, 0))
        y_shape = (nb, x.shape[0] // nb, D_MODEL)
        extra = [pltpu.VMEM((D_MODEL // LANE, rows, LANE), F32)]
    else:
        y_spec = pl.BlockSpec((rows, D_MODEL), lambda c: (c, 0))
        y_shape = x.shape
        extra = []
    return pl.pallas_call(
        functools.partial(_s5_kernel, nb, nt, state is not None, final_norm, batch_major_out),
        grid=(n_chunks,),
        in_specs=in_specs,
        out_specs=[y_spec, _full((nb, NS)), _full((nb, NS))],
        out_shape=[jax.ShapeDtypeStruct(y_shape, F32), jax.ShapeDtypeStruct((nb, NS), F32),
                   jax.ShapeDtypeStruct((nb, NS), F32)],
        scratch_shapes=[pltpu.VMEM((nb + rows, NS), F32), pltpu.VMEM((nb + rows, NS), F32),
                        pltpu.VMEM((rows, D_MODEL), F32)] + extra,
        compiler_params=_params(1),
        name="s5_layer",
    )(*ins)


def _to_time_major(t):
    b, l, c = t.shape
    return jnp.swapaxes(t, 0, 1).reshape(l * b, c)


def _from_time_major(t, b):
    lb, c = t.shape
    return jnp.swapaxes(t.reshape(lb // b, b, c), 0, 1)


def kernel(x_prompt, x_sample, c_prompt, c_sample, state_conv_a, state_lru, state_conv_b, state_delta, state_s5_re, state_s5_im, norm_w, mod_w, mod_b, ab_in_w, ab_out_w, conv_a_w, conv_a_b, lru_gx_w, lru_gx_b, lru_ga_w, lru_ga_b, lru_a_param, conv_b_w, gdn_a_log, gdn_dt_bias, gdn_norm_w, c_in_w, c_out_w, s5_a_re, s5_a_im, s5_b_re, s5_b_im, s5_c_re, s5_c_im, s5_d, s5_log_dt, glu_w, glu_b, final_norm_w):
    nbp, seq, _ = x_prompt.shape
    nbs = x_sample.shape[0]
    n_ab = ab_in_w.shape[0]
    n_c = c_in_w.shape[0]
    nt = GDN_BLOCK

    mods = _mod_all(jnp.concatenate([c_prompt, c_sample], axis=0), mod_w, mod_b)
    abr, abi, bbr, bbi = _s5_prep(s5_a_re, s5_a_im, s5_log_dt, s5_b_re, s5_b_im)
    fnw = final_norm_w.reshape(1, D_MODEL)

    xp = x_prompt
    xs = x_sample.reshape(nbs, D_MODEL)
    p_states = [[] for _ in range(6)]
    s_states = [[] for _ in range(6)]
    s_delta = None
    for i in range(DEPTH):
        j = i // 2
        normw = norm_w[i].reshape(1, D_MODEL)
        mod_p = mods[i, 0:nbp]
        mod_s = mods[i, nbp:nbp + nbs]
        if i % 2 == 0:
            w = _ab_weights(j, ab_in_w, ab_out_w, conv_a_w, conv_a_b, lru_gx_w, lru_gx_b, lru_ga_w, lru_ga_b,
                            lru_a_param, conv_b_w, gdn_a_log, gdn_dt_bias, gdn_norm_w)
            xp, ca, lh, cb, ds = _ab_prompt(xp, mod_p, normw, w, nbp, nt)
            for lst, val in zip(p_states[:4], (_from_time_major(ca, nbp), lh, _from_time_major(cb, nbp), ds)):
                lst.append(val)
            xs, ca, lh, cb, s_delta = _ab_sample(xs, mod_s, normw, w, _to_time_major(state_conv_a[j]),
                                                 state_lru[j], _to_time_major(state_conv_b[j]), state_delta, j,
                                                 s_delta)
            for lst, val in zip(s_states[:3], (_from_time_major(ca, nbs), lh, _from_time_major(cb, nbs))):
                lst.append(val)
        else:
            w = dict(
                w_in=c_in_w[j].astype(BF16), bre=_block_diag_in(bbr[j]).astype(BF16),
                bim=_block_diag_in(bbi[j]).astype(BF16), cre=_block_diag_out(s5_c_re[j]).astype(BF16),
                cim=_block_diag_out(s5_c_im[j]).astype(BF16), abr=abr[j].reshape(1, NS),
                abi=abi[j].reshape(1, NS), d=s5_d[j].reshape(1, D_MODEL), gluw=glu_w[j].astype(BF16),
                glub=glu_b[j].reshape(1, D_MODEL), w_out=c_out_w[j].astype(BF16), fnw=fnw)
            last = i == DEPTH - 1
            xp, sr, si = _s5_layer(xp, mod_p, normw, w, nbp, nt, None, last, batch_major_out=last)
            p_states[4].append(sr.reshape(nbp, G_C, P_C))
            p_states[5].append(si.reshape(nbp, G_C, P_C))
            xs, sr, si = _s5_layer(xs, mod_s, normw, w, nbs, 1,
                                   (state_s5_re[j].reshape(nbs, NS), state_s5_im[j].reshape(nbs, NS)), last)
            s_states[4].append(sr.reshape(nbs, G_C, P_C))
            s_states[5].append(si.reshape(nbs, G_C, P_C))
    y_prompt = xp
    y_sample = xs.reshape(nbs, 1, D_MODEL)
    stack = lambda lists: tuple(jnp.stack(l) for l in lists)
    s_out = stack(s_states[:3]) + (s_delta,) + stack(s_states[4:])
    return (y_prompt, y_sample) + stack(p_states) + s_out
```

```python
import functools

import jax
import jax.numpy as jnp
from jax import lax
from jax.experimental import pallas as pl
from jax.experimental.pallas import tpu as pltpu

F32 = jnp.float32
BF16 = jnp.bfloat16

D_MODEL = 1024
DEPTH = 4
CONV_W = 4
W_A = 1024
H_A = 8
BW_A = 128
LRU_C = 8.0
H_B = 8
DK = 128
DV = 128
W_B = H_B * DV
QKV_B = 3 * W_B
CG = 16
G_C = 64
P_C = 64
NS = G_C * P_C
EPS = 1e-6
LANE = 128
GDN_BLOCK = 64
GDN_SEQS = 4
S5_LANE_BLOCKS = D_MODEL // LANE
S5_BLOCK_STATE = NS // S5_LANE_BLOCKS
VMEM_LIMIT = 60 * 1024 * 1024


def _dot(a, b):
    return jnp.dot(a.astype(BF16), b.astype(BF16), preferred_element_type=F32)


def _dot_nt(a, b):
    return lax.dot_general(a.astype(BF16), b.astype(BF16), (((1,), (1,)), ((), ())),
                           preferred_element_type=F32)


def _silu(x):
    return x * jax.nn.sigmoid(x)


def _softplus(x):
    return jnp.maximum(x, 0.0) + jnp.log1p(jnp.exp(-jnp.abs(x)))


def _norm_mod(x, normw, mod_ref, nb):
    rows = x.shape[0]
    ms = jnp.mean(x * x, axis=-1, keepdims=True)
    y = x * lax.rsqrt(ms + EPS) * normw
    shift = mod_ref[:, 0:D_MODEL]
    scale = mod_ref[:, D_MODEL:2 * D_MODEL]
    y3 = y.reshape(rows // nb, nb, D_MODEL)
    return (y3 * (1.0 + scale)[None] + shift[None]).reshape(rows, D_MODEL)


def _residual(x, out, mod_ref, nb):
    rows = x.shape[0]
    gate = mod_ref[:, 2 * D_MODEL:3 * D_MODEL]
    return x + (out.reshape(rows // nb, nb, D_MODEL) * gate[None]).reshape(rows, D_MODEL)


def _conv_inplace(ext, lane0, w_ref, w_lane0, nb, rows, post):
    sl = slice(lane0, lane0 + LANE)
    wl = slice(w_lane0, w_lane0 + LANE)
    acc = ext[0:rows, sl] * w_ref[0:1, wl]
    for j in range(1, CONV_W):
        acc = acc + ext[j * nb:j * nb + rows, sl] * w_ref[j:j + 1, wl]
    tail = ext[rows:rows + 3 * nb, sl]
    ext[3 * nb:3 * nb + rows, sl] = post(acc)
    ext[0:3 * nb, sl] = tail


def _l2norm(t):
    return t * lax.rsqrt(jnp.sum(t * t, axis=-1, keepdims=True) + EPS)


def _ab_pre(first, nb, nt, reset_first, x, mod_ref, normw_ref, w_in_ref, w_ba_ref, caw_ref, cab_ref,
            gxw_ref, gxb_ref, gaw_ref, gab_ref, apar_ref, cbw_ref, alog_ref, dtb_ref,
            ext_a, ext_b, za, zbs, la, lb, gsc, bsc, h_s):
    rows = nb * nt
    hb = _norm_mod(x, normw_ref[...], mod_ref, nb).astype(BF16)
    tile = 2 * LANE

    def emit_tile(c0):
        part = _dot(hb, w_in_ref[:, c0:c0 + tile])
        for m in range(tile // LANE):
            c = c0 + m * LANE
            piece = part[:, m * LANE:(m + 1) * LANE]
            if c < W_A:
                ext_a[3 * nb:3 * nb + rows, c:c + LANE] = piece
            elif c < 2 * W_A:
                za[:, c - W_A:c - W_A + LANE] = piece
            elif c < 2 * W_A + QKV_B:
                ext_b[(c - 2 * W_A) // LANE, 3 * nb:3 * nb + rows, :] = piece
            else:
                zbs[(c - 2 * W_A - QKV_B) // LANE] = piece

    def lru_block(blk):
        sl = slice(blk * BW_A, (blk + 1) * BW_A)
        _conv_inplace(ext_a, blk * BW_A, caw_ref, blk * BW_A, nb, rows, lambda acc: acc + cab_ref[:, sl])
        xb = ext_a[3 * nb:3 * nb + rows, sl]
        xbb = xb.astype(BF16)
        gate_x = jax.nn.sigmoid(_dot(xbb, gxw_ref[blk]) + gxb_ref[:, sl])
        gate_a = jax.nn.sigmoid(_dot(xbb, gaw_ref[blk]) + gab_ref[:, sl])
        log_a = -LRU_C * gate_a * _softplus(-apar_ref[:, sl])
        a = jnp.exp(log_a)
        m2 = 1.0 - a * a
        mult = jnp.where(m2 > 0.0, m2 * lax.rsqrt(m2), 0.0)
        if reset_first:
            row = lax.broadcasted_iota(jnp.int32, (rows, BW_A), 0)
            mult = jnp.where(jnp.logical_and(first, row < nb), 1.0, mult)
        la[:, sl] = a
        lb[:, sl] = mult * gate_x * xb

    def qkv_slice(n):
        if n < H_B:
            post = lambda acc: _l2norm(_silu(acc)) * (DK ** -0.5)
        elif n < 2 * H_B:
            post = lambda acc: _l2norm(_silu(acc))
        else:
            post = _silu
        _conv_inplace(ext_b.at[n], 0, cbw_ref, n * LANE, nb, rows, post)

    col_xa, col_za, col_qkv, col_zb = 0, W_A, 2 * W_A, 2 * W_A + QKV_B
    tiles = ([col_xa + i * tile for i in range(W_A // tile)] + [col_qkv + i * tile for i in range(QKV_B // tile)]
             + [col_za + i * tile for i in range(W_A // tile)] + [col_zb + i * tile for i in range(W_B // tile)])
    n_xa = W_A // tile
    tasks = [(n_xa - 1, functools.partial(lru_block, blk)) for blk in range(H_A)]
    tasks += [(n_xa + n // 2, functools.partial(qkv_slice, n)) for n in range(QKV_B // LANE)]
    for ti, c0 in enumerate(tiles):
        emit_tile(c0)
        for _ in range(2):
            if tasks and tasks[0][0] <= ti:
                tasks.pop(0)[1]()
    ba = _dot(hb, w_ba_ref[...])
    bsc[...] = jax.nn.sigmoid(ba[:, 0:LANE])
    gsc[...] = -jnp.exp(alog_ref[...]) * _softplus(ba[:, LANE:2 * LANE] + dtb_ref[...])
    for _, task in tasks:
        task()

    def lru_step(t, h):
        r = pl.multiple_of(t * nb, nb)
        h = la[pl.ds(r, nb), :] * h + lb[pl.ds(r, nb), :]
        lb[pl.ds(r, nb), :] = h
        return h

    if nt == 1:
        h = la[...] * h_s[...] + lb[...]
        lb[...] = h
        h_s[...] = h
    else:
        h_s[...] = lax.fori_loop(0, nt, lru_step, h_s[...])


def _gdn_block_masks():
    n = 2 * GDN_BLOCK
    ri = lax.broadcasted_iota(jnp.int32, (n, n), 0)
    ci = lax.broadcasted_iota(jnp.int32, (n, n), 1)
    same = (ri >= GDN_BLOCK) == (ci >= GDN_BLOCK)
    tri = jnp.where(jnp.logical_and(same, ri >= ci), 1.0, 0.0).astype(F32)
    strict = jnp.where(jnp.logical_and(same, ri > ci), 1.0, 0.0).astype(F32)
    eye = jnp.where(ri == ci, 1.0, 0.0).astype(F32)
    levels = []
    for l in range(6):
        rb = ri >> l
        sub = jnp.logical_and((rb & 1) == 1, (ci >> l) == rb - 1)
        levels.append(jnp.where(jnp.logical_and(same, sub), 1.0, 0.0).astype(F32))
    return tri, strict, eye, levels


def _gdn_block(seqs, nb, ext_b, zbs, gsc, bsc, s_ref, gnw_ref, masks):
    c_len = GDN_BLOCK
    top = lax.broadcasted_iota(jnp.int32, (2 * c_len, LANE), 0) < c_len
    tri, strict, eye, levels = masks
    chains = [(si, p) for si in range(len(seqs)) for p in range(H_B // 2)]
    n = range(len(chains))

    def cat(a0, a1):
        return jnp.concatenate([a0, a1], axis=0)

    rows = [pl.ds(3 * nb + b, c_len, stride=nb) for b in seqs]
    zrows = [pl.ds(b, c_len, stride=nb) for b in seqs]

    def head_pair(base, si, p):
        return cat(ext_b[base + 2 * p, rows[si], :], ext_b[base + 2 * p + 1, rows[si], :])

    def col_pair(t, p):
        return cat(t[:, 2 * p:2 * p + 1], t[:, 2 * p + 1:2 * p + 2])

    gcb = [gsc[zr, :] for zr in zrows]
    betab = [bsc[zr, :] for zr in zrows]
    glast = [gsc[pl.ds((c_len - 1) * nb + b, 1), :] for b in seqs]
    q = [head_pair(0, si, p) for si, p in chains]
    k = [head_pair(H_B, si, p) for si, p in chains]
    v = [head_pair(2 * H_B, si, p) for si, p in chains]
    c = [col_pair(gcb[si], p) for si, p in chains]
    bcol = [col_pair(betab[si], p) for si, p in chains]
    gl = [col_pair(jnp.broadcast_to(glast[si], (c_len, LANE)), p) for si, p in chains]
    decay = []
    for i in n:
        cm = jnp.broadcast_to(c[i], (2 * c_len, 2 * c_len))
        decay.append(jnp.exp((cm - cm.T) * tri) * tri)
    kb = [k[i] * bcol[i] for i in n]
    a_mat = [_dot_nt(kb[i], k[i]) * decay[i] * strict for i in n]
    qk = [_dot_nt(q[i], k[i]) * decay[i] for i in n]
    x = [eye - a_mat[i] * levels[0] for i in n]
    for l in range(1, 6):
        t = [_dot(a_mat[i] * levels[l], x[i]) for i in n]
        x = [x[i] - _dot(x[i], t[i]) for i in n]
    sol = [_dot(x[i], jnp.concatenate([v[i] * bcol[i], kb[i] * jnp.exp(c[i])], axis=1)) for i in n]
    s0 = [s_ref[seqs[si], 2 * p] for si, p in chains]
    s1 = [s_ref[seqs[si], 2 * p + 1] for si, p in chains]

    def per_head(lhs, i):
        return cat(_dot(lhs[0:c_len], s0[i]), _dot(lhs[c_len:2 * c_len], s1[i]))

    ws = [per_head(sol[i][:, DV:2 * DV], i) for i in n]
    qs = [per_head(q[i] * jnp.exp(c[i]), i) for i in n]
    v_new = [sol[i][:, 0:DV] - ws[i] for i in n]
    o = [qs[i] + _dot(qk[i], v_new[i]) for i in n]
    upd = []
    for i in n:
        kdec = k[i] * jnp.exp(gl[i] - c[i])
        vblk = jnp.concatenate([jnp.where(top, v_new[i], 0.0), jnp.where(top, 0.0, v_new[i])], axis=1)
        upd.append(_dot(kdec.T, vblk))
    for i, (si, p) in enumerate(chains):
        b = seqs[si]
        h0, h1 = 2 * p, 2 * p + 1
        s_ref[b, h0] = s0[i] * jnp.exp(glast[si][:, h0:h0 + 1]) + upd[i][:, 0:DV]
        s_ref[b, h1] = s1[i] * jnp.exp(glast[si][:, h1:h1 + 1]) + upd[i][:, DV:2 * DV]
        on = o[i] * lax.rsqrt(jnp.mean(o[i] * o[i], axis=-1, keepdims=True) + EPS) * gnw_ref[...]
        zb = cat(zbs[h0, zrows[si], :], zbs[h1, zrows[si], :])
        og = on * _silu(zb)
        ext_b[h0, rows[si], :] = og[0:c_len]
        ext_b[h1, rows[si], :] = og[c_len:2 * c_len]


def _ab_post(x, nb, mod_ref, ya, o, w_out_ref):
    out = _dot(ya.astype(BF16), w_out_ref[0:W_A, :]) + _dot(o.astype(BF16), w_out_ref[W_A:W_A + W_B, :])
    return _residual(x, out, mod_ref, nb)


def _to_time_major_rows(x_ref, x_tm, nb, nt):
    for b in range(nb):
        for m in range(D_MODEL // LANE):
            x_tm[m, pl.ds(b, nt, stride=nb), :] = x_ref[b, :, m * LANE:(m + 1) * LANE]
    return jnp.concatenate([x_tm[m] for m in range(D_MODEL // LANE)], axis=1)


def _from_time_major_rows(y, y_ref, y_tm, nb, nt):
    for m in range(D_MODEL // LANE):
        y_tm[m] = y[:, m * LANE:(m + 1) * LANE]
    for b in range(nb):
        for m in range(D_MODEL // LANE):
            y_ref[b, :, m * LANE:(m + 1) * LANE] = y_tm[m, pl.ds(b, nt, stride=nb), :]


def _ab_prompt_kernel(nb, nt, batch_major_in, x_ref, mod_ref, normw_ref, w_in_ref, w_ba_ref, caw_ref, cab_ref,
                      gxw_ref, gxb_ref, gaw_ref, gab_ref, apar_ref, cbw_ref, alog_ref, dtb_ref, gnw_ref,
                      w_out_ref, y_ref, ca_out, lru_out, cb_out, s_ref,
                      ext_a, ext_b, za, zbs, la, lb, gsc, bsc, h_s, *x_tm):
    c = pl.program_id(0)
    rows = nb * nt

    @pl.when(c == 0)
    def _():
        ext_a[0:3 * nb, :] = jnp.zeros((3 * nb, W_A), F32)
        ext_b[:, 0:3 * nb, :] = jnp.zeros((QKV_B // LANE, 3 * nb, LANE), F32)
        h_s[...] = jnp.zeros(h_s.shape, F32)
        s_ref[...] = jnp.zeros(s_ref.shape, F32)

    x = _to_time_major_rows(x_ref, x_tm[0], nb, nt) if batch_major_in else x_ref[...]
    _ab_pre(c == 0, nb, nt, True, x, mod_ref, normw_ref, w_in_ref, w_ba_ref, caw_ref, cab_ref, gxw_ref,
            gxb_ref, gaw_ref, gab_ref, apar_ref, cbw_ref, alog_ref, dtb_ref,
            ext_a, ext_b, za, zbs, la, lb, gsc, bsc, h_s)

    def cum_step(t, acc):
        r = pl.multiple_of(t * nb, nb)
        acc = acc + gsc[pl.ds(r, nb), :]
        gsc[pl.ds(r, nb), :] = acc
        return acc

    lax.fori_loop(0, nt, cum_step, jnp.zeros((nb, LANE), F32))
    masks = _gdn_block_masks()

    def per_seq_group(i, carry):
        _gdn_block([GDN_SEQS * i + s for s in range(GDN_SEQS)], nb, ext_b, zbs, gsc, bsc, s_ref, gnw_ref, masks)
        return carry

    lax.fori_loop(0, nb // GDN_SEQS, per_seq_group, 0)

    ya = lb[...] * _silu(za[...])
    o = jnp.concatenate([ext_b[h, 3 * nb:3 * nb + rows, :] for h in range(H_B)], axis=1)
    if batch_major_in:
        x = jnp.concatenate([x_tm[0][m] for m in range(D_MODEL // LANE)], axis=1)
    else:
        x = x_ref[...]
    y_ref[...] = _ab_post(x, nb, mod_ref, ya, o, w_out_ref)

    @pl.when(c == pl.num_programs(0) - 1)
    def _():
        ca_out[...] = ext_a[0:3 * nb, :]
        for n in range(QKV_B // LANE):
            cb_out[:, n * LANE:(n + 1) * LANE] = ext_b[n, 0:3 * nb, :]
        lru_out[...] = h_s[...]


def _ab_sample_pre_kernel(nb, x_ref, mod_ref, normw_ref, w_in_ref, w_ba_ref, caw_ref, cab_ref, gxw_ref,
                          gxb_ref, gaw_ref, gab_ref, apar_ref, cbw_ref, alog_ref, dtb_ref,
                          ca_in, lru_in, cb_in,
                          qkv_out, g_out, beta_out, zb_out, ya_out, ca_out, lru_out, cb_out,
                          ext_a, ext_b, za, zbs, la, lb, gsc, bsc, h_s):
    ext_a[0:3 * nb, :] = ca_in[...]
    for n in range(QKV_B // LANE):
        ext_b[n, 0:3 * nb, :] = cb_in[:, n * LANE:(n + 1) * LANE]
    h_s[...] = lru_in[...]
    _ab_pre(False, nb, 1, False, x_ref[...], mod_ref, normw_ref, w_in_ref, w_ba_ref, caw_ref, cab_ref,
            gxw_ref, gxb_ref, gaw_ref, gab_ref, apar_ref, cbw_ref, alog_ref, dtb_ref,
            ext_a, ext_b, za, zbs, la, lb, gsc, bsc, h_s)
    for n in range(QKV_B // LANE):
        qkv_out[:, n * LANE:(n + 1) * LANE] = ext_b[n, 3 * nb:4 * nb, :]
        cb_out[:, n * LANE:(n + 1) * LANE] = ext_b[n, 0:3 * nb, :]
    g_out[...] = gsc[...]
    beta_out[...] = bsc[...]
    for h in range(H_B):
        zb_out[:, h * LANE:(h + 1) * LANE] = zbs[h]
    ya_out[...] = lb[...] * _silu(za[...])
    ca_out[...] = ext_a[0:3 * nb, :]
    lru_out[...] = h_s[...]


def _gdn_step_kernel(bb, layer, n_layers, qkv_ref, g_ref, beta_ref, zb_ref, gnw_ref, s_in, *rest):
    if layer == 0:
        o_ref, s_all = rest
        for l in range(1, n_layers):
            s_all[l] = jnp.zeros(s_all.shape[1:], F32)
        s_out = s_all.at[0]
    else:
        _, o_ref, s_out = rest

    def per_seq(i, carry):
        g_row = g_ref[i]
        beta_row = beta_ref[i]
        heads = range(H_B)
        q = [qkv_ref[i, :, h * LANE:(h + 1) * LANE] for h in heads]
        k = [qkv_ref[i, :, H_B * DK + h * LANE:H_B * DK + (h + 1) * LANE] for h in heads]
        v = [qkv_ref[i, :, 2 * H_B * DK + h * LANE:2 * H_B * DK + (h + 1) * LANE] for h in heads]
        eg = [jnp.exp(g_row[:, h:h + 1]) for h in heads]
        kcol = [jnp.broadcast_to(k[h], (DK, DK)).T for h in heads]
        qcol = [jnp.broadcast_to(q[h], (DK, DK)).T for h in heads]
        s = [s_in[i, h] for h in heads]
        ks = [jnp.sum(kcol[h] * s[h], axis=0, keepdims=True) for h in heads]
        v_new = [beta_row[:, h:h + 1] * (v[h] - eg[h] * ks[h]) for h in heads]
        s_new = [eg[h] * s[h] + kcol[h] * v_new[h] for h in heads]
        for h in heads:
            s_out[i, h] = s_new[h]
        o = [jnp.sum(qcol[h] * s_new[h], axis=0, keepdims=True) for h in heads]
        for h in heads:
            on = o[h] * lax.rsqrt(jnp.mean(o[h] * o[h], axis=-1, keepdims=True) + EPS) * gnw_ref[...]
            zb = zb_ref[i, :, h * LANE:(h + 1) * LANE]
            o_ref[i, :, h * LANE:(h + 1) * LANE] = on * _silu(zb)
        return carry

    lax.fori_loop(0, bb, per_seq, 0)


def _ab_sample_post_kernel(nb, x_ref, mod_ref, ya_ref, o_ref, w_out_ref, y_ref):
    y_ref[...] = _ab_post(x_ref[...], nb, mod_ref, ya_ref[...], o_ref[...], w_out_ref)


def _s5_kernel(nb, nt, has_state, final_norm, batch_major_out, *refs):
    (x_ref, mod_ref, normw_ref, w_in_ref, bre_ref, bim_ref, cre_ref, cim_ref, abr_ref, abi_ref, d_ref,
     gluw_ref, glub_ref, w_out_ref, fnw_ref) = refs[:15]
    refs = refs[15:]
    if has_state:
        sre_in, sim_in = refs[:2]
        refs = refs[2:]
    y_ref, sre_out, sim_out, xs_re, xs_im, y_s = refs[:6]
    c = pl.program_id(0)
    rows = nb * nt

    @pl.when(c == 0)
    def _():
        if has_state:
            xs_re[0:nb, :] = sre_in[...]
            xs_im[0:nb, :] = sim_in[...]
        else:
            xs_re[0:nb, :] = jnp.zeros((nb, NS), F32)
            xs_im[0:nb, :] = jnp.zeros((nb, NS), F32)

    x = x_ref[...]
    hb = _norm_mod(x, normw_ref[...], mod_ref, nb).astype(BF16)
    uz = _dot(hb, w_in_ref[...])
    u = uz[:, 0:D_MODEL]
    z = uz[:, D_MODEL:2 * D_MODEL]
    ub = u.astype(BF16)
    for j in range(S5_LANE_BLOCKS):
        sl = slice(j * S5_BLOCK_STATE, (j + 1) * S5_BLOCK_STATE)
        cl = slice(j * LANE, (j + 1) * LANE)
        uj = ub[:, cl]
        xs_re[nb:nb + rows, sl] = _dot(uj, bre_ref[j])
        xs_im[nb:nb + rows, sl] = _dot(uj, bim_ref[j])
        ar = jnp.broadcast_to(abr_ref[:, sl], (nb, S5_BLOCK_STATE))
        ai = jnp.broadcast_to(abi_ref[:, sl], (nb, S5_BLOCK_STATE))
        sr, si = xs_re[0:nb, sl], xs_im[0:nb, sl]
        for t in range(nt):
            r = slice(nb + t * nb, 2 * nb + t * nb)
            sr, si = ar * sr - ai * si + xs_re[r, sl], ar * si + ai * sr + xs_im[r, sl]
            xs_re[r, sl] = sr
            xs_im[r, sl] = si
        xs_re[0:nb, sl] = sr
        xs_im[0:nb, sl] = si
        yj = (_dot(xs_re[nb:nb + rows, sl].astype(BF16), cre_ref[j])
              - _dot(xs_im[nb:nb + rows, sl].astype(BF16), cim_ref[j]))
        yj = yj + d_ref[:, cl] * u[:, cl]
        cdf = 0.5 * (1.0 + jnp.tanh(0.7978845608028654 * (yj + 0.044715 * (yj * yj * yj))))
        y_s[:, cl] = yj * cdf
    y = y_s[...]
    y = y * jax.nn.sigmoid(_dot(y.astype(BF16), gluw_ref[...]) + glub_ref[...])
    y = y * _silu(z)
    xn = _residual(x, _dot(y.astype(BF16), w_out_ref[...]), mod_ref, nb)
    if final_norm:
        xn = xn * lax.rsqrt(jnp.mean(xn * xn, axis=-1, keepdims=True) + EPS) * fnw_ref[...]
    if batch_major_out:
        _from_time_major_rows(xn, y_ref, refs[6], nb, nt)
    else:
        y_ref[...] = xn

    @pl.when(c == pl.num_programs(0) - 1)
    def _():
        sre_out[...] = xs_re[0:nb, :]
        sim_out[...] = xs_im[0:nb, :]


def _s5_prep_kernel(are_ref, aim_ref, ldt_ref, bre_ref, bim_ref, abr_out, abi_out, bbr_out, bbi_out):
    a_re = are_ref[...]
    a_im = aim_ref[...]
    dt = jnp.exp(ldt_ref[...])
    mag = jnp.exp(a_re * dt)
    abr = mag * jnp.cos(a_im * dt)
    abi = mag * jnp.sin(a_im * dt)
    abr_out[...] = abr
    abi_out[...] = abi
    den = a_re * a_re + a_im * a_im
    nr = abr - 1.0
    cr = (nr * a_re + abi * a_im) / den
    ci = (abi * a_re - nr * a_im) / den
    b_re = bre_ref[...]
    b_im = bim_ref[...]
    bbr_out[...] = cr[:, None, :] * b_re - ci[:, None, :] * b_im
    bbi_out[...] = cr[:, None, :] * b_im + ci[:, None, :] * b_re


def _mod_kernel(c_ref, w_ref, b_ref, o_ref):
    c = c_ref[...]
    o_ref[...] = _dot(_silu(c).astype(BF16), w_ref[...].astype(BF16)) + b_ref[...]


def _full(shape):
    n = len(shape)
    return pl.BlockSpec(shape, lambda *_: (0,) * n)


def _const(shape):
    n = len(shape)
    return pl.BlockSpec(shape, lambda *_: (0,) * n, pipeline_mode=pl.Buffered(1))


def _params(n_grid):
    return pltpu.CompilerParams(dimension_semantics=("arbitrary",) * n_grid, vmem_limit_bytes=VMEM_LIMIT)


def _mod_all(c_all, mod_w, mod_b):
    n = c_all.shape[0]
    return pl.pallas_call(
        _mod_kernel,
        grid=(DEPTH, 3),
        in_specs=[pl.BlockSpec((n, D_MODEL), lambda i, j: (0, 0)),
                  pl.BlockSpec((None, D_MODEL, D_MODEL), lambda i, j: (i, 0, j)),
                  pl.BlockSpec((None, 1, D_MODEL), lambda i, j: (i, 0, j))],
        out_specs=pl.BlockSpec((None, n, D_MODEL), lambda i, j: (i, 0, j)),
        out_shape=jax.ShapeDtypeStruct((DEPTH, n, 3 * D_MODEL), F32),
        compiler_params=_params(2),
        name="mod_all",
    )(c_all, mod_w, mod_b.reshape(DEPTH, 1, 3 * D_MODEL))


def _ab_weights(j, ab_in_w, ab_out_w, conv_a_w, conv_a_b, lru_gx_w, lru_gx_b, lru_ga_w, lru_ga_b,
                lru_a_param, conv_b_w, gdn_a_log, gdn_dt_bias, gdn_norm_w):
    n_main = 2 * W_A + QKV_B + W_B
    w_in = ab_in_w[j, :, 0:n_main].astype(BF16)
    w_ba = jnp.zeros((D_MODEL, 2 * LANE), F32)
    w_ba = w_ba.at[:, 0:H_B].set(ab_in_w[j, :, n_main:n_main + H_B])
    w_ba = w_ba.at[:, LANE:LANE + H_B].set(ab_in_w[j, :, n_main + H_B:n_main + 2 * H_B]).astype(BF16)
    pad = lambda t: jnp.zeros((1, LANE), F32).at[0, 0:H_B].set(t)
    return dict(
        w_in=w_in, w_ba=w_ba, caw=conv_a_w[j], cab=conv_a_b[j].reshape(1, W_A),
        gxw=lru_gx_w[j].astype(BF16), gxb=lru_gx_b[j].reshape(1, W_A),
        gaw=lru_ga_w[j].astype(BF16), gab=lru_ga_b[j].reshape(1, W_A),
        apar=lru_a_param[j].reshape(1, W_A), cbw=conv_b_w[j], alog=pad(gdn_a_log[j]),
        dtb=pad(gdn_dt_bias[j]), gnw=gdn_norm_w[j].reshape(1, DV), w_out=ab_out_w[j].astype(BF16))


_AB_PRE_NAMES = ("w_in", "w_ba", "caw", "cab", "gxw", "gxb", "gaw", "gab", "apar", "cbw", "alog", "dtb")


def _ab_scratch(nb, rows):
    return [pltpu.VMEM((3 * nb + rows, W_A), F32), pltpu.VMEM((QKV_B // LANE, 3 * nb + rows, LANE), F32),
            pltpu.VMEM((rows, W_A), F32), pltpu.VMEM((H_B, rows, LANE), F32),
            pltpu.VMEM((rows, W_A), F32), pltpu.VMEM((rows, W_A), F32),
            pltpu.VMEM((rows, LANE), F32), pltpu.VMEM((rows, LANE), F32), pltpu.VMEM((nb, W_A), F32)]


def _ab_prompt(x, mod, normw, w, nb, nt):
    rows = nb * nt
    batch_major = x.ndim == 3
    n_rows = x.shape[0] * x.shape[1] if batch_major else x.shape[0]
    n_chunks = n_rows // rows
    pre = [w[k] for k in _AB_PRE_NAMES]
    if batch_major:
        x_spec = pl.BlockSpec((nb, nt, D_MODEL), lambda c: (0, c, 0))
        extra = [pltpu.VMEM((D_MODEL // LANE, rows, LANE), F32)]
    else:
        x_spec = pl.BlockSpec((rows, D_MODEL), lambda c: (c, 0))
        extra = []
    return pl.pallas_call(
        functools.partial(_ab_prompt_kernel, nb, nt, batch_major),
        grid=(n_chunks,),
        in_specs=[x_spec, _const(mod.shape), _const(normw.shape)]
        + [_const(t.shape) for t in pre] + [_const(w["gnw"].shape), _const(w["w_out"].shape)],
        out_specs=[pl.BlockSpec((rows, D_MODEL), lambda c: (c, 0)), _full((3 * nb, W_A)), _full((nb, W_A)),
                   _full((3 * nb, QKV_B)), _const((nb, H_B, DK, DV))],
        out_shape=[jax.ShapeDtypeStruct((n_rows, D_MODEL), F32), jax.ShapeDtypeStruct((3 * nb, W_A), F32),
                   jax.ShapeDtypeStruct((nb, W_A), F32), jax.ShapeDtypeStruct((3 * nb, QKV_B), F32),
                   jax.ShapeDtypeStruct((nb, H_B, DK, DV), F32)],
        scratch_shapes=_ab_scratch(nb, rows) + extra,
        compiler_params=_params(1),
        name="ab_prompt",
    )(x, mod, normw, *pre, w["gnw"], w["w_out"])


def _ab_sample(x, mod, normw, w, conv_a, lru_h, conv_b, delta_all, layer, delta_new_all):
    nb = x.shape[0]
    pre = [w[k] for k in _AB_PRE_NAMES]
    ins = [x, mod, normw, *pre, conv_a, lru_h, conv_b]
    outs = [(nb, QKV_B), (nb, LANE), (nb, LANE), (nb, W_B), (nb, W_A), (3 * nb, W_A), (nb, W_A),
            (3 * nb, QKV_B)]
    qkv, g, beta, zb, ya, ca_new, lru_new, cb_new = pl.pallas_call(
        functools.partial(_ab_sample_pre_kernel, nb),
        in_specs=[_full(t.shape) for t in ins],
        out_specs=[_full(s) for s in outs],
        out_shape=[jax.ShapeDtypeStruct(s, F32) for s in outs],
        scratch_shapes=_ab_scratch(nb, nb),
        compiler_params=pltpu.CompilerParams(vmem_limit_bytes=VMEM_LIMIT),
        name="ab_sample_pre",
    )(*ins)
    bb = 8
    row_block = lambda width: pl.BlockSpec((bb, 1, width), lambda i: (i, 0, 0))
    n_layers = delta_all.shape[0]
    state_block = pl.BlockSpec((None, bb, H_B, DK, DV), lambda i: (layer, i, 0, 0, 0))
    step_ins = [qkv.reshape(nb, 1, QKV_B), g.reshape(nb, 1, LANE), beta.reshape(nb, 1, LANE),
                zb.reshape(nb, 1, W_B), w["gnw"], delta_all]
    step_specs = [row_block(QKV_B), row_block(LANE), row_block(LANE), row_block(W_B), _full((1, DV)),
                  state_block]
    if layer == 0:
        out_state_block = pl.BlockSpec((n_layers, bb, H_B, DK, DV), lambda i: (0, i, 0, 0, 0))
        aliases = {}
    else:
        step_ins.append(delta_new_all)
        step_specs.append(pl.BlockSpec(memory_space=pl.ANY))
        out_state_block = state_block
        aliases = {len(step_ins) - 1: 1}
    o, delta_new_all = pl.pallas_call(
        functools.partial(_gdn_step_kernel, bb, layer, n_layers),
        grid=(nb // bb,),
        in_specs=step_specs,
        out_specs=[row_block(W_B), out_state_block],
        out_shape=[jax.ShapeDtypeStruct((nb, 1, W_B), F32), jax.ShapeDtypeStruct(delta_all.shape, F32)],
        input_output_aliases=aliases,
        compiler_params=_params(1),
        name="gdn_step",
    )(*step_ins)
    o = o.reshape(nb, W_B)
    y = pl.pallas_call(
        functools.partial(_ab_sample_post_kernel, nb),
        in_specs=[_full(x.shape), _full(mod.shape), _full(ya.shape), _full(o.shape), _full(w["w_out"].shape)],
        out_specs=_full(x.shape),
        out_shape=jax.ShapeDtypeStruct(x.shape, F32),
        compiler_params=pltpu.CompilerParams(vmem_limit_bytes=VMEM_LIMIT),
        name="ab_sample_post",
    )(x, mod, ya, o, w["w_out"])
    return y, ca_new, lru_new, cb_new, delta_new_all


def _s5_prep(s5_a_re, s5_a_im, s5_log_dt, s5_b_re, s5_b_im):
    n = s5_a_re.shape[0]
    gp = pl.BlockSpec((None, G_C, P_C), lambda i: (i, 0, 0))
    gcp = pl.BlockSpec((None, G_C, CG, P_C), lambda i: (i, 0, 0, 0))
    return pl.pallas_call(
        _s5_prep_kernel,
        grid=(n,),
        in_specs=[gp, gp, pl.BlockSpec((None, G_C, 1), lambda i: (i, 0, 0)), gcp, gcp],
        out_specs=[gp, gp, gcp, gcp],
        out_shape=[jax.ShapeDtypeStruct((n, G_C, P_C), F32)] * 2
        + [jax.ShapeDtypeStruct((n, G_C, CG, P_C), F32)] * 2,
        compiler_params=_params(1),
        name="s5_prep",
    )(s5_a_re, s5_a_im, s5_log_dt.reshape(n, G_C, 1), jnp.swapaxes(s5_b_re, 2, 3), jnp.swapaxes(s5_b_im, 2, 3))


def _block_diag_in(t):
    gl = G_C // S5_LANE_BLOCKS
    t = t.reshape(S5_LANE_BLOCKS, gl, CG, P_C)
    eye = jnp.eye(gl, dtype=t.dtype)
    return jnp.einsum("jgcp,gh->jgchp", t, eye).reshape(S5_LANE_BLOCKS, gl * CG, gl * P_C)


def _block_diag_out(t):
    gl = G_C // S5_LANE_BLOCKS
    t = t.reshape(S5_LANE_BLOCKS, gl, CG, P_C)
    eye = jnp.eye(gl, dtype=t.dtype)
    return jnp.einsum("jgcp,gh->jgphc", t, eye).reshape(S5_LANE_BLOCKS, gl * P_C, gl * CG)


def _s5_layer(x, mod, normw, w, nb, nt, state, final_norm, batch_major_out=False):
    rows = nb * nt
    n_chunks = x.shape[0] // rows
    consts = [mod, normw, w["w_in"], w["bre"], w["bim"], w["cre"], w["cim"], w["abr"], w["abi"], w["d"],
              w["gluw"], w["glub"], w["w_out"], w["fnw"]]
    ins = [x, *consts]
    in_specs = [pl.BlockSpec((rows, D_MODEL), lambda c: (c, 0))] + [_const(t.shape) for t in consts]
    if state is not None:
        ins += list(state)
        in_specs += [_const((nb, NS))] * 2
    if batch_major_out:
        y_spec = pl.BlockSpec((nb, nt, D_MODEL), lambda c: (0, c, 0))
        y_shape = (nb, x.shape[0] // nb, D_MODEL)
        extra = [pltpu.VMEM((D_MODEL // LANE, rows, LANE), F32)]
    else:
        y_spec = pl.BlockSpec((rows, D_MODEL), lambda c: (c, 0))
        y_shape = x.shape
        extra = []
    return pl.pallas_call(
        functools.partial(_s5_kernel, nb, nt, state is not None, final_norm, batch_major_out),
        grid=(n_chunks,),
        in_specs=in_specs,
        out_specs=[y_spec, _full((nb, NS)), _full((nb, NS))],
        out_shape=[jax.ShapeDtypeStruct(y_shape, F32), jax.ShapeDtypeStruct((nb, NS), F32),
                   jax.ShapeDtypeStruct((nb, NS), F32)],
        scratch_shapes=[pltpu.VMEM((nb + rows, NS), F32), pltpu.VMEM((nb + rows, NS), F32),
                        pltpu.VMEM((rows, D_MODEL), F32)] + extra,
        compiler_params=_params(1),
        name="s5_layer",
    )(*ins)


def _to_time_major(t):
    b, l, c = t.shape
    return jnp.swapaxes(t, 0, 1).reshape(l * b, c)


def _from_time_major(t, b):
    lb, c = t.shape
    return jnp.swapaxes(t.reshape(lb // b, b, c), 0, 1)


def kernel(x_prompt, x_sample, c_prompt, c_sample, state_conv_a, state_lru, state_conv_b, state_delta, state_s5_re, state_s5_im, norm_w, mod_w, mod_b, ab_in_w, ab_out_w, conv_a_w, conv_a_b, lru_gx_w, lru_gx_b, lru_ga_w, lru_ga_b, lru_a_param, conv_b_w, gdn_a_log, gdn_dt_bias, gdn_norm_w, c_in_w, c_out_w, s5_a_re, s5_a_im, s5_b_re, s5_b_im, s5_c_re, s5_c_im, s5_d, s5_log_dt, glu_w, glu_b, final_norm_w):
    nbp, seq, _ = x_prompt.shape
    nbs = x_sample.shape[0]
    n_ab = ab_in_w.shape[0]
    n_c = c_in_w.shape[0]
    nt = GDN_BLOCK

    mods = _mod_all(jnp.concatenate([c_prompt, c_sample], axis=0), mod_w, mod_b)
    abr, abi, bbr, bbi = _s5_prep(s5_a_re, s5_a_im, s5_log_dt, s5_b_re, s5_b_im)
    fnw = final_norm_w.reshape(1, D_MODEL)

    xp = x_prompt
    xs = x_sample.reshape(nbs, D_MODEL)
    p_states = [[] for _ in range(6)]
    s_states = [[] for _ in range(6)]
    s_delta = None
    for i in range(DEPTH):
        j = i // 2
        normw = norm_w[i].reshape(1, D_MODEL)
        mod_p = mods[i, 0:nbp]
        mod_s = mods[i, nbp:nbp + nbs]
        if i % 2 == 0:
            w = _ab_weights(j, ab_in_w, ab_out_w, conv_a_w, conv_a_b, lru_gx_w, lru_gx_b, lru_ga_w, lru_ga_b,
                            lru_a_param, conv_b_w, gdn_a_log, gdn_dt_bias, gdn_norm_w)
            xp, ca, lh, cb, ds = _ab_prompt(xp, mod_p, normw, w, nbp, nt)
            for lst, val in zip(p_states[:4], (_from_time_major(ca, nbp), lh, _from_time_major(cb, nbp), ds)):
                lst.append(val)
            xs, ca, lh, cb, s_delta = _ab_sample(xs, mod_s, normw, w, _to_time_major(state_conv_a[j]),
                                                 state_lru[j], _to_time_major(state_conv_b[j]), state_delta, j,
                                                 s_delta)
            for lst, val in zip(s_states[:3], (_from_time_major(ca, nbs), lh, _from_time_major(cb, nbs))):
                lst.append(val)
        else:
            w = dict(
                w_in=c_in_w[j].astype(BF16), bre=_block_diag_in(bbr[j]).astype(BF16),
                bim=_block_diag_in(bbi[j]).astype(BF16), cre=_block_diag_out(s5_c_re[j]).astype(BF16),
                cim=_block_diag_out(s5_c_im[j]).astype(BF16), abr=abr[j].reshape(1, NS),
                abi=abi[j].reshape(1, NS), d=s5_d[j].reshape(1, D_MODEL), gluw=glu_w[j].astype(BF16),
                glub=glu_b[j].reshape(1, D_MODEL), w_out=c_out_w[j].astype(BF16), fnw=fnw)
            last = i == DEPTH - 1
            xp, sr, si = _s5_layer(xp, mod_p, normw, w, nbp, nt, None, last, batch_major_out=last)
            p_states[4].append(sr.reshape(nbp, G_C, P_C))
            p_states[5].append(si.reshape(nbp, G_C, P_C))
            xs, sr, si = _s5_layer(xs, mod_s, normw, w, nbs, 1,
                                   (state_s5_re[j].reshape(nbs, NS), state_s5_im[j].reshape(nbs, NS)), last)
            s_states[4].append(sr.reshape(nbs, G_C, P_C))
            s_states[5].append(si.reshape(nbs, G_C, P_C))
    y_prompt = xp
    y_sample = xs.reshape(nbs, 1, D_MODEL)
    stack = lambda lists: tuple(jnp.stack(l) for l in lists)
    s_out = stack(s_states[:3]) + (s_delta,) + stack(s_states[4:])
    return (y_prompt, y_sample) + stack(p_states) + s_out
```

```python
import functools

import jax
import jax.numpy as jnp
from jax import lax
from jax.experimental import pallas as pl
from jax.experimental.pallas import tpu as pltpu

F32 = jnp.float32
BF16 = jnp.bfloat16

D_MODEL = 1024
DEPTH = 4
CONV_W = 4
W_A = 1024
H_A = 8
BW_A = 128
LRU_C = 8.0
H_B = 8
DK = 128
DV = 128
W_B = H_B * DV
QKV_B = 3 * W_B
CG = 16
G_C = 64
P_C = 64
NS = G_C * P_C
EPS = 1e-6
LANE = 128
GDN_BLOCK = 64
GDN_SEQS = 4
S5_LANE_BLOCKS = D_MODEL // LANE
S5_BLOCK_STATE = NS // S5_LANE_BLOCKS
VMEM_LIMIT = 60 * 1024 * 1024


def _dot(a, b):
    return jnp.dot(a.astype(BF16), b.astype(BF16), preferred_element_type=F32)


def _dot_nt(a, b):
    return lax.dot_general(a.astype(BF16), b.astype(BF16), (((1,), (1,)), ((), ())),
                           preferred_element_type=F32)


def _silu(x):
    return x * jax.nn.sigmoid(x)


def _softplus(x):
    return jnp.maximum(x, 0.0) + jnp.log1p(jnp.exp(-jnp.abs(x)))


def _norm_mod(x, normw, mod_ref, nb):
    rows = x.shape[0]
    ms = jnp.mean(x * x, axis=-1, keepdims=True)
    y = x * lax.rsqrt(ms + EPS) * normw
    shift = mod_ref[:, 0:D_MODEL]
    scale = mod_ref[:, D_MODEL:2 * D_MODEL]
    y3 = y.reshape(rows // nb, nb, D_MODEL)
    return (y3 * (1.0 + scale)[None] + shift[None]).reshape(rows, D_MODEL)


def _residual(x, out, mod_ref, nb):
    rows = x.shape[0]
    gate = mod_ref[:, 2 * D_MODEL:3 * D_MODEL]
    return x + (out.reshape(rows // nb, nb, D_MODEL) * gate[None]).reshape(rows, D_MODEL)


def _conv_inplace(ext, lane0, w_ref, w_lane0, nb, rows, post):
    sl = slice(lane0, lane0 + LANE)
    wl = slice(w_lane0, w_lane0 + LANE)
    acc = ext[0:rows, sl] * w_ref[0:1, wl]
    for j in range(1, CONV_W):
        acc = acc + ext[j * nb:j * nb + rows, sl] * w_ref[j:j + 1, wl]
    tail = ext[rows:rows + 3 * nb, sl]
    ext[3 * nb:3 * nb + rows, sl] = post(acc)
    ext[0:3 * nb, sl] = tail


def _l2norm(t):
    return t * lax.rsqrt(jnp.sum(t * t, axis=-1, keepdims=True) + EPS)


def _ab_pre(first, nb, nt, reset_first, x, mod_ref, normw_ref, w_in_ref, w_ba_ref, caw_ref, cab_ref,
            gw_ref, gxb_ref, gab_ref, apar_ref, cbw_ref, alog_ref, dtb_ref,
            ext_a, ext_b, za, zbs, lb, gsc, bsc, h_s):
    rows = nb * nt
    hb = _norm_mod(x, normw_ref[...], mod_ref, nb).astype(BF16)
    tile = 2 * LANE

    def emit_tile(c0):
        part = _dot(hb, w_in_ref[:, c0:c0 + tile])
        for m in range(tile // LANE):
            c = c0 + m * LANE
            piece = part[:, m * LANE:(m + 1) * LANE]
            if c < W_A:
                ext_a[3 * nb:3 * nb + rows, c:c + LANE] = piece
            elif c < 2 * W_A:
                za[:, c - W_A:c - W_A + LANE] = piece
            elif c < 2 * W_A + QKV_B:
                ext_b[(c - 2 * W_A) // LANE, 3 * nb:3 * nb + rows, :] = piece
            else:
                zbs[(c - 2 * W_A - QKV_B) // LANE] = piece

    def lru_block(blk):
        sl = slice(blk * BW_A, (blk + 1) * BW_A)
        _conv_inplace(ext_a, blk * BW_A, caw_ref, blk * BW_A, nb, rows, lambda acc: acc + cab_ref[:, sl])
        xb = ext_a[3 * nb:3 * nb + rows, sl]
        xbb = xb.astype(BF16)
        gates = _dot(xbb, gw_ref[blk])
        gate_x = jax.nn.sigmoid(gates[:, 0:BW_A] + gxb_ref[:, sl])
        gate_a = jax.nn.sigmoid(gates[:, BW_A:2 * BW_A] + gab_ref[:, sl])
        log_a = -LRU_C * gate_a * _softplus(-apar_ref[:, sl])
        a = jnp.exp(log_a)
        m2 = 1.0 - a * a
        mult = jnp.where(m2 > 0.0, m2 * lax.rsqrt(m2), 0.0)
        if reset_first:
            row = lax.broadcasted_iota(jnp.int32, (rows, BW_A), 0)
            mult = jnp.where(jnp.logical_and(first, row < nb), 1.0, mult)
        bval = mult * gate_x * xb
        h = h_s[:, sl]
        for t in range(nt):
            r = slice(t * nb, (t + 1) * nb)
            h = a[r] * h + bval[r]
            lb[r, sl] = h
        h_s[:, sl] = h

    def qkv_slice(n):
        if n < H_B:
            post = lambda acc: _l2norm(_silu(acc)) * (DK ** -0.5)
        elif n < 2 * H_B:
            post = lambda acc: _l2norm(_silu(acc))
        else:
            post = _silu
        _conv_inplace(ext_b.at[n], 0, cbw_ref, n * LANE, nb, rows, post)

    col_xa, col_za, col_qkv, col_zb = 0, W_A, 2 * W_A, 2 * W_A + QKV_B
    tiles = ([col_xa + i * tile for i in range(W_A // tile)] + [col_qkv + i * tile for i in range(QKV_B // tile)]
             + [col_za + i * tile for i in range(W_A // tile)] + [col_zb + i * tile for i in range(W_B // tile)])
    n_xa = W_A // tile
    tasks = [(n_xa - 1, functools.partial(lru_block, blk)) for blk in range(H_A)]
    tasks += [(n_xa + n // 2, functools.partial(qkv_slice, n)) for n in range(QKV_B // LANE)]
    for ti, c0 in enumerate(tiles):
        emit_tile(c0)
        for _ in range(2):
            if tasks and tasks[0][0] <= ti:
                tasks.pop(0)[1]()
    ba = _dot(hb, w_ba_ref[...])
    bsc[...] = jax.nn.sigmoid(ba[:, 0:LANE])
    g = -jnp.exp(alog_ref[...]) * _softplus(ba[:, LANE:2 * LANE] + dtb_ref[...])
    acc = g[0:nb]
    gsc[0:nb, :] = acc
    for t in range(1, nt):
        acc = acc + g[t * nb:(t + 1) * nb]
        gsc[t * nb:(t + 1) * nb, :] = acc
    for _, task in tasks:
        task()


def _gdn_block_masks():
    n = 2 * GDN_BLOCK
    ri = lax.broadcasted_iota(jnp.int32, (n, n), 0)
    ci = lax.broadcasted_iota(jnp.int32, (n, n), 1)
    same = (ri >= GDN_BLOCK) == (ci >= GDN_BLOCK)
    tri = jnp.where(jnp.logical_and(same, ri >= ci), 1.0, 0.0).astype(F32)
    strict = jnp.where(jnp.logical_and(same, ri > ci), 1.0, 0.0).astype(F32)
    eye = jnp.where(ri == ci, 1.0, 0.0).astype(F32)
    levels = []
    for l in range(6):
        rb = ri >> l
        sub = jnp.logical_and((rb & 1) == 1, (ci >> l) == rb - 1)
        levels.append(jnp.where(jnp.logical_and(same, sub), 1.0, 0.0).astype(F32))
    return tri, strict, eye, levels


def _gdn_block(seqs, nb, ext_b, zbs, gsc, bsc, s_ref, gnw_ref, masks):
    c_len = GDN_BLOCK
    top = lax.broadcasted_iota(jnp.int32, (2 * c_len, LANE), 0) < c_len
    tri, strict, eye, levels = masks
    chains = [(si, p) for si in range(len(seqs)) for p in range(H_B // 2)]
    n = range(len(chains))

    def cat(a0, a1):
        return jnp.concatenate([a0, a1], axis=0)

    rows = [pl.ds(3 * nb + b, c_len, stride=nb) for b in seqs]
    zrows = [pl.ds(b, c_len, stride=nb) for b in seqs]

    def head_pair(base, si, p):
        return cat(ext_b[base + 2 * p, rows[si], :], ext_b[base + 2 * p + 1, rows[si], :])

    def col_pair(t, p):
        return cat(t[:, 2 * p:2 * p + 1], t[:, 2 * p + 1:2 * p + 2])

    gcb = [gsc[zr, :] for zr in zrows]
    betab = [bsc[zr, :] for zr in zrows]
    glast = [gsc[pl.ds((c_len - 1) * nb + b, 1), :] for b in seqs]
    q = [head_pair(0, si, p) for si, p in chains]
    k = [head_pair(H_B, si, p) for si, p in chains]
    v = [head_pair(2 * H_B, si, p) for si, p in chains]
    c = [col_pair(gcb[si], p) for si, p in chains]
    bcol = [col_pair(betab[si], p) for si, p in chains]
    gl = [col_pair(jnp.broadcast_to(glast[si], (c_len, LANE)), p) for si, p in chains]
    decay = []
    for i in n:
        cm = jnp.broadcast_to(c[i], (2 * c_len, 2 * c_len))
        decay.append(jnp.exp((cm - cm.T) * tri) * tri)
    kb = [k[i] * bcol[i] for i in n]
    a_mat = [_dot_nt(kb[i], k[i]) * decay[i] * strict for i in n]
    qk = [_dot_nt(q[i], k[i]) * decay[i] for i in n]
    x = [eye - a_mat[i] * levels[0] for i in n]
    for l in range(1, 6):
        t = [_dot(a_mat[i] * levels[l], x[i]) for i in n]
        x = [x[i] - _dot(x[i], t[i]) for i in n]
    sol = [_dot(x[i], jnp.concatenate([v[i] * bcol[i], kb[i] * jnp.exp(c[i])], axis=1)) for i in n]
    s0 = [s_ref[seqs[si], 2 * p] for si, p in chains]
    s1 = [s_ref[seqs[si], 2 * p + 1] for si, p in chains]

    def per_head(lhs, i):
        return cat(_dot(lhs[0:c_len], s0[i]), _dot(lhs[c_len:2 * c_len], s1[i]))

    ws = [per_head(sol[i][:, DV:2 * DV], i) for i in n]
    qs = [per_head(q[i] * jnp.exp(c[i]), i) for i in n]
    v_new = [sol[i][:, 0:DV] - ws[i] for i in n]
    o = [qs[i] + _dot(qk[i], v_new[i]) for i in n]
    upd = []
    for i in n:
        kdec = k[i] * jnp.exp(gl[i] - c[i])
        vblk = jnp.concatenate([jnp.where(top, v_new[i], 0.0), jnp.where(top, 0.0, v_new[i])], axis=1)
        upd.append(_dot(kdec.T, vblk))
    for i, (si, p) in enumerate(chains):
        b = seqs[si]
        h0, h1 = 2 * p, 2 * p + 1
        s_ref[b, h0] = s0[i] * jnp.exp(glast[si][:, h0:h0 + 1]) + upd[i][:, 0:DV]
        s_ref[b, h1] = s1[i] * jnp.exp(glast[si][:, h1:h1 + 1]) + upd[i][:, DV:2 * DV]
        on = o[i] * lax.rsqrt(jnp.mean(o[i] * o[i], axis=-1, keepdims=True) + EPS) * gnw_ref[...]
        zb = cat(zbs[h0, zrows[si], :], zbs[h1, zrows[si], :])
        og = on * _silu(zb)
        ext_b[h0, rows[si], :] = og[0:c_len]
        ext_b[h1, rows[si], :] = og[c_len:2 * c_len]


def _ab_post(x, nb, mod_ref, ya, o, w_out_ref):
    out = _dot(ya.astype(BF16), w_out_ref[0:W_A, :]) + _dot(o.astype(BF16), w_out_ref[W_A:W_A + W_B, :])
    return _residual(x, out, mod_ref, nb)


def _to_time_major_rows(x_ref, x_tm, nb, nt):
    for b in range(nb):
        for m in range(D_MODEL // LANE):
            x_tm[m, pl.ds(b, nt, stride=nb), :] = x_ref[b, :, m * LANE:(m + 1) * LANE]
    return jnp.concatenate([x_tm[m] for m in range(D_MODEL // LANE)], axis=1)


def _from_time_major_rows(y, y_ref, y_tm, nb, nt):
    for m in range(D_MODEL // LANE):
        y_tm[m] = y[:, m * LANE:(m + 1) * LANE]
    for b in range(nb):
        for m in range(D_MODEL // LANE):
            y_ref[b, :, m * LANE:(m + 1) * LANE] = y_tm[m, pl.ds(b, nt, stride=nb), :]


def _ab_prompt_kernel(nb, nt, batch_major_in, x_ref, mod_ref, normw_ref, w_in_ref, w_ba_ref, caw_ref, cab_ref,
                      gw_ref, gxb_ref, gab_ref, apar_ref, cbw_ref, alog_ref, dtb_ref, gnw_ref,
                      w_out_ref, y_ref, ca_out, lru_out, cb_out, s_ref,
                      ext_a, ext_b, za, zbs, lb, gsc, bsc, h_s, *x_tm):
    c = pl.program_id(0)
    rows = nb * nt

    @pl.when(c == 0)
    def _():
        ext_a[0:3 * nb, :] = jnp.zeros((3 * nb, W_A), F32)
        ext_b[:, 0:3 * nb, :] = jnp.zeros((QKV_B // LANE, 3 * nb, LANE), F32)
        h_s[...] = jnp.zeros(h_s.shape, F32)
        s_ref[...] = jnp.zeros(s_ref.shape, F32)

    x = _to_time_major_rows(x_ref, x_tm[0], nb, nt) if batch_major_in else x_ref[...]
    _ab_pre(c == 0, nb, nt, True, x, mod_ref, normw_ref, w_in_ref, w_ba_ref, caw_ref, cab_ref, gw_ref,
            gxb_ref, gab_ref, apar_ref, cbw_ref, alog_ref, dtb_ref,
            ext_a, ext_b, za, zbs, lb, gsc, bsc, h_s)

    masks = _gdn_block_masks()

    def per_seq_group(i, carry):
        _gdn_block([GDN_SEQS * i + s for s in range(GDN_SEQS)], nb, ext_b, zbs, gsc, bsc, s_ref, gnw_ref, masks)
        return carry

    lax.fori_loop(0, nb // GDN_SEQS, per_seq_group, 0)

    ya = lb[...] * _silu(za[...])
    o = jnp.concatenate([ext_b[h, 3 * nb:3 * nb + rows, :] for h in range(H_B)], axis=1)
    if batch_major_in:
        x = jnp.concatenate([x_tm[0][m] for m in range(D_MODEL // LANE)], axis=1)
    else:
        x = x_ref[...]
    y_ref[...] = _ab_post(x, nb, mod_ref, ya, o, w_out_ref)

    @pl.when(c == pl.num_programs(0) - 1)
    def _():
        ca_out[...] = ext_a[0:3 * nb, :]
        for n in range(QKV_B // LANE):
            cb_out[:, n * LANE:(n + 1) * LANE] = ext_b[n, 0:3 * nb, :]
        lru_out[...] = h_s[...]


def _ab_sample_pre_kernel(nb, x_ref, mod_ref, normw_ref, w_in_ref, w_ba_ref, caw_ref, cab_ref, gw_ref,
                          gxb_ref, gab_ref, apar_ref, cbw_ref, alog_ref, dtb_ref,
                          ca_in, lru_in, cb_in,
                          qkv_out, g_out, beta_out, zb_out, ya_out, ca_out, lru_out, cb_out,
                          ext_a, ext_b, za, zbs, lb, gsc, bsc, h_s):
    ext_a[0:3 * nb, :] = ca_in[...]
    for n in range(QKV_B // LANE):
        ext_b[n, 0:3 * nb, :] = cb_in[:, n * LANE:(n + 1) * LANE]
    h_s[...] = lru_in[...]
    _ab_pre(False, nb, 1, False, x_ref[...], mod_ref, normw_ref, w_in_ref, w_ba_ref, caw_ref, cab_ref,
            gw_ref, gxb_ref, gab_ref, apar_ref, cbw_ref, alog_ref, dtb_ref,
            ext_a, ext_b, za, zbs, lb, gsc, bsc, h_s)
    for n in range(QKV_B // LANE):
        qkv_out[:, n * LANE:(n + 1) * LANE] = ext_b[n, 3 * nb:4 * nb, :]
        cb_out[:, n * LANE:(n + 1) * LANE] = ext_b[n, 0:3 * nb, :]
    g_out[...] = gsc[...]
    beta_out[...] = bsc[...]
    for h in range(H_B):
        zb_out[:, h * LANE:(h + 1) * LANE] = zbs[h]
    ya_out[...] = lb[...] * _silu(za[...])
    ca_out[...] = ext_a[0:3 * nb, :]
    lru_out[...] = h_s[...]


def _gdn_step_kernel(bb, layer, n_layers, qkv_ref, g_ref, beta_ref, zb_ref, gnw_ref, s_in, *rest):
    if layer == 0:
        o_ref, s_all = rest
        for l in range(1, n_layers):
            s_all[l] = jnp.zeros(s_all.shape[1:], F32)
        s_out = s_all.at[0]
    else:
        _, o_ref, s_out = rest

    def per_seq(i, carry):
        g_row = g_ref[i]
        beta_row = beta_ref[i]
        heads = range(H_B)
        q = [qkv_ref[i, :, h * LANE:(h + 1) * LANE] for h in heads]
        k = [qkv_ref[i, :, H_B * DK + h * LANE:H_B * DK + (h + 1) * LANE] for h in heads]
        v = [qkv_ref[i, :, 2 * H_B * DK + h * LANE:2 * H_B * DK + (h + 1) * LANE] for h in heads]
        eg = [jnp.exp(g_row[:, h:h + 1]) for h in heads]
        kcol = [jnp.broadcast_to(k[h], (DK, DK)).T for h in heads]
        qcol = [jnp.broadcast_to(q[h], (DK, DK)).T for h in heads]
        s = [s_in[i, h] for h in heads]
        ks = [jnp.sum(kcol[h] * s[h], axis=0, keepdims=True) for h in heads]
        v_new = [beta_row[:, h:h + 1] * (v[h] - eg[h] * ks[h]) for h in heads]
        s_new = [eg[h] * s[h] + kcol[h] * v_new[h] for h in heads]
        for h in heads:
            s_out[i, h] = s_new[h]
        o = [jnp.sum(qcol[h] * s_new[h], axis=0, keepdims=True) for h in heads]
        for h in heads:
            on = o[h] * lax.rsqrt(jnp.mean(o[h] * o[h], axis=-1, keepdims=True) + EPS) * gnw_ref[...]
            zb = zb_ref[i, :, h * LANE:(h + 1) * LANE]
            o_ref[i, :, h * LANE:(h + 1) * LANE] = on * _silu(zb)
        return carry

    lax.fori_loop(0, bb, per_seq, 0)


def _ab_sample_post_kernel(nb, x_ref, mod_ref, ya_ref, o_ref, w_out_ref, y_ref):
    y_ref[...] = _ab_post(x_ref[...], nb, mod_ref, ya_ref[...], o_ref[...], w_out_ref)


def _s5_kernel(nb, nt, has_state, final_norm, batch_major_out, *refs):
    (x_ref, mod_ref, normw_ref, w_in_ref, bre_ref, bim_ref, cre_ref, cim_ref, abr_ref, abi_ref, d_ref,
     gluw_ref, glub_ref, w_out_ref, fnw_ref) = refs[:15]
    refs = refs[15:]
    if has_state:
        sre_in, sim_in = refs[:2]
        refs = refs[2:]
    y_ref, sre_out, sim_out, xs_re, xs_im, y_s = refs[:6]
    c = pl.program_id(0)
    rows = nb * nt

    @pl.when(c == 0)
    def _():
        if has_state:
            xs_re[0:nb, :] = sre_in[...]
            xs_im[0:nb, :] = sim_in[...]
        else:
            xs_re[0:nb, :] = jnp.zeros((nb, NS), F32)
            xs_im[0:nb, :] = jnp.zeros((nb, NS), F32)

    hb = _norm_mod(x_ref[...], normw_ref[...], mod_ref, nb).astype(BF16)
    uz = _dot(hb, w_in_ref[...])
    u = uz[:, 0:D_MODEL]
    z = uz[:, D_MODEL:2 * D_MODEL]
    ub = u.astype(BF16)
    for j in range(S5_LANE_BLOCKS):
        sl = slice(j * S5_BLOCK_STATE, (j + 1) * S5_BLOCK_STATE)
        cl = slice(j * LANE, (j + 1) * LANE)
        uj = ub[:, cl]
        xs_re[nb:nb + rows, sl] = _dot(uj, bre_ref[j])
        xs_im[nb:nb + rows, sl] = _dot(uj, bim_ref[j])
        ar = jnp.broadcast_to(abr_ref[:, sl], (nb, S5_BLOCK_STATE))
        ai = jnp.broadcast_to(abi_ref[:, sl], (nb, S5_BLOCK_STATE))
        sr, si = xs_re[0:nb, sl], xs_im[0:nb, sl]
        for t in range(nt):
            r = slice(nb + t * nb, 2 * nb + t * nb)
            sr, si = ar * sr - ai * si + xs_re[r, sl], ar * si + ai * sr + xs_im[r, sl]
            xs_re[r, sl] = sr
            xs_im[r, sl] = si
        xs_re[0:nb, sl] = sr
        xs_im[0:nb, sl] = si
        yj = (_dot(xs_re[nb:nb + rows, sl].astype(BF16), cre_ref[j])
              - _dot(xs_im[nb:nb + rows, sl].astype(BF16), cim_ref[j]))
        yj = yj + d_ref[:, cl] * u[:, cl]
        cdf = 0.5 * (1.0 + jnp.tanh(0.7978845608028654 * (yj + 0.044715 * (yj * yj * yj))))
        y_s[:, cl] = yj * cdf
    y = y_s[...]
    y = y * jax.nn.sigmoid(_dot(y.astype(BF16), gluw_ref[...]) + glub_ref[...])
    y = y * _silu(z)
    xn = _residual(x_ref[...], _dot(y.astype(BF16), w_out_ref[...]), mod_ref, nb)
    if final_norm:
        xn = xn * lax.rsqrt(jnp.mean(xn * xn, axis=-1, keepdims=True) + EPS) * fnw_ref[...]
    if batch_major_out:
        _from_time_major_rows(xn, y_ref, refs[6], nb, nt)
    else:
        y_ref[...] = xn

    @pl.when(c == pl.num_programs(0) - 1)
    def _():
        sre_out[...] = xs_re[0:nb, :]
        sim_out[...] = xs_im[0:nb, :]


def _s5_prep_kernel(are_ref, aim_ref, ldt_ref, bre_ref, bim_ref, abr_out, abi_out, bbr_out, bbi_out):
    a_re = are_ref[...]
    a_im = aim_ref[...]
    dt = jnp.exp(ldt_ref[...])
    mag = jnp.exp(a_re * dt)
    abr = mag * jnp.cos(a_im * dt)
    abi = mag * jnp.sin(a_im * dt)
    abr_out[...] = abr
    abi_out[...] = abi
    den = a_re * a_re + a_im * a_im
    nr = abr - 1.0
    cr = (nr * a_re + abi * a_im) / den
    ci = (abi * a_re - nr * a_im) / den
    b_re = bre_ref[...]
    b_im = bim_ref[...]
    bbr_out[...] = cr[:, None, :] * b_re - ci[:, None, :] * b_im
    bbi_out[...] = cr[:, None, :] * b_im + ci[:, None, :] * b_re


def _mod_kernel(c_ref, w_ref, b_ref, o_ref):
    c = c_ref[...]
    o_ref[...] = _dot(_silu(c).astype(BF16), w_ref[...].astype(BF16)) + b_ref[...]


def _full(shape):
    n = len(shape)
    return pl.BlockSpec(shape, lambda *_: (0,) * n)


def _const(shape):
    n = len(shape)
    return pl.BlockSpec(shape, lambda *_: (0,) * n, pipeline_mode=pl.Buffered(1))


def _params(n_grid):
    return pltpu.CompilerParams(dimension_semantics=("arbitrary",) * n_grid, vmem_limit_bytes=VMEM_LIMIT)


def _mod_all(c_all, mod_w, mod_b):
    n = c_all.shape[0]
    return pl.pallas_call(
        _mod_kernel,
        grid=(DEPTH, 3),
        in_specs=[pl.BlockSpec((n, D_MODEL), lambda i, j: (0, 0)),
                  pl.BlockSpec((None, D_MODEL, D_MODEL), lambda i, j: (i, 0, j)),
                  pl.BlockSpec((None, 1, D_MODEL), lambda i, j: (i, 0, j))],
        out_specs=pl.BlockSpec((None, n, D_MODEL), lambda i, j: (i, 0, j)),
        out_shape=jax.ShapeDtypeStruct((DEPTH, n, 3 * D_MODEL), F32),
        compiler_params=_params(2),
        name="mod_all",
    )(c_all, mod_w, mod_b.reshape(DEPTH, 1, 3 * D_MODEL))


def _ab_weights(j, ab_in_w, ab_out_w, conv_a_w, conv_a_b, lru_gx_w, lru_gx_b, lru_ga_w, lru_ga_b,
                lru_a_param, conv_b_w, gdn_a_log, gdn_dt_bias, gdn_norm_w):
    n_main = 2 * W_A + QKV_B + W_B
    w_in = ab_in_w[j, :, 0:n_main].astype(BF16)
    w_ba = jnp.zeros((D_MODEL, 2 * LANE), F32)
    w_ba = w_ba.at[:, 0:H_B].set(ab_in_w[j, :, n_main:n_main + H_B])
    w_ba = w_ba.at[:, LANE:LANE + H_B].set(ab_in_w[j, :, n_main + H_B:n_main + 2 * H_B]).astype(BF16)
    pad = lambda t: jnp.zeros((1, LANE), F32).at[0, 0:H_B].set(t)
    return dict(
        w_in=w_in, w_ba=w_ba, caw=conv_a_w[j], cab=conv_a_b[j].reshape(1, W_A),
        gw=jnp.concatenate([lru_gx_w[j], lru_ga_w[j]], axis=-1).astype(BF16),
        gxb=lru_gx_b[j].reshape(1, W_A), gab=lru_ga_b[j].reshape(1, W_A),
        apar=lru_a_param[j].reshape(1, W_A), cbw=conv_b_w[j], alog=pad(gdn_a_log[j]),
        dtb=pad(gdn_dt_bias[j]), gnw=gdn_norm_w[j].reshape(1, DV), w_out=ab_out_w[j].astype(BF16))


_AB_PRE_NAMES = ("w_in", "w_ba", "caw", "cab", "gw", "gxb", "gab", "apar", "cbw", "alog", "dtb")


def _ab_scratch(nb, rows):
    return [pltpu.VMEM((3 * nb + rows, W_A), F32), pltpu.VMEM((QKV_B // LANE, 3 * nb + rows, LANE), F32),
            pltpu.VMEM((rows, W_A), F32), pltpu.VMEM((H_B, rows, LANE), F32),
            pltpu.VMEM((rows, W_A), F32),
            pltpu.VMEM((rows, LANE), F32), pltpu.VMEM((rows, LANE), F32), pltpu.VMEM((nb, W_A), F32)]


def _ab_prompt(x, mod, normw, w, nb, nt):
    rows = nb * nt
    batch_major = x.ndim == 3
    n_rows = x.shape[0] * x.shape[1] if batch_major else x.shape[0]
    n_chunks = n_rows // rows
    pre = [w[k] for k in _AB_PRE_NAMES]
    if batch_major:
        x_spec = pl.BlockSpec((nb, nt, D_MODEL), lambda c: (0, c, 0))
        extra = [pltpu.VMEM((D_MODEL // LANE, rows, LANE), F32)]
    else:
        x_spec = pl.BlockSpec((rows, D_MODEL), lambda c: (c, 0))
        extra = []
    return pl.pallas_call(
        functools.partial(_ab_prompt_kernel, nb, nt, batch_major),
        grid=(n_chunks,),
        in_specs=[x_spec, _const(mod.shape), _const(normw.shape)]
        + [_const(t.shape) for t in pre] + [_const(w["gnw"].shape), _const(w["w_out"].shape)],
        out_specs=[pl.BlockSpec((rows, D_MODEL), lambda c: (c, 0)), _full((3 * nb, W_A)), _full((nb, W_A)),
                   _full((3 * nb, QKV_B)), _const((nb, H_B, DK, DV))],
        out_shape=[jax.ShapeDtypeStruct((n_rows, D_MODEL), F32), jax.ShapeDtypeStruct((3 * nb, W_A), F32),
                   jax.ShapeDtypeStruct((nb, W_A), F32), jax.ShapeDtypeStruct((3 * nb, QKV_B), F32),
                   jax.ShapeDtypeStruct((nb, H_B, DK, DV), F32)],
        scratch_shapes=_ab_scratch(nb, rows) + extra,
        compiler_params=_params(1),
        name="ab_prompt",
    )(x, mod, normw, *pre, w["gnw"], w["w_out"])


def _ab_sample(x, mod, normw, w, conv_a, lru_h, conv_b, delta_all, layer, delta_new_all):
    nb = x.shape[0]
    pre = [w[k] for k in _AB_PRE_NAMES]
    ins = [x, mod, normw, *pre, conv_a, lru_h, conv_b]
    outs = [(nb, QKV_B), (nb, LANE), (nb, LANE), (nb, W_B), (nb, W_A), (3 * nb, W_A), (nb, W_A),
            (3 * nb, QKV_B)]
    qkv, g, beta, zb, ya, ca_new, lru_new, cb_new = pl.pallas_call(
        functools.partial(_ab_sample_pre_kernel, nb),
        in_specs=[_full(t.shape) for t in ins],
        out_specs=[_full(s) for s in outs],
        out_shape=[jax.ShapeDtypeStruct(s, F32) for s in outs],
        scratch_shapes=_ab_scratch(nb, nb),
        compiler_params=pltpu.CompilerParams(vmem_limit_bytes=VMEM_LIMIT),
        name="ab_sample_pre",
    )(*ins)
    bb = 8
    row_block = lambda width: pl.BlockSpec((bb, 1, width), lambda i: (i, 0, 0))
    n_layers = delta_all.shape[0]
    state_block = pl.BlockSpec((None, bb, H_B, DK, DV), lambda i: (layer, i, 0, 0, 0))
    step_ins = [qkv.reshape(nb, 1, QKV_B), g.reshape(nb, 1, LANE), beta.reshape(nb, 1, LANE),
                zb.reshape(nb, 1, W_B), w["gnw"], delta_all]
    step_specs = [row_block(QKV_B), row_block(LANE), row_block(LANE), row_block(W_B), _full((1, DV)),
                  state_block]
    if layer == 0:
        out_state_block = pl.BlockSpec((n_layers, bb, H_B, DK, DV), lambda i: (0, i, 0, 0, 0))
        aliases = {}
    else:
        step_ins.append(delta_new_all)
        step_specs.append(pl.BlockSpec(memory_space=pl.ANY))
        out_state_block = state_block
        aliases = {len(step_ins) - 1: 1}
    o, delta_new_all = pl.pallas_call(
        functools.partial(_gdn_step_kernel, bb, layer, n_layers),
        grid=(nb // bb,),
        in_specs=step_specs,
        out_specs=[row_block(W_B), out_state_block],
        out_shape=[jax.ShapeDtypeStruct((nb, 1, W_B), F32), jax.ShapeDtypeStruct(delta_all.shape, F32)],
        input_output_aliases=aliases,
        compiler_params=_params(1),
        name="gdn_step",
    )(*step_ins)
    o = o.reshape(nb, W_B)
    y = pl.pallas_call(
        functools.partial(_ab_sample_post_kernel, nb),
        in_specs=[_full(x.shape), _full(mod.shape), _full(ya.shape), _full(o.shape), _full(w["w_out"].shape)],
        out_specs=_full(x.shape),
        out_shape=jax.ShapeDtypeStruct(x.shape, F32),
        compiler_params=pltpu.CompilerParams(vmem_limit_bytes=VMEM_LIMIT),
        name="ab_sample_post",
    )(x, mod, ya, o, w["w_out"])
    return y, ca_new, lru_new, cb_new, delta_new_all


def _s5_prep(s5_a_re, s5_a_im, s5_log_dt, s5_b_re, s5_b_im):
    n = s5_a_re.shape[0]
    gp = pl.BlockSpec((None, G_C, P_C), lambda i: (i, 0, 0))
    gcp = pl.BlockSpec((None, G_C, CG, P_C), lambda i: (i, 0, 0, 0))
    return pl.pallas_call(
        _s5_prep_kernel,
        grid=(n,),
        in_specs=[gp, gp, pl.BlockSpec((None, G_C, 1), lambda i: (i, 0, 0)), gcp, gcp],
        out_specs=[gp, gp, gcp, gcp],
        out_shape=[jax.ShapeDtypeStruct((n, G_C, P_C), F32)] * 2
        + [jax.ShapeDtypeStruct((n, G_C, CG, P_C), F32)] * 2,
        compiler_params=_params(1),
        name="s5_prep",
    )(s5_a_re, s5_a_im, s5_log_dt.reshape(n, G_C, 1), jnp.swapaxes(s5_b_re, 2, 3), jnp.swapaxes(s5_b_im, 2, 3))


def _block_diag_in(t):
    gl = G_C // S5_LANE_BLOCKS
    t = t.reshape(S5_LANE_BLOCKS, gl, CG, P_C)
    eye = jnp.eye(gl, dtype=t.dtype)
    return jnp.einsum("jgcp,gh->jgchp", t, eye).reshape(S5_LANE_BLOCKS, gl * CG, gl * P_C)


def _block_diag_out(t):
    gl = G_C // S5_LANE_BLOCKS
    t = t.reshape(S5_LANE_BLOCKS, gl, CG, P_C)
    eye = jnp.eye(gl, dtype=t.dtype)
    return jnp.einsum("jgcp,gh->jgphc", t, eye).reshape(S5_LANE_BLOCKS, gl * P_C, gl * CG)


def _s5_layer(x, mod, normw, w, nb, nt, state, final_norm, batch_major_out=False):
    rows = nb * nt
    n_chunks = x.shape[0] // rows
    consts = [mod, normw, w["w_in"], w["bre"], w["bim"], w["cre"], w["cim"], w["abr"], w["abi"], w["d"],
              w["gluw"], w["glub"], w["w_out"], w["fnw"]]
    ins = [x, *consts]
    in_specs = [pl.BlockSpec((rows, D_MODEL), lambda c: (c, 0))] + [_const(t.shape) for t in consts]
    if state is not None:
        ins += list(state)
        in_specs += [_const((nb, NS))] * 2
    if batch_major_out:
        y_spec = pl.BlockSpec((nb, nt, D_MODEL), lambda c: (0, c, 0))
        y_shape = (nb, x.shape[0] // nb, D_MODEL)
        extra = [pltpu.VMEM((D_MODEL // LANE, rows, LANE), F32)]
    else:
        y_spec = pl.BlockSpec((rows, D_MODEL), lambda c: (c, 0))
        y_shape = x.shape
        extra = []
    return pl.pallas_call(
        functools.partial(_s5_kernel, nb, nt, state is not None, final_norm, batch_major_out),
        grid=(n_chunks,),
        in_specs=in_specs,
        out_specs=[y_spec, _full((nb, NS)), _full((nb, NS))],
        out_shape=[jax.ShapeDtypeStruct(y_shape, F32), jax.ShapeDtypeStruct((nb, NS), F32),
                   jax.ShapeDtypeStruct((nb, NS), F32)],
        scratch_shapes=[pltpu.VMEM((nb + rows, NS), F32), pltpu.VMEM((nb + rows, NS), F32),
                        pltpu.VMEM((rows, D_MODEL), F32)] + extra,
        compiler_params=_params(1),
        name="s5_layer",
    )(*ins)


def _to_time_major(t):
    b, l, c = t.shape
    return jnp.swapaxes(t, 0, 1).reshape(l * b, c)


def _from_time_major(t, b):
    lb, c = t.shape
    return jnp.swapaxes(t.reshape(lb // b, b, c), 0, 1)


def kernel(x_prompt, x_sample, c_prompt, c_sample, state_conv_a, state_lru, state_conv_b, state_delta, state_s5_re, state_s5_im, norm_w, mod_w, mod_b, ab_in_w, ab_out_w, conv_a_w, conv_a_b, lru_gx_w, lru_gx_b, lru_ga_w, lru_ga_b, lru_a_param, conv_b_w, gdn_a_log, gdn_dt_bias, gdn_norm_w, c_in_w, c_out_w, s5_a_re, s5_a_im, s5_b_re, s5_b_im, s5_c_re, s5_c_im, s5_d, s5_log_dt, glu_w, glu_b, final_norm_w):
    nbp = x_prompt.shape[0]
    nbs = x_sample.shape[0]
    nt = GDN_BLOCK

    mods = _mod_all(jnp.concatenate([c_prompt, c_sample], axis=0), mod_w, mod_b)
    abr, abi, bbr, bbi = _s5_prep(s5_a_re, s5_a_im, s5_log_dt, s5_b_re, s5_b_im)
    fnw = final_norm_w.reshape(1, D_MODEL)

    xp = x_prompt
    xs = x_sample.reshape(nbs, D_MODEL)
    p_states = [[] for _ in range(6)]
    s_states = [[] for _ in range(6)]
    s_delta = None
    for i in range(DEPTH):
        j = i // 2
        normw = norm_w[i].reshape(1, D_MODEL)
        mod_p = mods[i, 0:nbp]
        mod_s = mods[i, nbp:nbp + nbs]
        if i % 2 == 0:
            w = _ab_weights(j, ab_in_w, ab_out_w, conv_a_w, conv_a_b, lru_gx_w, lru_gx_b, lru_ga_w, lru_ga_b,
                            lru_a_param, conv_b_w, gdn_a_log, gdn_dt_bias, gdn_norm_w)
            xp, ca, lh, cb, ds = _ab_prompt(xp, mod_p, normw, w, nbp, nt)
            for lst, val in zip(p_states[:4], (_from_time_major(ca, nbp), lh, _from_time_major(cb, nbp), ds)):
                lst.append(val)
            xs, ca, lh, cb, s_delta = _ab_sample(xs, mod_s, normw, w, _to_time_major(state_conv_a[j]),
                                                 state_lru[j], _to_time_major(state_conv_b[j]), state_delta, j,
                                                 s_delta)
            for lst, val in zip(s_states[:3], (_from_time_major(ca, nbs), lh, _from_time_major(cb, nbs))):
                lst.append(val)
        else:
            w = dict(
                w_in=c_in_w[j].astype(BF16), bre=_block_diag_in(bbr[j]).astype(BF16),
                bim=_block_diag_in(bbi[j]).astype(BF16), cre=_block_diag_out(s5_c_re[j]).astype(BF16),
                cim=_block_diag_out(s5_c_im[j]).astype(BF16), abr=abr[j].reshape(1, NS),
                abi=abi[j].reshape(1, NS), d=s5_d[j].reshape(1, D_MODEL), gluw=glu_w[j].astype(BF16),
                glub=glu_b[j].reshape(1, D_MODEL), w_out=c_out_w[j].astype(BF16), fnw=fnw)
            last = i == DEPTH - 1
            xp, sr, si = _s5_layer(xp, mod_p, normw, w, nbp, nt, None, last, batch_major_out=last)
            p_states[4].append(sr.reshape(nbp, G_C, P_C))
            p_states[5].append(si.reshape(nbp, G_C, P_C))
            xs, sr, si = _s5_layer(xs, mod_s, normw, w, nbs, 1,
                                   (state_s5_re[j].reshape(nbs, NS), state_s5_im[j].reshape(nbs, NS)), last)
            s_states[4].append(sr.reshape(nbs, G_C, P_C))
            s_states[5].append(si.reshape(nbs, G_C, P_C))
    y_prompt = xp
    y_sample = xs.reshape(nbs, 1, D_MODEL)
    stack = lambda lists: tuple(jnp.stack(l) for l in lists)
    s_out = stack(s_states[:3]) + (s_delta,) + stack(s_states[4:])
    return (y_prompt, y_sample) + stack(p_states) + s_out
```

```python
import functools

import jax
import jax.numpy as jnp
from jax import lax
from jax.experimental import pallas as pl
from jax.experimental.pallas import tpu as pltpu

F32 = jnp.float32
BF16 = jnp.bfloat16

D_MODEL = 1024
DEPTH = 4
CONV_W = 4
W_A = 1024
H_A = 8
BW_A = 128
LRU_C = 8.0
H_B = 8
DK = 128
DV = 128
W_B = H_B * DV
QKV_B = 3 * W_B
CG = 16
G_C = 64
P_C = 64
NS = G_C * P_C
EPS = 1e-6
LANE = 128
GDN_BLOCK = 64
GDN_SEQS = 4
TILE_COST = 2
LRU_TASK_COST = 3
SLICE_TASK_COST = 1
S5_LANE_BLOCKS = D_MODEL // LANE
S5_BLOCK_STATE = NS // S5_LANE_BLOCKS
VMEM_LIMIT = 60 * 1024 * 1024


def _dot(a, b):
    return jnp.dot(a.astype(BF16), b.astype(BF16), preferred_element_type=F32)


def _dot_nt(a, b):
    return lax.dot_general(a.astype(BF16), b.astype(BF16), (((1,), (1,)), ((), ())),
                           preferred_element_type=F32)


def _silu(x):
    return x * jax.nn.sigmoid(x)


def _softplus(x):
    return jnp.maximum(x, 0.0) + jnp.log1p(jnp.exp(-jnp.abs(x)))


def _norm_mod(x, normw, mod_ref, nb):
    rows = x.shape[0]
    ms = jnp.mean(x * x, axis=-1, keepdims=True)
    y = x * lax.rsqrt(ms + EPS) * normw
    shift = mod_ref[:, 0:D_MODEL]
    scale = mod_ref[:, D_MODEL:2 * D_MODEL]
    y3 = y.reshape(rows // nb, nb, D_MODEL)
    return (y3 * (1.0 + scale)[None] + shift[None]).reshape(rows, D_MODEL)


def _residual(x, out, mod_ref, nb):
    rows = x.shape[0]
    gate = mod_ref[:, 2 * D_MODEL:3 * D_MODEL]
    return x + (out.reshape(rows // nb, nb, D_MODEL) * gate[None]).reshape(rows, D_MODEL)


def _conv_inplace(ext, lane0, w_ref, w_lane0, nb, rows, post):
    sl = slice(lane0, lane0 + LANE)
    wl = slice(w_lane0, w_lane0 + LANE)
    acc = ext[0:rows, sl] * w_ref[0:1, wl]
    for j in range(1, CONV_W):
        acc = acc + ext[j * nb:j * nb + rows, sl] * w_ref[j:j + 1, wl]
    tail = ext[rows:rows + 3 * nb, sl]
    ext[3 * nb:3 * nb + rows, sl] = post(acc)
    ext[0:3 * nb, sl] = tail


def _l2norm(t):
    return t * lax.rsqrt(jnp.sum(t * t, axis=-1, keepdims=True) + EPS)


def _ab_pre(first, nb, nt, reset_first, x, mod_ref, normw_ref, w_in_ref, w_ba_ref, caw_ref, cab_ref,
            gw_ref, gxb_ref, gab_ref, apar_ref, cbw_ref, alog_ref, dtb_ref,
            ext_a, ext_b, za, zbs, lb, gsc, bsc, h_s):
    rows = nb * nt
    hb = _norm_mod(x, normw_ref[...], mod_ref, nb).astype(BF16)
    tile = 2 * LANE

    def emit_tile(c0):
        part = _dot_nt(hb, w_in_ref[c0:c0 + tile, :])
        for m in range(tile // LANE):
            c = c0 + m * LANE
            piece = part[:, m * LANE:(m + 1) * LANE]
            if c < W_A:
                ext_a[3 * nb:3 * nb + rows, c:c + LANE] = piece
            elif c < 2 * W_A:
                za[:, c - W_A:c - W_A + LANE] = piece
            elif c < 2 * W_A + QKV_B:
                ext_b[(c - 2 * W_A) // LANE, 3 * nb:3 * nb + rows, :] = piece
            else:
                zbs[(c - 2 * W_A - QKV_B) // LANE] = piece

    def lru_block(blk):
        sl = slice(blk * BW_A, (blk + 1) * BW_A)
        _conv_inplace(ext_a, blk * BW_A, caw_ref, blk * BW_A, nb, rows, lambda acc: acc + cab_ref[:, sl])
        xb = ext_a[3 * nb:3 * nb + rows, sl]
        xbb = xb.astype(BF16)
        gates = _dot(xbb, gw_ref[blk])
        gate_x = jax.nn.sigmoid(gates[:, 0:BW_A] + gxb_ref[:, sl])
        gate_a = jax.nn.sigmoid(gates[:, BW_A:2 * BW_A] + gab_ref[:, sl])
        log_a = -LRU_C * gate_a * _softplus(-apar_ref[:, sl])
        a = jnp.exp(log_a)
        m2 = 1.0 - a * a
        mult = jnp.where(m2 > 0.0, m2 * lax.rsqrt(m2), 0.0)
        if reset_first:
            row = lax.broadcasted_iota(jnp.int32, (rows, BW_A), 0)
            mult = jnp.where(jnp.logical_and(first, row < nb), 1.0, mult)
        bval = mult * gate_x * xb
        h = h_s[:, sl]
        for t in range(nt):
            r = slice(t * nb, (t + 1) * nb)
            h = a[r] * h + bval[r]
            lb[r, sl] = h
        h_s[:, sl] = h

    def qkv_slice(n):
        if n < H_B:
            post = lambda acc: _l2norm(_silu(acc)) * (DK ** -0.5)
        elif n < 2 * H_B:
            post = lambda acc: _l2norm(_silu(acc))
        else:
            post = _silu
        _conv_inplace(ext_b.at[n], 0, cbw_ref, n * LANE, nb, rows, post)

    col_xa, col_za, col_qkv, col_zb = 0, W_A, 2 * W_A, 2 * W_A + QKV_B
    tiles = ([col_xa + i * tile for i in range(W_A // tile)] + [col_qkv + i * tile for i in range(QKV_B // tile)]
             + [col_za + i * tile for i in range(W_A // tile)] + [col_zb + i * tile for i in range(W_B // tile)])
    n_xa = W_A // tile
    tasks = []
    for blk in range(H_A):
        tasks.append((n_xa - 1, LRU_TASK_COST, functools.partial(lru_block, blk)))
        for n in range(3 * blk, 3 * blk + 3):
            tasks.append((n_xa + n // 2, SLICE_TASK_COST, functools.partial(qkv_slice, n)))
    matmul_cost = vector_cost = 0
    for ti, c0 in enumerate(tiles):
        emit_tile(c0)
        matmul_cost += TILE_COST
        while tasks and tasks[0][0] <= ti and vector_cost + tasks[0][1] <= matmul_cost:
            _, cost, task = tasks.pop(0)
            task()
            vector_cost += cost
    ba = _dot_nt(hb, w_ba_ref[...])
    bsc[...] = jax.nn.sigmoid(ba[:, 0:LANE])
    g = -jnp.exp(alog_ref[...]) * _softplus(ba[:, LANE:2 * LANE] + dtb_ref[...])
    acc = g[0:nb]
    gsc[0:nb, :] = acc
    for t in range(1, nt):
        acc = acc + g[t * nb:(t + 1) * nb]
        gsc[t * nb:(t + 1) * nb, :] = acc
    for _, _, task in tasks:
        task()


def _gdn_block_masks():
    n = 2 * GDN_BLOCK
    ri = lax.broadcasted_iota(jnp.int32, (n, n), 0)
    ci = lax.broadcasted_iota(jnp.int32, (n, n), 1)
    same = (ri >= GDN_BLOCK) == (ci >= GDN_BLOCK)
    tri = jnp.where(jnp.logical_and(same, ri >= ci), 1.0, 0.0).astype(F32)
    strict = jnp.where(jnp.logical_and(same, ri > ci), 1.0, 0.0).astype(F32)
    eye = jnp.where(ri == ci, 1.0, 0.0).astype(F32)
    levels = []
    for l in range(6):
        rb = ri >> l
        sub = jnp.logical_and((rb & 1) == 1, (ci >> l) == rb - 1)
        levels.append(jnp.where(jnp.logical_and(same, sub), 1.0, 0.0).astype(F32))
    return tri, strict, eye, levels


def _gdn_block(seqs, nb, ext_b, zbs, gsc, bsc, s_ref, gnw_ref, masks):
    c_len = GDN_BLOCK
    top = lax.broadcasted_iota(jnp.int32, (2 * c_len, LANE), 0) < c_len
    tri, strict, eye, levels = masks
    chains = [(si, p) for si in range(len(seqs)) for p in range(H_B // 2)]
    n = range(len(chains))

    def cat(a0, a1):
        return jnp.concatenate([a0, a1], axis=0)

    rows = [pl.ds(3 * nb + b, c_len, stride=nb) for b in seqs]
    zrows = [pl.ds(b, c_len, stride=nb) for b in seqs]

    def head_pair(base, si, p):
        return cat(ext_b[base + 2 * p, rows[si], :], ext_b[base + 2 * p + 1, rows[si], :])

    def col_pair(t, p):
        return cat(t[:, 2 * p:2 * p + 1], t[:, 2 * p + 1:2 * p + 2])

    gcb = [gsc[zr, :] for zr in zrows]
    betab = [bsc[zr, :] for zr in zrows]
    glast = [gsc[pl.ds((c_len - 1) * nb + b, 1), :] for b in seqs]
    q = [head_pair(0, si, p) for si, p in chains]
    k = [head_pair(H_B, si, p) for si, p in chains]
    v = [head_pair(2 * H_B, si, p) for si, p in chains]
    c = [col_pair(gcb[si], p) for si, p in chains]
    bcol = [col_pair(betab[si], p) for si, p in chains]
    gl = [col_pair(jnp.broadcast_to(glast[si], (c_len, LANE)), p) for si, p in chains]
    gct = [cat(gcb[s], gcb[s + 1]).T for s in range(0, len(seqs), 2)]
    decay = []
    for i, (si, p) in enumerate(chains):
        half = slice((si % 2) * c_len, (si % 2 + 1) * c_len)
        crow = jnp.concatenate([gct[si // 2][2 * p:2 * p + 1, half], gct[si // 2][2 * p + 1:2 * p + 2, half]],
                               axis=1)
        cm = jnp.broadcast_to(c[i], (2 * c_len, 2 * c_len))
        decay.append(jnp.exp((cm - jnp.broadcast_to(crow, (2 * c_len, 2 * c_len))) * tri) * tri)
    kb = [k[i] * bcol[i] for i in n]
    a_mat = [_dot_nt(kb[i], k[i]) * decay[i] * strict for i in n]
    qk = [_dot_nt(q[i], k[i]) * decay[i] for i in n]
    x = [eye - a_mat[i] * levels[0] for i in n]
    for l in range(1, 6):
        t = [_dot(a_mat[i] * levels[l], x[i]) for i in n]
        x = [x[i] - _dot(x[i], t[i]) for i in n]
    sol = [_dot(x[i], jnp.concatenate([v[i] * bcol[i], kb[i] * jnp.exp(c[i])], axis=1)) for i in n]
    s0 = [s_ref[seqs[si], 2 * p] for si, p in chains]
    s1 = [s_ref[seqs[si], 2 * p + 1] for si, p in chains]

    def per_head(lhs, i):
        return cat(_dot(lhs[0:c_len], s0[i]), _dot(lhs[c_len:2 * c_len], s1[i]))

    ws = [per_head(sol[i][:, DV:2 * DV], i) for i in n]
    qs = [per_head(q[i] * jnp.exp(c[i]), i) for i in n]
    v_new = [sol[i][:, 0:DV] - ws[i] for i in n]
    o = [qs[i] + _dot(qk[i], v_new[i]) for i in n]
    upd = []
    for i in n:
        kdec = k[i] * jnp.exp(gl[i] - c[i])
        vblk = jnp.concatenate([jnp.where(top, v_new[i], 0.0), jnp.where(top, 0.0, v_new[i])], axis=1)
        upd.append(_dot(kdec.T, vblk))
    for i, (si, p) in enumerate(chains):
        b = seqs[si]
        h0, h1 = 2 * p, 2 * p + 1
        s_ref[b, h0] = s0[i] * jnp.exp(glast[si][:, h0:h0 + 1]) + upd[i][:, 0:DV]
        s_ref[b, h1] = s1[i] * jnp.exp(glast[si][:, h1:h1 + 1]) + upd[i][:, DV:2 * DV]
        on = o[i] * lax.rsqrt(jnp.mean(o[i] * o[i], axis=-1, keepdims=True) + EPS) * gnw_ref[...]
        zb = cat(zbs[h0, zrows[si], :], zbs[h1, zrows[si], :])
        og = on * _silu(zb)
        ext_b[h0, rows[si], :] = og[0:c_len]
        ext_b[h1, rows[si], :] = og[c_len:2 * c_len]


def _ab_post(x, nb, mod_ref, ya, o, w_out_ref):
    out = _dot(ya.astype(BF16), w_out_ref[0:W_A, :]) + _dot(o.astype(BF16), w_out_ref[W_A:W_A + W_B, :])
    return _residual(x, out, mod_ref, nb)


def _to_time_major_rows(x_ref, x_tm, nb, nt):
    for b in range(nb):
        for m in range(D_MODEL // LANE):
            x_tm[m, pl.ds(b, nt, stride=nb), :] = x_ref[b, :, m * LANE:(m + 1) * LANE]
    return jnp.concatenate([x_tm[m] for m in range(D_MODEL // LANE)], axis=1)


def _from_time_major_rows(y, y_ref, y_tm, nb, nt):
    for m in range(D_MODEL // LANE):
        y_tm[m] = y[:, m * LANE:(m + 1) * LANE]
    for b in range(nb):
        for m in range(D_MODEL // LANE):
            y_ref[b, :, m * LANE:(m + 1) * LANE] = y_tm[m, pl.ds(b, nt, stride=nb), :]


def _ab_prompt_kernel(nb, nt, batch_major_in, x_ref, mod_ref, normw_ref, w_in_ref, w_ba_ref, caw_ref, cab_ref,
                      gw_ref, gxb_ref, gab_ref, apar_ref, cbw_ref, alog_ref, dtb_ref, gnw_ref,
                      w_out_ref, y_ref, ca_out, lru_out, cb_out, s_ref,
                      ext_a, ext_b, za, zbs, lb, gsc, bsc, h_s, *x_tm):
    c = pl.program_id(0)
    rows = nb * nt

    @pl.when(c == 0)
    def _():
        ext_a[0:3 * nb, :] = jnp.zeros((3 * nb, W_A), F32)
        ext_b[:, 0:3 * nb, :] = jnp.zeros((QKV_B // LANE, 3 * nb, LANE), F32)
        h_s[...] = jnp.zeros(h_s.shape, F32)
        s_ref[...] = jnp.zeros(s_ref.shape, F32)

    x = _to_time_major_rows(x_ref, x_tm[0], nb, nt) if batch_major_in else x_ref[...]
    _ab_pre(c == 0, nb, nt, True, x, mod_ref, normw_ref, w_in_ref, w_ba_ref, caw_ref, cab_ref, gw_ref,
            gxb_ref, gab_ref, apar_ref, cbw_ref, alog_ref, dtb_ref,
            ext_a, ext_b, za, zbs, lb, gsc, bsc, h_s)

    masks = _gdn_block_masks()

    def per_seq_group(i, carry):
        _gdn_block([GDN_SEQS * i + s for s in range(GDN_SEQS)], nb, ext_b, zbs, gsc, bsc, s_ref, gnw_ref, masks)
        return carry

    lax.fori_loop(0, nb // GDN_SEQS, per_seq_group, 0)

    ya = lb[...] * _silu(za[...])
    o = jnp.concatenate([ext_b[h, 3 * nb:3 * nb + rows, :] for h in range(H_B)], axis=1)
    if batch_major_in:
        x = jnp.concatenate([x_tm[0][m] for m in range(D_MODEL // LANE)], axis=1)
    else:
        x = x_ref[...]
    y_ref[...] = _ab_post(x, nb, mod_ref, ya, o, w_out_ref)

    @pl.when(c == pl.num_programs(0) - 1)
    def _():
        ca_out[...] = ext_a[0:3 * nb, :]
        for n in range(QKV_B // LANE):
            cb_out[:, n * LANE:(n + 1) * LANE] = ext_b[n, 0:3 * nb, :]
        lru_out[...] = h_s[...]


def _ab_sample_pre_kernel(nb, x_ref, mod_ref, normw_ref, w_in_ref, w_ba_ref, caw_ref, cab_ref, gw_ref,
                          gxb_ref, gab_ref, apar_ref, cbw_ref, alog_ref, dtb_ref,
                          ca_in, lru_in, cb_in,
                          qkv_out, g_out, beta_out, zb_out, ya_out, ca_out, lru_out, cb_out,
                          ext_a, ext_b, za, zbs, lb, gsc, bsc, h_s):
    ext_a[0:3 * nb, :] = ca_in[...]
    for n in range(QKV_B // LANE):
        ext_b[n, 0:3 * nb, :] = cb_in[:, n * LANE:(n + 1) * LANE]
    h_s[...] = lru_in[...]
    _ab_pre(False, nb, 1, False, x_ref[...], mod_ref, normw_ref, w_in_ref, w_ba_ref, caw_ref, cab_ref,
            gw_ref, gxb_ref, gab_ref, apar_ref, cbw_ref, alog_ref, dtb_ref,
            ext_a, ext_b, za, zbs, lb, gsc, bsc, h_s)
    for n in range(QKV_B // LANE):
        qkv_out[:, n * LANE:(n + 1) * LANE] = ext_b[n, 3 * nb:4 * nb, :]
        cb_out[:, n * LANE:(n + 1) * LANE] = ext_b[n, 0:3 * nb, :]
    g_out[...] = gsc[...]
    beta_out[...] = bsc[...]
    for h in range(H_B):
        zb_out[:, h * LANE:(h + 1) * LANE] = zbs[h]
    ya_out[...] = lb[...] * _silu(za[...])
    ca_out[...] = ext_a[0:3 * nb, :]
    lru_out[...] = h_s[...]


def _gdn_step_kernel(bb, layer, n_layers, qkv_ref, g_ref, beta_ref, zb_ref, gnw_ref, s_in, *rest):
    if layer == 0:
        o_ref, s_all = rest
        for l in range(1, n_layers):
            s_all[l] = jnp.zeros(s_all.shape[1:], F32)
        s_out = s_all.at[0]
    else:
        _, o_ref, s_out = rest

    def per_seq(i, carry):
        g_row = g_ref[i]
        beta_row = beta_ref[i]
        heads = range(H_B)
        q = [qkv_ref[i, :, h * LANE:(h + 1) * LANE] for h in heads]
        k = [qkv_ref[i, :, H_B * DK + h * LANE:H_B * DK + (h + 1) * LANE] for h in heads]
        v = [qkv_ref[i, :, 2 * H_B * DK + h * LANE:2 * H_B * DK + (h + 1) * LANE] for h in heads]
        eg = [jnp.exp(g_row[:, h:h + 1]) for h in heads]
        kcol = [jnp.broadcast_to(k[h], (DK, DK)).T for h in heads]
        qcol = [jnp.broadcast_to(q[h], (DK, DK)).T for h in heads]
        s = [s_in[i, h] for h in heads]
        ks = [jnp.sum(kcol[h] * s[h], axis=0, keepdims=True) for h in heads]
        v_new = [beta_row[:, h:h + 1] * (v[h] - eg[h] * ks[h]) for h in heads]
        s_new = [eg[h] * s[h] + kcol[h] * v_new[h] for h in heads]
        for h in heads:
            s_out[i, h] = s_new[h]
        o = [jnp.sum(qcol[h] * s_new[h], axis=0, keepdims=True) for h in heads]
        for h in heads:
            on = o[h] * lax.rsqrt(jnp.mean(o[h] * o[h], axis=-1, keepdims=True) + EPS) * gnw_ref[...]
            zb = zb_ref[i, :, h * LANE:(h + 1) * LANE]
            o_ref[i, :, h * LANE:(h + 1) * LANE] = on * _silu(zb)
        return carry

    lax.fori_loop(0, bb, per_seq, 0)


def _ab_sample_post_kernel(nb, x_ref, mod_ref, ya_ref, o_ref, w_out_ref, y_ref):
    y_ref[...] = _ab_post(x_ref[...], nb, mod_ref, ya_ref[...], o_ref[...], w_out_ref)


def _s5_kernel(nb, nt, has_state, final_norm, batch_major_out, *refs):
    (x_ref, mod_ref, normw_ref, w_in_ref, bre_ref, bim_ref, cre_ref, cim_ref, abr_ref, abi_ref, d_ref,
     gluw_ref, glub_ref, w_out_ref, fnw_ref) = refs[:15]
    refs = refs[15:]
    if has_state:
        sre_in, sim_in = refs[:2]
        refs = refs[2:]
    y_ref, sre_out, sim_out, xs_re, xs_im, y_s = refs[:6]
    c = pl.program_id(0)
    rows = nb * nt

    @pl.when(c == 0)
    def _():
        if has_state:
            xs_re[0:nb, :] = sre_in[...]
            xs_im[0:nb, :] = sim_in[...]
        else:
            xs_re[0:nb, :] = jnp.zeros((nb, NS), F32)
            xs_im[0:nb, :] = jnp.zeros((nb, NS), F32)

    hb = _norm_mod(x_ref[...], normw_ref[...], mod_ref, nb).astype(BF16)
    uz = _dot(hb, w_in_ref[...])
    u = uz[:, 0:D_MODEL]
    z = uz[:, D_MODEL:2 * D_MODEL]
    ub = u.astype(BF16)
    for j in range(S5_LANE_BLOCKS):
        sl = slice(j * S5_BLOCK_STATE, (j + 1) * S5_BLOCK_STATE)
        cl = slice(j * LANE, (j + 1) * LANE)
        uj = ub[:, cl]
        xs_re[nb:nb + rows, sl] = _dot(uj, bre_ref[j])
        xs_im[nb:nb + rows, sl] = _dot(uj, bim_ref[j])
        ar = jnp.broadcast_to(abr_ref[:, sl], (nb, S5_BLOCK_STATE))
        ai = jnp.broadcast_to(abi_ref[:, sl], (nb, S5_BLOCK_STATE))
        sr, si = xs_re[0:nb, sl], xs_im[0:nb, sl]
        for t in range(nt):
            r = slice(nb + t * nb, 2 * nb + t * nb)
            sr, si = ar * sr - ai * si + xs_re[r, sl], ar * si + ai * sr + xs_im[r, sl]
            xs_re[r, sl] = sr
            xs_im[r, sl] = si
        xs_re[0:nb, sl] = sr
        xs_im[0:nb, sl] = si
        yj = (_dot(xs_re[nb:nb + rows, sl].astype(BF16), cre_ref[j])
              - _dot(xs_im[nb:nb + rows, sl].astype(BF16), cim_ref[j]))
        yj = yj + d_ref[:, cl] * u[:, cl]
        cdf = 0.5 * (1.0 + jnp.tanh(0.7978845608028654 * (yj + 0.044715 * (yj * yj * yj))))
        y_s[:, cl] = yj * cdf
    y = y_s[...]
    y = y * jax.nn.sigmoid(_dot(y.astype(BF16), gluw_ref[...]) + glub_ref[...])
    y = y * _silu(z)
    xn = _residual(x_ref[...], _dot(y.astype(BF16), w_out_ref[...]), mod_ref, nb)
    if final_norm:
        xn = xn * lax.rsqrt(jnp.mean(xn * xn, axis=-1, keepdims=True) + EPS) * fnw_ref[...]
    if batch_major_out:
        _from_time_major_rows(xn, y_ref, refs[6], nb, nt)
    else:
        y_ref[...] = xn

    @pl.when(c == pl.num_programs(0) - 1)
    def _():
        sre_out[...] = xs_re[0:nb, :]
        sim_out[...] = xs_im[0:nb, :]


def _s5_prep_kernel(are_ref, aim_ref, ldt_ref, bre_ref, bim_ref, abr_out, abi_out, bbr_out, bbi_out):
    a_re = are_ref[...]
    a_im = aim_ref[...]
    dt = jnp.exp(ldt_ref[...])
    mag = jnp.exp(a_re * dt)
    abr = mag * jnp.cos(a_im * dt)
    abi = mag * jnp.sin(a_im * dt)
    abr_out[...] = abr
    abi_out[...] = abi
    den = a_re * a_re + a_im * a_im
    nr = abr - 1.0
    cr = (nr * a_re + abi * a_im) / den
    ci = (abi * a_re - nr * a_im) / den
    b_re = bre_ref[...]
    b_im = bim_ref[...]
    bbr_out[...] = cr[:, None, :] * b_re - ci[:, None, :] * b_im
    bbi_out[...] = cr[:, None, :] * b_im + ci[:, None, :] * b_re


def _mod_kernel(c_ref, w_ref, b_ref, o_ref):
    c = c_ref[...]
    o_ref[...] = _dot(_silu(c).astype(BF16), w_ref[...].astype(BF16)) + b_ref[...]


def _full(shape):
    n = len(shape)
    return pl.BlockSpec(shape, lambda *_: (0,) * n)


def _const(shape):
    n = len(shape)
    return pl.BlockSpec(shape, lambda *_: (0,) * n, pipeline_mode=pl.Buffered(1))


def _params(n_grid):
    return pltpu.CompilerParams(dimension_semantics=("arbitrary",) * n_grid, vmem_limit_bytes=VMEM_LIMIT)


def _mod_all(c_all, mod_w, mod_b):
    n = c_all.shape[0]
    return pl.pallas_call(
        _mod_kernel,
        grid=(DEPTH, 3),
        in_specs=[pl.BlockSpec((n, D_MODEL), lambda i, j: (0, 0)),
                  pl.BlockSpec((None, D_MODEL, D_MODEL), lambda i, j: (i, 0, j)),
                  pl.BlockSpec((None, 1, D_MODEL), lambda i, j: (i, 0, j))],
        out_specs=pl.BlockSpec((None, n, D_MODEL), lambda i, j: (i, 0, j)),
        out_shape=jax.ShapeDtypeStruct((DEPTH, n, 3 * D_MODEL), F32),
        compiler_params=_params(2),
        name="mod_all",
    )(c_all, mod_w, mod_b.reshape(DEPTH, 1, 3 * D_MODEL))


def _ab_weights(j, ab_in_w, ab_out_w, conv_a_w, conv_a_b, lru_gx_w, lru_gx_b, lru_ga_w, lru_ga_b,
                lru_a_param, conv_b_w, gdn_a_log, gdn_dt_bias, gdn_norm_w):
    n_main = 2 * W_A + QKV_B + W_B
    w_t = jnp.swapaxes(ab_in_w[j], 0, 1)
    w_in = w_t[0:n_main].astype(BF16)
    w_ba = jnp.zeros((2 * LANE, D_MODEL), F32)
    w_ba = w_ba.at[0:H_B].set(w_t[n_main:n_main + H_B])
    w_ba = w_ba.at[LANE:LANE + H_B].set(w_t[n_main + H_B:n_main + 2 * H_B]).astype(BF16)
    pad = lambda t: jnp.zeros((1, LANE), F32).at[0, 0:H_B].set(t)
    return dict(
        w_in=w_in, w_ba=w_ba, caw=conv_a_w[j], cab=conv_a_b[j].reshape(1, W_A),
        gw=jnp.concatenate([lru_gx_w[j], lru_ga_w[j]], axis=-1).astype(BF16),
        gxb=lru_gx_b[j].reshape(1, W_A), gab=lru_ga_b[j].reshape(1, W_A),
        apar=lru_a_param[j].reshape(1, W_A), cbw=conv_b_w[j], alog=pad(gdn_a_log[j]),
        dtb=pad(gdn_dt_bias[j]), gnw=gdn_norm_w[j].reshape(1, DV), w_out=ab_out_w[j].astype(BF16))


_AB_PRE_NAMES = ("w_in", "w_ba", "caw", "cab", "gw", "gxb", "gab", "apar", "cbw", "alog", "dtb")


def _ab_scratch(nb, rows):
    return [pltpu.VMEM((3 * nb + rows, W_A), F32), pltpu.VMEM((QKV_B // LANE, 3 * nb + rows, LANE), F32),
            pltpu.VMEM((rows, W_A), F32), pltpu.VMEM((H_B, rows, LANE), F32),
            pltpu.VMEM((rows, W_A), F32),
            pltpu.VMEM((rows, LANE), F32), pltpu.VMEM((rows, LANE), F32), pltpu.VMEM((nb, W_A), F32)]


def _ab_prompt(x, mod, normw, w, nb, nt):
    rows = nb * nt
    batch_major = x.ndim == 3
    n_rows = x.shape[0] * x.shape[1] if batch_major else x.shape[0]
    n_chunks = n_rows // rows
    pre = [w[k] for k in _AB_PRE_NAMES]
    if batch_major:
        x_spec = pl.BlockSpec((nb, nt, D_MODEL), lambda c: (0, c, 0))
        extra = [pltpu.VMEM((D_MODEL // LANE, rows, LANE), F32)]
    else:
        x_spec = pl.BlockSpec((rows, D_MODEL), lambda c: (c, 0))
        extra = []
    return pl.pallas_call(
        functools.partial(_ab_prompt_kernel, nb, nt, batch_major),
        grid=(n_chunks,),
        in_specs=[x_spec, _const(mod.shape), _const(normw.shape)]
        + [_const(t.shape) for t in pre] + [_const(w["gnw"].shape), _const(w["w_out"].shape)],
        out_specs=[pl.BlockSpec((rows, D_MODEL), lambda c: (c, 0)), _full((3 * nb, W_A)), _full((nb, W_A)),
                   _full((3 * nb, QKV_B)), _const((nb, H_B, DK, DV))],
        out_shape=[jax.ShapeDtypeStruct((n_rows, D_MODEL), F32), jax.ShapeDtypeStruct((3 * nb, W_A), F32),
                   jax.ShapeDtypeStruct((nb, W_A), F32), jax.ShapeDtypeStruct((3 * nb, QKV_B), F32),
                   jax.ShapeDtypeStruct((nb, H_B, DK, DV), F32)],
        scratch_shapes=_ab_scratch(nb, rows) + extra,
        compiler_params=_params(1),
        name="ab_prompt",
    )(x, mod, normw, *pre, w["gnw"], w["w_out"])


def _ab_sample(x, mod, normw, w, conv_a, lru_h, conv_b, delta_all, layer, delta_new_all):
    nb = x.shape[0]
    pre = [w[k] for k in _AB_PRE_NAMES]
    ins = [x, mod, normw, *pre, conv_a, lru_h, conv_b]
    outs = [(nb, QKV_B), (nb, LANE), (nb, LANE), (nb, W_B), (nb, W_A), (3 * nb, W_A), (nb, W_A),
            (3 * nb, QKV_B)]
    qkv, g, beta, zb, ya, ca_new, lru_new, cb_new = pl.pallas_call(
        functools.partial(_ab_sample_pre_kernel, nb),
        in_specs=[_full(t.shape) for t in ins],
        out_specs=[_full(s) for s in outs],
        out_shape=[jax.ShapeDtypeStruct(s, F32) for s in outs],
        scratch_shapes=_ab_scratch(nb, nb),
        compiler_params=pltpu.CompilerParams(vmem_limit_bytes=VMEM_LIMIT),
        name="ab_sample_pre",
    )(*ins)
    bb = 8
    row_block = lambda width: pl.BlockSpec((bb, 1, width), lambda i: (i, 0, 0))
    n_layers = delta_all.shape[0]
    state_block = pl.BlockSpec((None, bb, H_B, DK, DV), lambda i: (layer, i, 0, 0, 0))
    step_ins = [qkv.reshape(nb, 1, QKV_B), g.reshape(nb, 1, LANE), beta.reshape(nb, 1, LANE),
                zb.reshape(nb, 1, W_B), w["gnw"], delta_all]
    step_specs = [row_block(QKV_B), row_block(LANE), row_block(LANE), row_block(W_B), _full((1, DV)),
                  state_block]
    if layer == 0:
        out_state_block = pl.BlockSpec((n_layers, bb, H_B, DK, DV), lambda i: (0, i, 0, 0, 0))
        aliases = {}
    else:
        step_ins.append(delta_new_all)
        step_specs.append(pl.BlockSpec(memory_space=pl.ANY))
        out_state_block = state_block
        aliases = {len(step_ins) - 1: 1}
    o, delta_new_all = pl.pallas_call(
        functools.partial(_gdn_step_kernel, bb, layer, n_layers),
        grid=(nb // bb,),
        in_specs=step_specs,
        out_specs=[row_block(W_B), out_state_block],
        out_shape=[jax.ShapeDtypeStruct((nb, 1, W_B), F32), jax.ShapeDtypeStruct(delta_all.shape, F32)],
        input_output_aliases=aliases,
        compiler_params=_params(1),
        name="gdn_step",
    )(*step_ins)
    o = o.reshape(nb, W_B)
    y = pl.pallas_call(
        functools.partial(_ab_sample_post_kernel, nb),
        in_specs=[_full(x.shape), _full(mod.shape), _full(ya.shape), _full(o.shape), _full(w["w_out"].shape)],
        out_specs=_full(x.shape),
        out_shape=jax.ShapeDtypeStruct(x.shape, F32),
        compiler_params=pltpu.CompilerParams(vmem_limit_bytes=VMEM_LIMIT),
        name="ab_sample_post",
    )(x, mod, ya, o, w["w_out"])
    return y, ca_new, lru_new, cb_new, delta_new_all


def _s5_prep(s5_a_re, s5_a_im, s5_log_dt, s5_b_re, s5_b_im):
    n = s5_a_re.shape[0]
    gp = pl.BlockSpec((None, G_C, P_C), lambda i: (i, 0, 0))
    gcp = pl.BlockSpec((None, G_C, CG, P_C), lambda i: (i, 0, 0, 0))
    return pl.pallas_call(
        _s5_prep_kernel,
        grid=(n,),
        in_specs=[gp, gp, pl.BlockSpec((None, G_C, 1), lambda i: (i, 0, 0)), gcp, gcp],
        out_specs=[gp, gp, gcp, gcp],
        out_shape=[jax.ShapeDtypeStruct((n, G_C, P_C), F32)] * 2
        + [jax.ShapeDtypeStruct((n, G_C, CG, P_C), F32)] * 2,
        compiler_params=_params(1),
        name="s5_prep",
    )(s5_a_re, s5_a_im, s5_log_dt.reshape(n, G_C, 1), jnp.swapaxes(s5_b_re, 2, 3), jnp.swapaxes(s5_b_im, 2, 3))


def _block_diag_in(t):
    gl = G_C // S5_LANE_BLOCKS
    t = t.reshape(S5_LANE_BLOCKS, gl, CG, P_C)
    eye = jnp.eye(gl, dtype=t.dtype)
    return jnp.einsum("jgcp,gh->jgchp", t, eye).reshape(S5_LANE_BLOCKS, gl * CG, gl * P_C)


def _block_diag_out(t):
    gl = G_C // S5_LANE_BLOCKS
    t = t.reshape(S5_LANE_BLOCKS, gl, CG, P_C)
    eye = jnp.eye(gl, dtype=t.dtype)
    return jnp.einsum("jgcp,gh->jgphc", t, eye).reshape(S5_LANE_BLOCKS, gl * P_C, gl * CG)


def _s5_layer(x, mod, normw, w, nb, nt, state, final_norm, batch_major_out=False):
    rows = nb * nt
    n_chunks = x.shape[0] // rows
    consts = [mod, normw, w["w_in"], w["bre"], w["bim"], w["cre"], w["cim"], w["abr"], w["abi"], w["d"],
              w["gluw"], w["glub"], w["w_out"], w["fnw"]]
    ins = [x, *consts]
    in_specs = [pl.BlockSpec((rows, D_MODEL), lambda c: (c, 0))] + [_const(t.shape) for t in consts]
    if state is not None:
        ins += list(state)
        in_specs += [_const((nb, NS))] * 2
    if batch_major_out:
        y_spec = pl.BlockSpec((nb, nt, D_MODEL), lambda c: (0, c, 0))
        y_shape = (nb, x.shape[0] // nb, D_MODEL)
        extra = [pltpu.VMEM((D_MODEL // LANE, rows, LANE), F32)]
    else:
        y_spec = pl.BlockSpec((rows, D_MODEL), lambda c: (c, 0))
        y_shape = x.shape
        extra = []
    return pl.pallas_call(
        functools.partial(_s5_kernel, nb, nt, state is not None, final_norm, batch_major_out),
        grid=(n_chunks,),
        in_specs=in_specs,
        out_specs=[y_spec, _full((nb, NS)), _full((nb, NS))],
        out_shape=[jax.ShapeDtypeStruct(y_shape, F32), jax.ShapeDtypeStruct((nb, NS), F32),
                   jax.ShapeDtypeStruct((nb, NS), F32)],
        scratch_shapes=[pltpu.VMEM((nb + rows, NS), F32), pltpu.VMEM((nb + rows, NS), F32),
                        pltpu.VMEM((rows, D_MODEL), F32)] + extra,
        compiler_params=_params(1),
        name="s5_layer",
    )(*ins)


def _to_time_major(t):
    b, l, c = t.shape
    return jnp.swapaxes(t, 0, 1).reshape(l * b, c)


def _from_time_major(t, b):
    lb, c = t.shape
    return jnp.swapaxes(t.reshape(lb // b, b, c), 0, 1)


def kernel(x_prompt, x_sample, c_prompt, c_sample, state_conv_a, state_lru, state_conv_b, state_delta, state_s5_re, state_s5_im, norm_w, mod_w, mod_b, ab_in_w, ab_out_w, conv_a_w, conv_a_b, lru_gx_w, lru_gx_b, lru_ga_w, lru_ga_b, lru_a_param, conv_b_w, gdn_a_log, gdn_dt_bias, gdn_norm_w, c_in_w, c_out_w, s5_a_re, s5_a_im, s5_b_re, s5_b_im, s5_c_re, s5_c_im, s5_d, s5_log_dt, glu_w, glu_b, final_norm_w):
    nbp = x_prompt.shape[0]
    nbs = x_sample.shape[0]
    nt = GDN_BLOCK

    mods = _mod_all(jnp.concatenate([c_prompt, c_sample], axis=0), mod_w, mod_b)
    abr, abi, bbr, bbi = _s5_prep(s5_a_re, s5_a_im, s5_log_dt, s5_b_re, s5_b_im)
    fnw = final_norm_w.reshape(1, D_MODEL)

    xp = x_prompt
    xs = x_sample.reshape(nbs, D_MODEL)
    p_states = [[] for _ in range(6)]
    s_states = [[] for _ in range(6)]
    s_delta = None
    for i in range(DEPTH):
        j = i // 2
        normw = norm_w[i].reshape(1, D_MODEL)
        mod_p = mods[i, 0:nbp]
        mod_s = mods[i, nbp:nbp + nbs]
        if i % 2 == 0:
            w = _ab_weights(j, ab_in_w, ab_out_w, conv_a_w, conv_a_b, lru_gx_w, lru_gx_b, lru_ga_w, lru_ga_b,
                            lru_a_param, conv_b_w, gdn_a_log, gdn_dt_bias, gdn_norm_w)
            xp, ca, lh, cb, ds = _ab_prompt(xp, mod_p, normw, w, nbp, nt)
            for lst, val in zip(p_states[:4], (_from_time_major(ca, nbp), lh, _from_time_major(cb, nbp), ds)):
                lst.append(val)
            xs, ca, lh, cb, s_delta = _ab_sample(xs, mod_s, normw, w, _to_time_major(state_conv_a[j]),
                                                 state_lru[j], _to_time_major(state_conv_b[j]), state_delta, j,
                                                 s_delta)
            for lst, val in zip(s_states[:3], (_from_time_major(ca, nbs), lh, _from_time_major(cb, nbs))):
                lst.append(val)
        else:
            w = dict(
                w_in=c_in_w[j].astype(BF16), bre=_block_diag_in(bbr[j]).astype(BF16),
                bim=_block_diag_in(bbi[j]).astype(BF16), cre=_block_diag_out(s5_c_re[j]).astype(BF16),
                cim=_block_diag_out(s5_c_im[j]).astype(BF16), abr=abr[j].reshape(1, NS),
                abi=abi[j].reshape(1, NS), d=s5_d[j].reshape(1, D_MODEL), gluw=glu_w[j].astype(BF16),
                glub=glu_b[j].reshape(1, D_MODEL), w_out=c_out_w[j].astype(BF16), fnw=fnw)
            last = i == DEPTH - 1
            xp, sr, si = _s5_layer(xp, mod_p, normw, w, nbp, nt, None, last, batch_major_out=last)
            p_states[4].append(sr.reshape(nbp, G_C, P_C))
            p_states[5].append(si.reshape(nbp, G_C, P_C))
            xs, sr, si = _s5_layer(xs, mod_s, normw, w, nbs, 1,
                                   (state_s5_re[j].reshape(nbs, NS), state_s5_im[j].reshape(nbs, NS)), last)
            s_states[4].append(sr.reshape(nbs, G_C, P_C))
            s_states[5].append(si.reshape(nbs, G_C, P_C))
    y_prompt = xp
    y_sample = xs.reshape(nbs, 1, D_MODEL)
    stack = lambda lists: tuple(jnp.stack(l) for l in lists)
    s_out = stack(s_states[:3]) + (s_delta,) + stack(s_states[4:])
    return (y_prompt, y_sample) + stack(p_states) + s_out
```

```python
import functools

import jax
import jax.numpy as jnp
from jax import lax
from jax.experimental import pallas as pl
from jax.experimental.pallas import tpu as pltpu

F32 = jnp.float32
BF16 = jnp.bfloat16

D_MODEL = 1024
DEPTH = 4
CONV_W = 4
W_A = 1024
H_A = 8
BW_A = 128
LRU_C = 8.0
H_B = 8
DK = 128
DV = 128
W_B = H_B * DV
QKV_B = 3 * W_B
CG = 16
G_C = 64
P_C = 64
NS = G_C * P_C
EPS = 1e-6
LANE = 128
GDN_BLOCK = 64
GDN_SEQS = 4
TILE_COST = 2
LRU_TASK_COST = 3
SLICE_TASK_COST = 1
S5_LANE_BLOCKS = D_MODEL // LANE
S5_BLOCK_STATE = NS // S5_LANE_BLOCKS
VMEM_LIMIT = 60 * 1024 * 1024


def _dot(a, b):
    return jnp.dot(a.astype(BF16), b.astype(BF16), preferred_element_type=F32)


def _dot_nt(a, b):
    return lax.dot_general(a.astype(BF16), b.astype(BF16), (((1,), (1,)), ((), ())),
                           preferred_element_type=F32)


def _silu(x):
    return x * jax.nn.sigmoid(x)


def _softplus(x):
    return jnp.maximum(x, 0.0) + jnp.log1p(jnp.exp(-jnp.abs(x)))


def _norm_mod(x, normw, mod_ref, nb):
    rows = x.shape[0]
    ms = jnp.mean(x * x, axis=-1, keepdims=True)
    y = x * lax.rsqrt(ms + EPS) * normw
    shift = mod_ref[:, 0:D_MODEL]
    scale = mod_ref[:, D_MODEL:2 * D_MODEL]
    y3 = y.reshape(rows // nb, nb, D_MODEL)
    return (y3 * (1.0 + scale)[None] + shift[None]).reshape(rows, D_MODEL)


def _residual(x, out, mod_ref, nb):
    rows = x.shape[0]
    gate = mod_ref[:, 2 * D_MODEL:3 * D_MODEL]
    return x + (out.reshape(rows // nb, nb, D_MODEL) * gate[None]).reshape(rows, D_MODEL)


def _conv_inplace(ext, lane0, w_ref, w_lane0, nb, rows, post):
    sl = slice(lane0, lane0 + LANE)
    wl = slice(w_lane0, w_lane0 + LANE)
    acc = ext[0:rows, sl] * w_ref[0:1, wl]
    for j in range(1, CONV_W):
        acc = acc + ext[j * nb:j * nb + rows, sl] * w_ref[j:j + 1, wl]
    tail = ext[rows:rows + 3 * nb, sl]
    ext[3 * nb:3 * nb + rows, sl] = post(acc)
    ext[0:3 * nb, sl] = tail


def _l2norm(t):
    return t * lax.rsqrt(jnp.sum(t * t, axis=-1, keepdims=True) + EPS)


def _ab_pre(first, nb, nt, reset_first, x, mod_ref, normw_ref, w_in_ref, w_ba_ref, caw_ref, cab_ref,
            gw_ref, gxb_ref, gab_ref, apar_ref, cbw_ref, alog_ref, dtb_ref,
            ext_a, ext_b, za, zbs, lb, gsc, bsc, h_s):
    rows = nb * nt
    hb = _norm_mod(x, normw_ref[...], mod_ref, nb).astype(BF16)
    tile = 2 * LANE

    def emit_tile(c0):
        part = _dot_nt(hb, w_in_ref[c0:c0 + tile, :])
        for m in range(tile // LANE):
            c = c0 + m * LANE
            piece = part[:, m * LANE:(m + 1) * LANE]
            if c < W_A:
                ext_a[3 * nb:3 * nb + rows, c:c + LANE] = piece
            elif c < 2 * W_A:
                za[:, c - W_A:c - W_A + LANE] = piece
            elif c < 2 * W_A + QKV_B:
                ext_b[(c - 2 * W_A) // LANE, 3 * nb:3 * nb + rows, :] = piece
            else:
                zbs[(c - 2 * W_A - QKV_B) // LANE] = piece

    def lru_block(blk):
        sl = slice(blk * BW_A, (blk + 1) * BW_A)
        _conv_inplace(ext_a, blk * BW_A, caw_ref, blk * BW_A, nb, rows, lambda acc: acc + cab_ref[:, sl])
        xb = ext_a[3 * nb:3 * nb + rows, sl]
        xbb = xb.astype(BF16)
        gates = _dot(xbb, gw_ref[blk])
        gate_x = jax.nn.sigmoid(gates[:, 0:BW_A] + gxb_ref[:, sl])
        gate_a = jax.nn.sigmoid(gates[:, BW_A:2 * BW_A] + gab_ref[:, sl])
        log_a = -LRU_C * gate_a * _softplus(-apar_ref[:, sl])
        a = jnp.exp(log_a)
        m2 = 1.0 - a * a
        mult = jnp.where(m2 > 0.0, m2 * lax.rsqrt(m2), 0.0)
        if reset_first:
            row = lax.broadcasted_iota(jnp.int32, (rows, BW_A), 0)
            mult = jnp.where(jnp.logical_and(first, row < nb), 1.0, mult)
        bval = mult * gate_x * xb
        h = h_s[:, sl]
        for t in range(nt):
            r = slice(t * nb, (t + 1) * nb)
            h = a[r] * h + bval[r]
            lb[r, sl] = h
        h_s[:, sl] = h

    def qkv_slice(n):
        if n < H_B:
            post = lambda acc: _l2norm(_silu(acc)) * (DK ** -0.5)
        elif n < 2 * H_B:
            post = lambda acc: _l2norm(_silu(acc))
        else:
            post = _silu
        _conv_inplace(ext_b.at[n], 0, cbw_ref, n * LANE, nb, rows, post)

    col_xa, col_za, col_qkv, col_zb = 0, W_A, 2 * W_A, 2 * W_A + QKV_B
    tiles = ([col_xa + i * tile for i in range(W_A // tile)] + [col_qkv + i * tile for i in range(QKV_B // tile)]
             + [col_za + i * tile for i in range(W_A // tile)] + [col_zb + i * tile for i in range(W_B // tile)])
    n_xa = W_A // tile
    tasks = []
    for blk in range(H_A):
        tasks.append((n_xa - 1, LRU_TASK_COST, functools.partial(lru_block, blk)))
        for n in range(3 * blk, 3 * blk + 3):
            tasks.append((n_xa + n // 2, SLICE_TASK_COST, functools.partial(qkv_slice, n)))
    matmul_cost = vector_cost = 0
    for ti, c0 in enumerate(tiles):
        emit_tile(c0)
        matmul_cost += TILE_COST
        while tasks and tasks[0][0] <= ti and vector_cost + tasks[0][1] <= matmul_cost:
            _, cost, task = tasks.pop(0)
            task()
            vector_cost += cost
    ba = _dot_nt(hb, w_ba_ref[...])
    bsc[...] = jax.nn.sigmoid(ba[:, 0:LANE])
    g = -jnp.exp(alog_ref[...]) * _softplus(ba[:, LANE:2 * LANE] + dtb_ref[...])
    acc = g[0:nb]
    gsc[0:nb, :] = acc
    for t in range(1, nt):
        acc = acc + g[t * nb:(t + 1) * nb]
        gsc[t * nb:(t + 1) * nb, :] = acc
    for _, _, task in tasks:
        task()


def _gdn_block_masks():
    n = 2 * GDN_BLOCK
    ri = lax.broadcasted_iota(jnp.int32, (n, n), 0)
    ci = lax.broadcasted_iota(jnp.int32, (n, n), 1)
    same = (ri >= GDN_BLOCK) == (ci >= GDN_BLOCK)
    tri = jnp.where(jnp.logical_and(same, ri >= ci), 1.0, 0.0).astype(F32)
    strict = jnp.where(jnp.logical_and(same, ri > ci), 1.0, 0.0).astype(F32)
    eye = jnp.where(ri == ci, 1.0, 0.0).astype(F32)
    levels = []
    for l in range(6):
        rb = ri >> l
        sub = jnp.logical_and((rb & 1) == 1, (ci >> l) == rb - 1)
        levels.append(jnp.where(jnp.logical_and(same, sub), 1.0, 0.0).astype(F32))
    return tri, strict, eye, levels


def _gdn_block(seqs, nb, ext_b, zbs, gsc, bsc, s_ref, gnw_ref, masks):
    c_len = GDN_BLOCK
    top = lax.broadcasted_iota(jnp.int32, (2 * c_len, LANE), 0) < c_len
    tri, strict, eye, levels = masks
    chains = [(si, p) for si in range(len(seqs)) for p in range(H_B // 2)]
    n = range(len(chains))

    def cat(a0, a1):
        return jnp.concatenate([a0, a1], axis=0)

    rows = [pl.ds(3 * nb + b, c_len, stride=nb) for b in seqs]
    zrows = [pl.ds(b, c_len, stride=nb) for b in seqs]

    def head_pair(base, si, p):
        return cat(ext_b[base + 2 * p, rows[si], :], ext_b[base + 2 * p + 1, rows[si], :])

    def col_pair(t, p):
        return cat(t[:, 2 * p:2 * p + 1], t[:, 2 * p + 1:2 * p + 2])

    gcb = [gsc[zr, :] for zr in zrows]
    betab = [bsc[zr, :] for zr in zrows]
    glast = [gsc[pl.ds((c_len - 1) * nb + b, 1), :] for b in seqs]
    q = [head_pair(0, si, p) for si, p in chains]
    k = [head_pair(H_B, si, p) for si, p in chains]
    v = [head_pair(2 * H_B, si, p) for si, p in chains]
    c = [col_pair(gcb[si], p) for si, p in chains]
    bcol = [col_pair(betab[si], p) for si, p in chains]
    gl = [col_pair(jnp.broadcast_to(glast[si], (c_len, LANE)), p) for si, p in chains]
    gct = [cat(gcb[s], gcb[s + 1]).T for s in range(0, len(seqs), 2)]
    decay = []
    for i, (si, p) in enumerate(chains):
        half = slice((si % 2) * c_len, (si % 2 + 1) * c_len)
        crow = jnp.concatenate([gct[si // 2][2 * p:2 * p + 1, half], gct[si // 2][2 * p + 1:2 * p + 2, half]],
                               axis=1)
        cm = jnp.broadcast_to(c[i], (2 * c_len, 2 * c_len))
        decay.append(jnp.exp((cm - jnp.broadcast_to(crow, (2 * c_len, 2 * c_len))) * tri) * tri)
    kb = [k[i] * bcol[i] for i in n]
    a_mat = [_dot_nt(kb[i], k[i]) * decay[i] * strict for i in n]
    qk = [_dot_nt(q[i], k[i]) * decay[i] for i in n]
    x = [eye - a_mat[i] * levels[0] for i in n]
    for l in range(1, 6):
        t = [_dot(a_mat[i] * levels[l], x[i]) for i in n]
        x = [x[i] - _dot(x[i], t[i]) for i in n]
    sol = [_dot(x[i], jnp.concatenate([v[i] * bcol[i], kb[i] * jnp.exp(c[i])], axis=1)) for i in n]
    s0 = [s_ref[seqs[si], 2 * p] for si, p in chains]
    s1 = [s_ref[seqs[si], 2 * p + 1] for si, p in chains]

    def per_head(lhs, i):
        return cat(_dot(lhs[0:c_len], s0[i]), _dot(lhs[c_len:2 * c_len], s1[i]))

    ws = [per_head(sol[i][:, DV:2 * DV], i) for i in n]
    qs = [per_head(q[i] * jnp.exp(c[i]), i) for i in n]
    v_new = [sol[i][:, 0:DV] - ws[i] for i in n]
    o = [qs[i] + _dot(qk[i], v_new[i]) for i in n]
    upd = []
    for i in n:
        kdec = k[i] * jnp.exp(gl[i] - c[i])
        vblk = jnp.concatenate([jnp.where(top, v_new[i], 0.0), jnp.where(top, 0.0, v_new[i])], axis=1)
        upd.append(_dot(kdec.T, vblk))
    for i, (si, p) in enumerate(chains):
        b = seqs[si]
        h0, h1 = 2 * p, 2 * p + 1
        s_ref[b, h0] = s0[i] * jnp.exp(glast[si][:, h0:h0 + 1]) + upd[i][:, 0:DV]
        s_ref[b, h1] = s1[i] * jnp.exp(glast[si][:, h1:h1 + 1]) + upd[i][:, DV:2 * DV]
        on = o[i] * lax.rsqrt(jnp.mean(o[i] * o[i], axis=-1, keepdims=True) + EPS) * gnw_ref[...]
        zb = cat(zbs[h0, zrows[si], :], zbs[h1, zrows[si], :])
        og = on * _silu(zb)
        ext_b[h0, rows[si], :] = og[0:c_len]
        ext_b[h1, rows[si], :] = og[c_len:2 * c_len]


def _ab_post(x, nb, mod_ref, ya, o, w_out_ref):
    out = _dot(ya.astype(BF16), w_out_ref[0:W_A, :]) + _dot(o.astype(BF16), w_out_ref[W_A:W_A + W_B, :])
    return _residual(x, out, mod_ref, nb)


def _to_time_major_rows(x_ref, x_tm, nb, nt):
    for b in range(nb):
        for m in range(D_MODEL // LANE):
            x_tm[m, pl.ds(b, nt, stride=nb), :] = x_ref[b, :, m * LANE:(m + 1) * LANE]
    return jnp.concatenate([x_tm[m] for m in range(D_MODEL // LANE)], axis=1)


def _from_time_major_rows(y, y_ref, y_tm, nb, nt):
    for m in range(D_MODEL // LANE):
        y_tm[m] = y[:, m * LANE:(m + 1) * LANE]
    for b in range(nb):
        for m in range(D_MODEL // LANE):
            y_ref[b, :, m * LANE:(m + 1) * LANE] = y_tm[m, pl.ds(b, nt, stride=nb), :]


def _ab_prompt_kernel(nb, nt, batch_major_in, x_ref, mod_ref, normw_ref, w_in_ref, w_ba_ref, caw_ref, cab_ref,
                      gw_ref, gxb_ref, gab_ref, apar_ref, cbw_ref, alog_ref, dtb_ref, gnw_ref,
                      w_out_ref, y_ref, ca_out, lru_out, cb_out, s_ref,
                      ext_a, ext_b, za, zbs, lb, gsc, bsc, h_s, *x_tm):
    c = pl.program_id(0)
    rows = nb * nt

    @pl.when(c == 0)
    def _():
        ext_a[0:3 * nb, :] = jnp.zeros((3 * nb, W_A), F32)
        ext_b[:, 0:3 * nb, :] = jnp.zeros((QKV_B // LANE, 3 * nb, LANE), F32)
        h_s[...] = jnp.zeros(h_s.shape, F32)
        s_ref[...] = jnp.zeros(s_ref.shape, F32)

    x = _to_time_major_rows(x_ref, x_tm[0], nb, nt) if batch_major_in else x_ref[...]
    _ab_pre(c == 0, nb, nt, True, x, mod_ref, normw_ref, w_in_ref, w_ba_ref, caw_ref, cab_ref, gw_ref,
            gxb_ref, gab_ref, apar_ref, cbw_ref, alog_ref, dtb_ref,
            ext_a, ext_b, za, zbs, lb, gsc, bsc, h_s)

    masks = _gdn_block_masks()

    def per_seq_group(i, carry):
        _gdn_block([GDN_SEQS * i + s for s in range(GDN_SEQS)], nb, ext_b, zbs, gsc, bsc, s_ref, gnw_ref, masks)
        return carry

    lax.fori_loop(0, nb // GDN_SEQS, per_seq_group, 0)

    ya = lb[...] * _silu(za[...])
    o = jnp.concatenate([ext_b[h, 3 * nb:3 * nb + rows, :] for h in range(H_B)], axis=1)
    if batch_major_in:
        x = jnp.concatenate([x_tm[0][m] for m in range(D_MODEL // LANE)], axis=1)
    else:
        x = x_ref[...]
    y_ref[...] = _ab_post(x, nb, mod_ref, ya, o, w_out_ref)

    @pl.when(c == pl.num_programs(0) - 1)
    def _():
        ca_out[...] = ext_a[0:3 * nb, :]
        for n in range(QKV_B // LANE):
            cb_out[:, n * LANE:(n + 1) * LANE] = ext_b[n, 0:3 * nb, :]
        lru_out[...] = h_s[...]


def _ab_sample_pre_kernel(nb, x_ref, mod_ref, normw_ref, w_in_ref, w_ba_ref, caw_ref, cab_ref, gw_ref,
                          gxb_ref, gab_ref, apar_ref, cbw_ref, alog_ref, dtb_ref,
                          ca_in, lru_in, cb_in,
                          qkv_out, g_out, beta_out, zb_out, ya_out, ca_out, lru_out, cb_out,
                          ext_a, ext_b, za, zbs, lb, gsc, bsc, h_s):
    ext_a[0:3 * nb, :] = ca_in[...]
    for n in range(QKV_B // LANE):
        ext_b[n, 0:3 * nb, :] = cb_in[:, n * LANE:(n + 1) * LANE]
    h_s[...] = lru_in[...]
    _ab_pre(False, nb, 1, False, x_ref[...], mod_ref, normw_ref, w_in_ref, w_ba_ref, caw_ref, cab_ref,
            gw_ref, gxb_ref, gab_ref, apar_ref, cbw_ref, alog_ref, dtb_ref,
            ext_a, ext_b, za, zbs, lb, gsc, bsc, h_s)
    for n in range(QKV_B // LANE):
        qkv_out[:, n * LANE:(n + 1) * LANE] = ext_b[n, 3 * nb:4 * nb, :]
        cb_out[:, n * LANE:(n + 1) * LANE] = ext_b[n, 0:3 * nb, :]
    g_out[...] = gsc[...]
    beta_out[...] = bsc[...]
    for h in range(H_B):
        zb_out[:, h * LANE:(h + 1) * LANE] = zbs[h]
    ya_out[...] = lb[...] * _silu(za[...])
    ca_out[...] = ext_a[0:3 * nb, :]
    lru_out[...] = h_s[...]


def _gdn_step_kernel(bb, layer, n_layers, qkv_ref, g_ref, beta_ref, zb_ref, gnw_ref, s_in, *rest):
    if layer == 0:
        o_ref, s_all = rest
        for l in range(1, n_layers):
            s_all[l] = jnp.zeros(s_all.shape[1:], F32)
        s_out = s_all.at[0]
    else:
        _, o_ref, s_out = rest

    def per_seq(i, carry):
        g_row = g_ref[i]
        beta_row = beta_ref[i]
        heads = range(H_B)
        q = [qkv_ref[i, :, h * LANE:(h + 1) * LANE] for h in heads]
        k = [qkv_ref[i, :, H_B * DK + h * LANE:H_B * DK + (h + 1) * LANE] for h in heads]
        v = [qkv_ref[i, :, 2 * H_B * DK + h * LANE:2 * H_B * DK + (h + 1) * LANE] for h in heads]
        eg = [jnp.exp(g_row[:, h:h + 1]) for h in heads]
        kcol = [jnp.broadcast_to(k[h], (DK, DK)).T for h in heads]
        qcol = [jnp.broadcast_to(q[h], (DK, DK)).T for h in heads]
        s = [s_in[i, h] for h in heads]
        ks = [jnp.sum(kcol[h] * s[h], axis=0, keepdims=True) for h in heads]
        v_new = [beta_row[:, h:h + 1] * (v[h] - eg[h] * ks[h]) for h in heads]
        s_new = [eg[h] * s[h] + kcol[h] * v_new[h] for h in heads]
        for h in heads:
            s_out[i, h] = s_new[h]
        o = [jnp.sum(qcol[h] * s_new[h], axis=0, keepdims=True) for h in heads]
        for h in heads:
            on = o[h] * lax.rsqrt(jnp.mean(o[h] * o[h], axis=-1, keepdims=True) + EPS) * gnw_ref[...]
            zb = zb_ref[i, :, h * LANE:(h + 1) * LANE]
            o_ref[i, :, h * LANE:(h + 1) * LANE] = on * _silu(zb)
        return carry

    lax.fori_loop(0, bb, per_seq, 0)


def _ab_sample_post_kernel(nb, x_ref, mod_ref, ya_ref, o_ref, w_out_ref, y_ref):
    y_ref[...] = _ab_post(x_ref[...], nb, mod_ref, ya_ref[...], o_ref[...], w_out_ref)


def _s5_kernel(nb, nt, has_state, final_norm, batch_major_out, *refs):
    (x_ref, mod_ref, normw_ref, w_in_ref, bre_ref, bim_ref, cre_ref, cim_ref, abr_ref, abi_ref, d_ref,
     gluw_ref, glub_ref, w_out_ref, fnw_ref) = refs[:15]
    refs = refs[15:]
    if has_state:
        sre_in, sim_in = refs[:2]
        refs = refs[2:]
    y_ref, sre_out, sim_out, xs_re, xs_im, y_s = refs[:6]
    c = pl.program_id(0)
    rows = nb * nt

    @pl.when(c == 0)
    def _():
        if has_state:
            xs_re[0:nb, :] = sre_in[...]
            xs_im[0:nb, :] = sim_in[...]
        else:
            xs_re[0:nb, :] = jnp.zeros((nb, NS), F32)
            xs_im[0:nb, :] = jnp.zeros((nb, NS), F32)

    hb = _norm_mod(x_ref[...], normw_ref[...], mod_ref, nb).astype(BF16)
    uz = _dot(hb, w_in_ref[...])
    u = uz[:, 0:D_MODEL]
    z = uz[:, D_MODEL:2 * D_MODEL]
    ub = u.astype(BF16)
    def b_proj(j):
        sl = slice(j * S5_BLOCK_STATE, (j + 1) * S5_BLOCK_STATE)
        uj = ub[:, j * LANE:(j + 1) * LANE]
        xs_re[nb:nb + rows, sl] = _dot(uj, bre_ref[j])
        xs_im[nb:nb + rows, sl] = _dot(uj, bim_ref[j])

    def recurrence(j):
        sl = slice(j * S5_BLOCK_STATE, (j + 1) * S5_BLOCK_STATE)
        ar = jnp.broadcast_to(abr_ref[:, sl], (nb, S5_BLOCK_STATE))
        ai = jnp.broadcast_to(abi_ref[:, sl], (nb, S5_BLOCK_STATE))
        sr, si = xs_re[0:nb, sl], xs_im[0:nb, sl]
        for t in range(nt):
            r = slice(nb + t * nb, 2 * nb + t * nb)
            sr, si = ar * sr - ai * si + xs_re[r, sl], ar * si + ai * sr + xs_im[r, sl]
            xs_re[r, sl] = sr
            xs_im[r, sl] = si
        xs_re[0:nb, sl] = sr
        xs_im[0:nb, sl] = si

    def c_proj(j):
        sl = slice(j * S5_BLOCK_STATE, (j + 1) * S5_BLOCK_STATE)
        cl = slice(j * LANE, (j + 1) * LANE)
        yj = (_dot(xs_re[nb:nb + rows, sl].astype(BF16), cre_ref[j])
              - _dot(xs_im[nb:nb + rows, sl].astype(BF16), cim_ref[j]))
        yj = yj + d_ref[:, cl] * u[:, cl]
        cdf = 0.5 * (1.0 + jnp.tanh(0.7978845608028654 * (yj + 0.044715 * (yj * yj * yj))))
        y_s[:, cl] = yj * cdf

    for j in range(S5_LANE_BLOCKS):
        b_proj(j)
    for j in range(S5_LANE_BLOCKS):
        recurrence(j)
        c_proj(j)
    n_parts = 2 if nt % 2 == 0 else 1
    parts = []
    for part in range(n_parts):
        rs = slice(part * (rows // n_parts), (part + 1) * (rows // n_parts))
        y = y_s[rs, :]
        y = y * jax.nn.sigmoid(_dot(y.astype(BF16), gluw_ref[...]) + glub_ref[...])
        y = y * _silu(z[rs])
        xn = _residual(x_ref[rs, :], _dot(y.astype(BF16), w_out_ref[...]), mod_ref, nb)
        if final_norm:
            xn = xn * lax.rsqrt(jnp.mean(xn * xn, axis=-1, keepdims=True) + EPS) * fnw_ref[...]
        parts.append(xn)
    xn = jnp.concatenate(parts, axis=0)
    if batch_major_out:
        _from_time_major_rows(xn, y_ref, refs[6], nb, nt)
    else:
        y_ref[...] = xn

    @pl.when(c == pl.num_programs(0) - 1)
    def _():
        sre_out[...] = xs_re[0:nb, :]
        sim_out[...] = xs_im[0:nb, :]


def _s5_prep_kernel(are_ref, aim_ref, ldt_ref, bre_ref, bim_ref, abr_out, abi_out, bbr_out, bbi_out):
    a_re = are_ref[...]
    a_im = aim_ref[...]
    dt = jnp.exp(ldt_ref[...])
    mag = jnp.exp(a_re * dt)
    abr = mag * jnp.cos(a_im * dt)
    abi = mag * jnp.sin(a_im * dt)
    abr_out[...] = abr
    abi_out[...] = abi
    den = a_re * a_re + a_im * a_im
    nr = abr - 1.0
    cr = (nr * a_re + abi * a_im) / den
    ci = (abi * a_re - nr * a_im) / den
    b_re = bre_ref[...]
    b_im = bim_ref[...]
    bbr_out[...] = cr[:, None, :] * b_re - ci[:, None, :] * b_im
    bbi_out[...] = cr[:, None, :] * b_im + ci[:, None, :] * b_re


def _mod_kernel(c_ref, w_ref, b_ref, o_ref):
    c = c_ref[...]
    o_ref[...] = _dot(_silu(c).astype(BF16), w_ref[...].astype(BF16)) + b_ref[...]


def _full(shape):
    n = len(shape)
    return pl.BlockSpec(shape, lambda *_: (0,) * n)


def _const(shape):
    n = len(shape)
    return pl.BlockSpec(shape, lambda *_: (0,) * n, pipeline_mode=pl.Buffered(1))


def _params(n_grid):
    return pltpu.CompilerParams(dimension_semantics=("arbitrary",) * n_grid, vmem_limit_bytes=VMEM_LIMIT)


def _mod_all(c_all, mod_w, mod_b):
    n = c_all.shape[0]
    return pl.pallas_call(
        _mod_kernel,
        grid=(DEPTH, 3),
        in_specs=[pl.BlockSpec((n, D_MODEL), lambda i, j: (0, 0)),
                  pl.BlockSpec((None, D_MODEL, D_MODEL), lambda i, j: (i, 0, j)),
                  pl.BlockSpec((None, 1, D_MODEL), lambda i, j: (i, 0, j))],
        out_specs=pl.BlockSpec((None, n, D_MODEL), lambda i, j: (i, 0, j)),
        out_shape=jax.ShapeDtypeStruct((DEPTH, n, 3 * D_MODEL), F32),
        compiler_params=_params(2),
        name="mod_all",
    )(c_all, mod_w, mod_b.reshape(DEPTH, 1, 3 * D_MODEL))


def _ab_weights(j, ab_in_w, ab_out_w, conv_a_w, conv_a_b, lru_gx_w, lru_gx_b, lru_ga_w, lru_ga_b,
                lru_a_param, conv_b_w, gdn_a_log, gdn_dt_bias, gdn_norm_w):
    n_main = 2 * W_A + QKV_B + W_B
    w_t = jnp.swapaxes(ab_in_w[j], 0, 1)
    w_in = w_t[0:n_main].astype(BF16)
    w_ba = jnp.zeros((2 * LANE, D_MODEL), F32)
    w_ba = w_ba.at[0:H_B].set(w_t[n_main:n_main + H_B])
    w_ba = w_ba.at[LANE:LANE + H_B].set(w_t[n_main + H_B:n_main + 2 * H_B]).astype(BF16)
    pad = lambda t: jnp.zeros((1, LANE), F32).at[0, 0:H_B].set(t)
    return dict(
        w_in=w_in, w_ba=w_ba, caw=conv_a_w[j], cab=conv_a_b[j].reshape(1, W_A),
        gw=jnp.concatenate([lru_gx_w[j], lru_ga_w[j]], axis=-1).astype(BF16),
        gxb=lru_gx_b[j].reshape(1, W_A), gab=lru_ga_b[j].reshape(1, W_A),
        apar=lru_a_param[j].reshape(1, W_A), cbw=conv_b_w[j], alog=pad(gdn_a_log[j]),
        dtb=pad(gdn_dt_bias[j]), gnw=gdn_norm_w[j].reshape(1, DV), w_out=ab_out_w[j].astype(BF16))


_AB_PRE_NAMES = ("w_in", "w_ba", "caw", "cab", "gw", "gxb", "gab", "apar", "cbw", "alog", "dtb")


def _ab_scratch(nb, rows):
    return [pltpu.VMEM((3 * nb + rows, W_A), F32), pltpu.VMEM((QKV_B // LANE, 3 * nb + rows, LANE), F32),
            pltpu.VMEM((rows, W_A), F32), pltpu.VMEM((H_B, rows, LANE), F32),
            pltpu.VMEM((rows, W_A), F32),
            pltpu.VMEM((rows, LANE), F32), pltpu.VMEM((rows, LANE), F32), pltpu.VMEM((nb, W_A), F32)]


def _ab_prompt(x, mod, normw, w, nb, nt):
    rows = nb * nt
    batch_major = x.ndim == 3
    n_rows = x.shape[0] * x.shape[1] if batch_major else x.shape[0]
    n_chunks = n_rows // rows
    pre = [w[k] for k in _AB_PRE_NAMES]
    if batch_major:
        x_spec = pl.BlockSpec((nb, nt, D_MODEL), lambda c: (0, c, 0))
        extra = [pltpu.VMEM((D_MODEL // LANE, rows, LANE), F32)]
    else:
        x_spec = pl.BlockSpec((rows, D_MODEL), lambda c: (c, 0))
        extra = []
    return pl.pallas_call(
        functools.partial(_ab_prompt_kernel, nb, nt, batch_major),
        grid=(n_chunks,),
        in_specs=[x_spec, _const(mod.shape), _const(normw.shape)]
        + [_const(t.shape) for t in pre] + [_const(w["gnw"].shape), _const(w["w_out"].shape)],
        out_specs=[pl.BlockSpec((rows, D_MODEL), lambda c: (c, 0)), _full((3 * nb, W_A)), _full((nb, W_A)),
                   _full((3 * nb, QKV_B)), _const((nb, H_B, DK, DV))],
        out_shape=[jax.ShapeDtypeStruct((n_rows, D_MODEL), F32), jax.ShapeDtypeStruct((3 * nb, W_A), F32),
                   jax.ShapeDtypeStruct((nb, W_A), F32), jax.ShapeDtypeStruct((3 * nb, QKV_B), F32),
                   jax.ShapeDtypeStruct((nb, H_B, DK, DV), F32)],
        scratch_shapes=_ab_scratch(nb, rows) + extra,
        compiler_params=_params(1),
        name="ab_prompt",
    )(x, mod, normw, *pre, w["gnw"], w["w_out"])


def _ab_sample(x, mod, normw, w, conv_a, lru_h, conv_b, delta_all, layer, delta_new_all):
    nb = x.shape[0]
    pre = [w[k] for k in _AB_PRE_NAMES]
    ins = [x, mod, normw, *pre, conv_a, lru_h, conv_b]
    outs = [(nb, QKV_B), (nb, LANE), (nb, LANE), (nb, W_B), (nb, W_A), (3 * nb, W_A), (nb, W_A),
            (3 * nb, QKV_B)]
    qkv, g, beta, zb, ya, ca_new, lru_new, cb_new = pl.pallas_call(
        functools.partial(_ab_sample_pre_kernel, nb),
        in_specs=[_full(t.shape) for t in ins],
        out_specs=[_full(s) for s in outs],
        out_shape=[jax.ShapeDtypeStruct(s, F32) for s in outs],
        scratch_shapes=_ab_scratch(nb, nb),
        compiler_params=pltpu.CompilerParams(vmem_limit_bytes=VMEM_LIMIT),
        name="ab_sample_pre",
    )(*ins)
    bb = 8
    row_block = lambda width: pl.BlockSpec((bb, 1, width), lambda i: (i, 0, 0))
    n_layers = delta_all.shape[0]
    state_block = pl.BlockSpec((None, bb, H_B, DK, DV), lambda i: (layer, i, 0, 0, 0))
    step_ins = [qkv.reshape(nb, 1, QKV_B), g.reshape(nb, 1, LANE), beta.reshape(nb, 1, LANE),
                zb.reshape(nb, 1, W_B), w["gnw"], delta_all]
    step_specs = [row_block(QKV_B), row_block(LANE), row_block(LANE), row_block(W_B), _full((1, DV)),
                  state_block]
    if layer == 0:
        out_state_block = pl.BlockSpec((n_layers, bb, H_B, DK, DV), lambda i: (0, i, 0, 0, 0))
        aliases = {}
    else:
        step_ins.append(delta_new_all)
        step_specs.append(pl.BlockSpec(memory_space=pl.ANY))
        out_state_block = state_block
        aliases = {len(step_ins) - 1: 1}
    o, delta_new_all = pl.pallas_call(
        functools.partial(_gdn_step_kernel, bb, layer, n_layers),
        grid=(nb // bb,),
        in_specs=step_specs,
        out_specs=[row_block(W_B), out_state_block],
        out_shape=[jax.ShapeDtypeStruct((nb, 1, W_B), F32), jax.ShapeDtypeStruct(delta_all.shape, F32)],
        input_output_aliases=aliases,
        compiler_params=_params(1),
        name="gdn_step",
    )(*step_ins)
    o = o.reshape(nb, W_B)
    y = pl.pallas_call(
        functools.partial(_ab_sample_post_kernel, nb),
        in_specs=[_full(x.shape), _full(mod.shape), _full(ya.shape), _full(o.shape), _full(w["w_out"].shape)],
        out_specs=_full(x.shape),
        out_shape=jax.ShapeDtypeStruct(x.shape, F32),
        compiler_params=pltpu.CompilerParams(vmem_limit_bytes=VMEM_LIMIT),
        name="ab_sample_post",
    )(x, mod, ya, o, w["w_out"])
    return y, ca_new, lru_new, cb_new, delta_new_all


def _s5_prep(s5_a_re, s5_a_im, s5_log_dt, s5_b_re, s5_b_im):
    n = s5_a_re.shape[0]
    gp = pl.BlockSpec((None, G_C, P_C), lambda i: (i, 0, 0))
    gcp = pl.BlockSpec((None, G_C, CG, P_C), lambda i: (i, 0, 0, 0))
    return pl.pallas_call(
        _s5_prep_kernel,
        grid=(n,),
        in_specs=[gp, gp, pl.BlockSpec((None, G_C, 1), lambda i: (i, 0, 0)), gcp, gcp],
        out_specs=[gp, gp, gcp, gcp],
        out_shape=[jax.ShapeDtypeStruct((n, G_C, P_C), F32)] * 2
        + [jax.ShapeDtypeStruct((n, G_C, CG, P_C), F32)] * 2,
        compiler_params=_params(1),
        name="s5_prep",
    )(s5_a_re, s5_a_im, s5_log_dt.reshape(n, G_C, 1), jnp.swapaxes(s5_b_re, 2, 3), jnp.swapaxes(s5_b_im, 2, 3))


def _block_diag_in(t):
    gl = G_C // S5_LANE_BLOCKS
    t = t.reshape(S5_LANE_BLOCKS, gl, CG, P_C)
    eye = jnp.eye(gl, dtype=t.dtype)
    return jnp.einsum("jgcp,gh->jgchp", t, eye).reshape(S5_LANE_BLOCKS, gl * CG, gl * P_C)


def _block_diag_out(t):
    gl = G_C // S5_LANE_BLOCKS
    t = t.reshape(S5_LANE_BLOCKS, gl, CG, P_C)
    eye = jnp.eye(gl, dtype=t.dtype)
    return jnp.einsum("jgcp,gh->jgphc", t, eye).reshape(S5_LANE_BLOCKS, gl * P_C, gl * CG)


def _s5_layer(x, mod, normw, w, nb, nt, state, final_norm, batch_major_out=False):
    rows = nb * nt
    n_chunks = x.shape[0] // rows
    consts = [mod, normw, w["w_in"], w["bre"], w["bim"], w["cre"], w["cim"], w["abr"], w["abi"], w["d"],
              w["gluw"], w["glub"], w["w_out"], w["fnw"]]
    ins = [x, *consts]
    in_specs = [pl.BlockSpec((rows, D_MODEL), lambda c: (c, 0))] + [_const(t.shape) for t in consts]
    if state is not None:
        ins += list(state)
        in_specs += [_const((nb, NS))] * 2
    if batch_major_out:
        y_spec = pl.BlockSpec((nb, nt, D_MODEL), lambda c: (0, c, 0))
        y_shape = (nb, x.shape[0] // nb, D_MODEL)
        extra = [pltpu.VMEM((D_MODEL // LANE, rows, LANE), F32)]
    else:
        y_spec = pl.BlockSpec((rows, D_MODEL), lambda c: (c, 0))
        y_shape = x.shape
        extra = []
    return pl.pallas_call(
        functools.partial(_s5_kernel, nb, nt, state is not None, final_norm, batch_major_out),
        grid=(n_chunks,),
        in_specs=in_specs,
        out_specs=[y_spec, _full((nb, NS)), _full((nb, NS))],
        out_shape=[jax.ShapeDtypeStruct(y_shape, F32), jax.ShapeDtypeStruct((nb, NS), F32),
                   jax.ShapeDtypeStruct((nb, NS), F32)],
        scratch_shapes=[pltpu.VMEM((nb + rows, NS), F32), pltpu.VMEM((nb + rows, NS), F32),
                        pltpu.VMEM((rows, D_MODEL), F32)] + extra,
        compiler_params=_params(1),
        name="s5_layer",
    )(*ins)


def _to_time_major(t):
    b, l, c = t.shape
    return jnp.swapaxes(t, 0, 1).reshape(l * b, c)


def _from_time_major(t, b):
    lb, c = t.shape
    return jnp.swapaxes(t.reshape(lb // b, b, c), 0, 1)


def kernel(x_prompt, x_sample, c_prompt, c_sample, state_conv_a, state_lru, state_conv_b, state_delta, state_s5_re, state_s5_im, norm_w, mod_w, mod_b, ab_in_w, ab_out_w, conv_a_w, conv_a_b, lru_gx_w, lru_gx_b, lru_ga_w, lru_ga_b, lru_a_param, conv_b_w, gdn_a_log, gdn_dt_bias, gdn_norm_w, c_in_w, c_out_w, s5_a_re, s5_a_im, s5_b_re, s5_b_im, s5_c_re, s5_c_im, s5_d, s5_log_dt, glu_w, glu_b, final_norm_w):
    nbp = x_prompt.shape[0]
    nbs = x_sample.shape[0]
    nt = GDN_BLOCK

    mods = _mod_all(jnp.concatenate([c_prompt, c_sample], axis=0), mod_w, mod_b)
    abr, abi, bbr, bbi = _s5_prep(s5_a_re, s5_a_im, s5_log_dt, s5_b_re, s5_b_im)
    fnw = final_norm_w.reshape(1, D_MODEL)

    xp = x_prompt
    xs = x_sample.reshape(nbs, D_MODEL)
    p_states = [[] for _ in range(6)]
    s_states = [[] for _ in range(6)]
    s_delta = None
    for i in range(DEPTH):
        j = i // 2
        normw = norm_w[i].reshape(1, D_MODEL)
        mod_p = mods[i, 0:nbp]
        mod_s = mods[i, nbp:nbp + nbs]
        if i % 2 == 0:
            w = _ab_weights(j, ab_in_w, ab_out_w, conv_a_w, conv_a_b, lru_gx_w, lru_gx_b, lru_ga_w, lru_ga_b,
                            lru_a_param, conv_b_w, gdn_a_log, gdn_dt_bias, gdn_norm_w)
            xp, ca, lh, cb, ds = _ab_prompt(xp, mod_p, normw, w, nbp, nt)
            for lst, val in zip(p_states[:4], (_from_time_major(ca, nbp), lh, _from_time_major(cb, nbp), ds)):
                lst.append(val)
            xs, ca, lh, cb, s_delta = _ab_sample(xs, mod_s, normw, w, _to_time_major(state_conv_a[j]),
                                                 state_lru[j], _to_time_major(state_conv_b[j]), state_delta, j,
                                                 s_delta)
            for lst, val in zip(s_states[:3], (_from_time_major(ca, nbs), lh, _from_time_major(cb, nbs))):
                lst.append(val)
        else:
            w = dict(
                w_in=c_in_w[j].astype(BF16), bre=_block_diag_in(bbr[j]).astype(BF16),
                bim=_block_diag_in(bbi[j]).astype(BF16), cre=_block_diag_out(s5_c_re[j]).astype(BF16),
                cim=_block_diag_out(s5_c_im[j]).astype(BF16), abr=abr[j].reshape(1, NS),
                abi=abi[j].reshape(1, NS), d=s5_d[j].reshape(1, D_MODEL), gluw=glu_w[j].astype(BF16),
                glub=glu_b[j].reshape(1, D_MODEL), w_out=c_out_w[j].astype(BF16), fnw=fnw)
            last = i == DEPTH - 1
            xp, sr, si = _s5_layer(xp, mod_p, normw, w, nbp, nt, None, last, batch_major_out=last)
            p_states[4].append(sr.reshape(nbp, G_C, P_C))
            p_states[5].append(si.reshape(nbp, G_C, P_C))
            xs, sr, si = _s5_layer(xs, mod_s, normw, w, nbs, 1,
                                   (state_s5_re[j].reshape(nbs, NS), state_s5_im[j].reshape(nbs, NS)), last)
            s_states[4].append(sr.reshape(nbs, G_C, P_C))
            s_states[5].append(si.reshape(nbs, G_C, P_C))
    y_prompt = xp
    y_sample = xs.reshape(nbs, 1, D_MODEL)
    stack = lambda lists: tuple(jnp.stack(l) for l in lists)
    s_out = stack(s_states[:3]) + (s_delta,) + stack(s_states[4:])
    return (y_prompt, y_sample) + stack(p_states) + s_out
```

```python
import functools

import jax
import jax.numpy as jnp
from jax import lax
from jax.experimental import pallas as pl
from jax.experimental.pallas import tpu as pltpu

F32 = jnp.float32
BF16 = jnp.bfloat16

D_MODEL = 1024
DEPTH = 4
CONV_W = 4
W_A = 1024
H_A = 8
BW_A = 128
LRU_C = 8.0
H_B = 8
DK = 128
DV = 128
W_B = H_B * DV
QKV_B = 3 * W_B
CG = 16
G_C = 64
P_C = 64
NS = G_C * P_C
EPS = 1e-6
GELU_SQRT_2_OVER_PI = 0.7978845608028654
GELU_CUBIC = 0.044715
LANE = 128
GDN_STEP_SEQS = 8
N_IN = 2 * W_A + QKV_B + W_B + 2 * H_B
BETA_LANE0 = LANE - 2 * H_B
G_LANE0 = LANE - H_B
GDN_BLOCK = 64
GDN_SEQS = 4
TILE_COST = 2
LRU_TASK_COST = 3
SLICE_TASK_COST = 1
S5_LANE_BLOCKS = D_MODEL // LANE
S5_BLOCK_STATE = NS // S5_LANE_BLOCKS
VMEM_LIMIT = 60 * 1024 * 1024


def _dot(a, b):
    return jnp.dot(a.astype(BF16), b.astype(BF16), preferred_element_type=F32)


def _dot_nt(a, b):
    return lax.dot_general(a.astype(BF16), b.astype(BF16), (((1,), (1,)), ((), ())),
                           preferred_element_type=F32)


def _silu(x):
    return x * jax.nn.sigmoid(x)


def _softplus(x):
    return jnp.maximum(x, 0.0) + jnp.log1p(jnp.exp(-jnp.abs(x)))


def _norm_mod(x, normw, mod_ref, nb):
    rows = x.shape[0]
    ms = jnp.mean(x * x, axis=-1, keepdims=True)
    y = x * lax.rsqrt(ms + EPS) * normw
    shift = mod_ref[:, 0:D_MODEL]
    scale = mod_ref[:, D_MODEL:2 * D_MODEL]
    y3 = y.reshape(rows // nb, nb, D_MODEL)
    return (y3 * (1.0 + scale)[None] + shift[None]).reshape(rows, D_MODEL)


def _residual(x, out, mod_ref, nb):
    rows = x.shape[0]
    gate = mod_ref[:, 2 * D_MODEL:3 * D_MODEL]
    return x + (out.reshape(rows // nb, nb, D_MODEL) * gate[None]).reshape(rows, D_MODEL)


def _conv_inplace(ext, lane0, w_ref, w_lane0, nb, rows, post):
    sl = slice(lane0, lane0 + LANE)
    wl = slice(w_lane0, w_lane0 + LANE)
    acc = ext[0:rows, sl] * w_ref[0:1, wl]
    for j in range(1, CONV_W):
        acc = acc + ext[j * nb:j * nb + rows, sl] * w_ref[j:j + 1, wl]
    tail = ext[rows:rows + 3 * nb, sl]
    ext[3 * nb:3 * nb + rows, sl] = post(acc)
    ext[0:3 * nb, sl] = tail


def _l2norm(t):
    return t * lax.rsqrt(jnp.sum(t * t, axis=-1, keepdims=True) + EPS)


def _ab_pre(first, nb, nt, reset_first, x, mod_ref, normw_ref, w_in_ref, caw_ref, cab_ref,
            gw_ref, gxb_ref, gab_ref, apar_ref, cbw_ref, alog_ref, dtb_ref,
            ext_a, ext_b, za, zbs, lb, gsc, bsc, h_s):
    rows = nb * nt
    hb = _norm_mod(x, normw_ref[...], mod_ref, nb).astype(BF16)
    tile = 2 * LANE

    def emit_tile(c0):
        part = _dot_nt(hb, w_in_ref[c0:c0 + tile, :])
        for m in range(tile // LANE):
            c = c0 + m * LANE
            piece = part[:, m * LANE:(m + 1) * LANE]
            if c < W_A:
                ext_a[3 * nb:3 * nb + rows, c:c + LANE] = piece
            elif c < 2 * W_A:
                za[:, c - W_A:c - W_A + LANE] = piece
            elif c < 2 * W_A + QKV_B:
                ext_b[(c - 2 * W_A) // LANE, 3 * nb:3 * nb + rows, :] = piece
            else:
                zbs[(c - 2 * W_A - QKV_B) // LANE] = piece

    def lru_block(blk):
        sl = slice(blk * BW_A, (blk + 1) * BW_A)
        _conv_inplace(ext_a, blk * BW_A, caw_ref, blk * BW_A, nb, rows, lambda acc: acc + cab_ref[:, sl])
        xb = ext_a[3 * nb:3 * nb + rows, sl]
        xbb = xb.astype(BF16)
        gates = _dot(xbb, gw_ref[blk])
        gate_x = jax.nn.sigmoid(gates[:, 0:BW_A] + gxb_ref[:, sl])
        gate_a = jax.nn.sigmoid(gates[:, BW_A:2 * BW_A] + gab_ref[:, sl])
        log_a = -LRU_C * gate_a * _softplus(-apar_ref[:, sl])
        a = jnp.exp(log_a)
        m2 = 1.0 - a * a
        mult = jnp.where(m2 > 0.0, m2 * lax.rsqrt(m2), 0.0)
        if reset_first:
            row = lax.broadcasted_iota(jnp.int32, (rows, BW_A), 0)
            mult = jnp.where(jnp.logical_and(first, row < nb), 1.0, mult)
        bval = mult * gate_x * xb
        h = h_s[:, sl]
        for t in range(nt):
            r = slice(t * nb, (t + 1) * nb)
            h = a[r] * h + bval[r]
            lb[r, sl] = h
        h_s[:, sl] = h

    def qkv_slice(n):
        if n < H_B:
            post = lambda acc: _l2norm(_silu(acc)) * (DK ** -0.5)
        elif n < 2 * H_B:
            post = lambda acc: _l2norm(_silu(acc))
        else:
            post = _silu
        _conv_inplace(ext_b.at[n], 0, cbw_ref, n * LANE, nb, rows, post)

    col_xa, col_za, col_qkv, col_zb = 0, W_A, 2 * W_A, 2 * W_A + QKV_B
    tiles = ([col_xa + i * tile for i in range(W_A // tile)] + [col_qkv + i * tile for i in range(QKV_B // tile)]
             + [col_za + i * tile for i in range(W_A // tile)] + [col_zb + i * tile for i in range(W_B // tile)])
    n_xa = W_A // tile
    tasks = []
    for blk in range(H_A):
        tasks.append((n_xa - 1, LRU_TASK_COST, functools.partial(lru_block, blk)))
        for n in range(3 * blk, 3 * blk + 3):
            tasks.append((n_xa + n // 2, SLICE_TASK_COST, functools.partial(qkv_slice, n)))
    matmul_cost = vector_cost = 0
    for ti, c0 in enumerate(tiles):
        emit_tile(c0)
        matmul_cost += TILE_COST
        while tasks and tasks[0][0] <= ti and vector_cost + tasks[0][1] <= matmul_cost:
            _, cost, task = tasks.pop(0)
            task()
            vector_cost += cost
    ba = _dot_nt(hb, w_in_ref[N_IN - LANE:N_IN, :])
    bsc[...] = jax.nn.sigmoid(ba)
    g = -jnp.exp(alog_ref[...]) * _softplus(ba + dtb_ref[...])
    acc = g[0:nb]
    gsc[0:nb, :] = acc
    for t in range(1, nt):
        acc = acc + g[t * nb:(t + 1) * nb]
        gsc[t * nb:(t + 1) * nb, :] = acc
    for _, _, task in tasks:
        task()


def _gdn_block_masks():
    n = 2 * GDN_BLOCK
    ri = lax.broadcasted_iota(jnp.int32, (n, n), 0)
    ci = lax.broadcasted_iota(jnp.int32, (n, n), 1)
    same = (ri >= GDN_BLOCK) == (ci >= GDN_BLOCK)
    tri = jnp.where(jnp.logical_and(same, ri >= ci), 1.0, 0.0).astype(F32)
    strict = jnp.where(jnp.logical_and(same, ri > ci), 1.0, 0.0).astype(F32)
    eye = jnp.where(ri == ci, 1.0, 0.0).astype(F32)
    levels = []
    for l in range(6):
        rb = ri >> l
        sub = jnp.logical_and((rb & 1) == 1, (ci >> l) == rb - 1)
        levels.append(jnp.where(jnp.logical_and(same, sub), 1.0, 0.0).astype(F32))
    return tri, strict, eye, levels


def _gdn_block(seqs, nb, ext_b, zbs, gsc, bsc, s_ref, gnw_ref, masks):
    c_len = GDN_BLOCK
    top = lax.broadcasted_iota(jnp.int32, (2 * c_len, LANE), 0) < c_len
    tri, strict, eye, levels = masks
    chains = [(si, p) for si in range(len(seqs)) for p in range(H_B // 2)]
    n = range(len(chains))

    def cat(a0, a1):
        return jnp.concatenate([a0, a1], axis=0)

    rows = [pl.ds(3 * nb + b, c_len, stride=nb) for b in seqs]
    zrows = [pl.ds(b, c_len, stride=nb) for b in seqs]

    def head_pair(base, si, p):
        return cat(ext_b[base + 2 * p, rows[si], :], ext_b[base + 2 * p + 1, rows[si], :])

    def col_pair(t, lane0, p):
        return cat(t[:, lane0 + 2 * p:lane0 + 2 * p + 1], t[:, lane0 + 2 * p + 1:lane0 + 2 * p + 2])

    gcb = [gsc[zr, :] for zr in zrows]
    betab = [bsc[zr, :] for zr in zrows]
    glast = [gsc[pl.ds((c_len - 1) * nb + b, 1), :] for b in seqs]
    q = [head_pair(0, si, p) for si, p in chains]
    k = [head_pair(H_B, si, p) for si, p in chains]
    v = [head_pair(2 * H_B, si, p) for si, p in chains]
    c = [col_pair(gcb[si], G_LANE0, p) for si, p in chains]
    bcol = [col_pair(betab[si], BETA_LANE0, p) for si, p in chains]
    gl = [col_pair(jnp.broadcast_to(glast[si], (c_len, LANE)), G_LANE0, p) for si, p in chains]
    gct = [cat(gcb[s], gcb[s + 1]).T for s in range(0, len(seqs), 2)]
    decay = []
    for i, (si, p) in enumerate(chains):
        half = slice((si % 2) * c_len, (si % 2 + 1) * c_len)
        g0 = G_LANE0 + 2 * p
        crow = jnp.concatenate([gct[si // 2][g0:g0 + 1, half], gct[si // 2][g0 + 1:g0 + 2, half]], axis=1)
        cm = jnp.broadcast_to(c[i], (2 * c_len, 2 * c_len))
        decay.append(jnp.exp((cm - jnp.broadcast_to(crow, (2 * c_len, 2 * c_len))) * tri) * tri)
    kb = [k[i] * bcol[i] for i in n]
    a_mat = [_dot_nt(kb[i], k[i]) * decay[i] * strict for i in n]
    qk = [_dot_nt(q[i], k[i]) * decay[i] for i in n]
    x = [eye - a_mat[i] * levels[0] for i in n]
    for l in range(1, 6):
        t = [_dot(a_mat[i] * levels[l], x[i]) for i in n]
        x = [x[i] - _dot(x[i], t[i]) for i in n]
    sol = [_dot(x[i], jnp.concatenate([v[i] * bcol[i], kb[i] * jnp.exp(c[i])], axis=1)) for i in n]
    s0 = [s_ref[seqs[si], 2 * p] for si, p in chains]
    s1 = [s_ref[seqs[si], 2 * p + 1] for si, p in chains]

    def per_head(lhs, i):
        return cat(_dot(lhs[0:c_len], s0[i]), _dot(lhs[c_len:2 * c_len], s1[i]))

    ws = [per_head(sol[i][:, DV:2 * DV], i) for i in n]
    qs = [per_head(q[i] * jnp.exp(c[i]), i) for i in n]
    v_new = [sol[i][:, 0:DV] - ws[i] for i in n]
    o = [qs[i] + _dot(qk[i], v_new[i]) for i in n]
    upd = []
    for i in n:
        kdec = k[i] * jnp.exp(gl[i] - c[i])
        vblk = jnp.concatenate([jnp.where(top, v_new[i], 0.0), jnp.where(top, 0.0, v_new[i])], axis=1)
        upd.append(_dot(kdec.T, vblk))
    for i, (si, p) in enumerate(chains):
        b = seqs[si]
        h0, h1 = 2 * p, 2 * p + 1
        s_ref[b, h0] = s0[i] * jnp.exp(glast[si][:, G_LANE0 + h0:G_LANE0 + h0 + 1]) + upd[i][:, 0:DV]
        s_ref[b, h1] = s1[i] * jnp.exp(glast[si][:, G_LANE0 + h1:G_LANE0 + h1 + 1]) + upd[i][:, DV:2 * DV]
        on = o[i] * lax.rsqrt(jnp.mean(o[i] * o[i], axis=-1, keepdims=True) + EPS) * gnw_ref[...]
        zb = cat(zbs[h0, zrows[si], :], zbs[h1, zrows[si], :])
        og = on * _silu(zb)
        ext_b[h0, rows[si], :] = og[0:c_len]
        ext_b[h1, rows[si], :] = og[c_len:2 * c_len]


def _ab_post(x, nb, mod_ref, ya, o, w_out_ref):
    out = _dot(ya.astype(BF16), w_out_ref[0:W_A, :]) + _dot(o.astype(BF16), w_out_ref[W_A:W_A + W_B, :])
    return _residual(x, out, mod_ref, nb)


def _to_time_major_rows(x_ref, x_tm, nb, nt):
    for b in range(nb):
        for m in range(D_MODEL // LANE):
            x_tm[m, pl.ds(b, nt, stride=nb), :] = x_ref[b, :, m * LANE:(m + 1) * LANE]
    return jnp.concatenate([x_tm[m] for m in range(D_MODEL // LANE)], axis=1)


def _from_time_major_rows(y, y_ref, y_tm, nb, nt):
    for m in range(D_MODEL // LANE):
        y_tm[m] = y[:, m * LANE:(m + 1) * LANE]
    for b in range(nb):
        for m in range(D_MODEL // LANE):
            y_ref[b, :, m * LANE:(m + 1) * LANE] = y_tm[m, pl.ds(b, nt, stride=nb), :]


def _ab_prompt_kernel(nb, nt, batch_major_in, x_ref, mod_ref, normw_ref, w_in_ref, caw_ref, cab_ref,
                      gw_ref, gxb_ref, gab_ref, apar_ref, cbw_ref, alog_ref, dtb_ref, gnw_ref,
                      w_out_ref, y_ref, ca_out, lru_out, cb_out, s_ref,
                      ext_a, ext_b, za, zbs, lb, gsc, bsc, h_s, *x_tm):
    c = pl.program_id(0)
    rows = nb * nt

    @pl.when(c == 0)
    def _():
        ext_a[0:3 * nb, :] = jnp.zeros((3 * nb, W_A), F32)
        ext_b[:, 0:3 * nb, :] = jnp.zeros((QKV_B // LANE, 3 * nb, LANE), F32)
        h_s[...] = jnp.zeros(h_s.shape, F32)
        s_ref[...] = jnp.zeros(s_ref.shape, F32)

    x = _to_time_major_rows(x_ref, x_tm[0], nb, nt) if batch_major_in else x_ref[...]
    _ab_pre(c == 0, nb, nt, True, x, mod_ref, normw_ref, w_in_ref, caw_ref, cab_ref, gw_ref,
            gxb_ref, gab_ref, apar_ref, cbw_ref, alog_ref, dtb_ref,
            ext_a, ext_b, za, zbs, lb, gsc, bsc, h_s)

    masks = _gdn_block_masks()

    def per_seq_group(i, carry):
        _gdn_block([GDN_SEQS * i + s for s in range(GDN_SEQS)], nb, ext_b, zbs, gsc, bsc, s_ref, gnw_ref, masks)
        return carry

    lax.fori_loop(0, nb // GDN_SEQS, per_seq_group, 0)

    ya = lb[...] * _silu(za[...])
    o = jnp.concatenate([ext_b[h, 3 * nb:3 * nb + rows, :] for h in range(H_B)], axis=1)
    if batch_major_in:
        x = jnp.concatenate([x_tm[0][m] for m in range(D_MODEL // LANE)], axis=1)
    else:
        x = x_ref[...]
    y_ref[...] = _ab_post(x, nb, mod_ref, ya, o, w_out_ref)

    @pl.when(c == pl.num_programs(0) - 1)
    def _():
        ca_out[...] = ext_a[0:3 * nb, :]
        for n in range(QKV_B // LANE):
            cb_out[:, n * LANE:(n + 1) * LANE] = ext_b[n, 0:3 * nb, :]
        lru_out[...] = h_s[...]


def _ab_sample_pre_kernel(nb, x_ref, mod_ref, normw_ref, w_in_ref, caw_ref, cab_ref, gw_ref,
                          gxb_ref, gab_ref, apar_ref, cbw_ref, alog_ref, dtb_ref,
                          ca_in, lru_in, cb_in,
                          qkv_out, g_out, beta_out, zb_out, ya_out, ca_out, lru_out, cb_out,
                          ext_a, ext_b, za, zbs, lb, gsc, bsc, h_s):
    ext_a[0:3 * nb, :] = ca_in[...]
    for n in range(QKV_B // LANE):
        ext_b[n, 0:3 * nb, :] = cb_in[:, n * LANE:(n + 1) * LANE]
    h_s[...] = lru_in[...]
    _ab_pre(False, nb, 1, False, x_ref[...], mod_ref, normw_ref, w_in_ref, caw_ref, cab_ref,
            gw_ref, gxb_ref, gab_ref, apar_ref, cbw_ref, alog_ref, dtb_ref,
            ext_a, ext_b, za, zbs, lb, gsc, bsc, h_s)
    for n in range(QKV_B // LANE):
        qkv_out[:, n * LANE:(n + 1) * LANE] = ext_b[n, 3 * nb:4 * nb, :]
        cb_out[:, n * LANE:(n + 1) * LANE] = ext_b[n, 0:3 * nb, :]
    g_out[...] = gsc[...]
    beta_out[...] = bsc[...]
    for h in range(H_B):
        zb_out[:, h * LANE:(h + 1) * LANE] = zbs[h]
    ya_out[...] = lb[...] * _silu(za[...])
    ca_out[...] = ext_a[0:3 * nb, :]
    lru_out[...] = h_s[...]


def _gdn_step_kernel(bb, layer, n_layers, qkv_ref, g_ref, beta_ref, zb_ref, gnw_ref, s_in, *rest):
    if layer == 0:
        o_ref, s_all = rest
        for l in range(1, n_layers):
            s_all[l] = jnp.zeros(s_all.shape[1:], F32)
        s_out = s_all.at[0]
    else:
        _, o_ref, s_out = rest

    def per_seq(i, carry):
        g_row = g_ref[i]
        beta_row = beta_ref[i]
        heads = range(H_B)
        q = [qkv_ref[i, :, h * LANE:(h + 1) * LANE] for h in heads]
        k = [qkv_ref[i, :, H_B * DK + h * LANE:H_B * DK + (h + 1) * LANE] for h in heads]
        v = [qkv_ref[i, :, 2 * H_B * DK + h * LANE:2 * H_B * DK + (h + 1) * LANE] for h in heads]
        eg = [jnp.exp(g_row[:, G_LANE0 + h:G_LANE0 + h + 1]) for h in heads]
        kcol = [jnp.broadcast_to(k[h], (DK, DK)).T for h in heads]
        qcol = [jnp.broadcast_to(q[h], (DK, DK)).T for h in heads]
        s = [s_in[i, h] for h in heads]
        ks = [jnp.sum(kcol[h] * s[h], axis=0, keepdims=True) for h in heads]
        v_new = [beta_row[:, BETA_LANE0 + h:BETA_LANE0 + h + 1] * (v[h] - eg[h] * ks[h]) for h in heads]
        s_new = [eg[h] * s[h] + kcol[h] * v_new[h] for h in heads]
        for h in heads:
            s_out[i, h] = s_new[h]
        o = [jnp.sum(qcol[h] * s_new[h], axis=0, keepdims=True) for h in heads]
        for h in heads:
            on = o[h] * lax.rsqrt(jnp.mean(o[h] * o[h], axis=-1, keepdims=True) + EPS) * gnw_ref[...]
            zb = zb_ref[i, :, h * LANE:(h + 1) * LANE]
            o_ref[i, :, h * LANE:(h + 1) * LANE] = on * _silu(zb)
        return carry

    lax.fori_loop(0, bb, per_seq, 0)


def _ab_sample_post_kernel(nb, x_ref, mod_ref, ya_ref, o_ref, w_out_ref, y_ref):
    y_ref[...] = _ab_post(x_ref[...], nb, mod_ref, ya_ref[...], o_ref[...], w_out_ref)


def _s5_kernel(nb, nt, has_state, final_norm, batch_major_out, *refs):
    (x_ref, mod_ref, normw_ref, w_in_ref, bre_ref, bim_ref, cre_ref, cim_ref, abr_ref, abi_ref, d_ref,
     gluw_ref, glub_ref, w_out_ref, fnw_ref) = refs[:15]
    refs = refs[15:]
    if has_state:
        sre_in, sim_in = refs[:2]
        refs = refs[2:]
    y_ref, sre_out, sim_out, xs_re, xs_im, y_s = refs[:6]
    c = pl.program_id(0)
    rows = nb * nt

    @pl.when(c == 0)
    def _():
        if has_state:
            xs_re[0:nb, :] = sre_in[...]
            xs_im[0:nb, :] = sim_in[...]
        else:
            xs_re[0:nb, :] = jnp.zeros((nb, NS), F32)
            xs_im[0:nb, :] = jnp.zeros((nb, NS), F32)

    hb = _norm_mod(x_ref[...], normw_ref[...], mod_ref, nb).astype(BF16)
    uz = _dot(hb, w_in_ref[...])
    u = uz[:, 0:D_MODEL]
    z = uz[:, D_MODEL:2 * D_MODEL]
    ub = u.astype(BF16)
    def b_proj(j):
        sl = slice(j * S5_BLOCK_STATE, (j + 1) * S5_BLOCK_STATE)
        uj = ub[:, j * LANE:(j + 1) * LANE]
        xs_re[nb:nb + rows, sl] = _dot(uj, bre_ref[j])
        xs_im[nb:nb + rows, sl] = _dot(uj, bim_ref[j])

    def recurrence(j):
        sl = slice(j * S5_BLOCK_STATE, (j + 1) * S5_BLOCK_STATE)
        ar = jnp.broadcast_to(abr_ref[:, sl], (nb, S5_BLOCK_STATE))
        ai = jnp.broadcast_to(abi_ref[:, sl], (nb, S5_BLOCK_STATE))
        sr, si = xs_re[0:nb, sl], xs_im[0:nb, sl]
        for t in range(nt):
            r = slice(nb + t * nb, 2 * nb + t * nb)
            sr, si = ar * sr - ai * si + xs_re[r, sl], ar * si + ai * sr + xs_im[r, sl]
            xs_re[r, sl] = sr
            xs_im[r, sl] = si
        xs_re[0:nb, sl] = sr
        xs_im[0:nb, sl] = si

    def c_proj(j):
        sl = slice(j * S5_BLOCK_STATE, (j + 1) * S5_BLOCK_STATE)
        cl = slice(j * LANE, (j + 1) * LANE)
        yj = (_dot(xs_re[nb:nb + rows, sl].astype(BF16), cre_ref[j])
              - _dot(xs_im[nb:nb + rows, sl].astype(BF16), cim_ref[j]))
        yj = yj + d_ref[:, cl] * u[:, cl]
        cdf = 0.5 * (1.0 + jnp.tanh(GELU_SQRT_2_OVER_PI * (yj + GELU_CUBIC * (yj * yj * yj))))
        y_s[:, cl] = yj * cdf

    for j in range(S5_LANE_BLOCKS):
        b_proj(j)
    for j in range(S5_LANE_BLOCKS):
        recurrence(j)
        c_proj(j)
    n_parts = 2 if nt % 2 == 0 else 1
    parts = []
    for part in range(n_parts):
        rs = slice(part * (rows // n_parts), (part + 1) * (rows // n_parts))
        y = y_s[rs, :]
        y = y * jax.nn.sigmoid(_dot(y.astype(BF16), gluw_ref[...]) + glub_ref[...])
        y = y * _silu(z[rs])
        xn = _residual(x_ref[rs, :], _dot(y.astype(BF16), w_out_ref[...]), mod_ref, nb)
        if final_norm:
            xn = xn * lax.rsqrt(jnp.mean(xn * xn, axis=-1, keepdims=True) + EPS) * fnw_ref[...]
        parts.append(xn)
    xn = jnp.concatenate(parts, axis=0)
    if batch_major_out:
        _from_time_major_rows(xn, y_ref, refs[6], nb, nt)
    else:
        y_ref[...] = xn

    @pl.when(c == pl.num_programs(0) - 1)
    def _():
        sre_out[...] = xs_re[0:nb, :]
        sim_out[...] = xs_im[0:nb, :]


def _s5_prep_kernel(are_ref, aim_ref, ldt_ref, bre_ref, bim_ref, abr_out, abi_out, bbr_out, bbi_out):
    a_re = are_ref[...]
    a_im = aim_ref[...]
    dt = jnp.exp(ldt_ref[...])
    mag = jnp.exp(a_re * dt)
    abr = mag * jnp.cos(a_im * dt)
    abi = mag * jnp.sin(a_im * dt)
    abr_out[...] = abr
    abi_out[...] = abi
    den = a_re * a_re + a_im * a_im
    nr = abr - 1.0
    cr = (nr * a_re + abi * a_im) / den
    ci = (abi * a_re - nr * a_im) / den
    b_re = bre_ref[...]
    b_im = bim_ref[...]
    bbr_out[...] = cr[:, None, :] * b_re - ci[:, None, :] * b_im
    bbi_out[...] = cr[:, None, :] * b_im + ci[:, None, :] * b_re


def _mod_kernel(c_ref, w_ref, b_ref, o_ref):
    c = c_ref[...]
    o_ref[...] = _dot(_silu(c).astype(BF16), w_ref[...].astype(BF16)) + b_ref[...]


def _full(shape):
    n = len(shape)
    return pl.BlockSpec(shape, lambda *_: (0,) * n)


def _const(shape):
    n = len(shape)
    return pl.BlockSpec(shape, lambda *_: (0,) * n, pipeline_mode=pl.Buffered(1))


def _params(n_grid):
    return pltpu.CompilerParams(dimension_semantics=("arbitrary",) * n_grid, vmem_limit_bytes=VMEM_LIMIT)


def _mod_all(c_all, mod_w, mod_b):
    n = c_all.shape[0]
    return pl.pallas_call(
        _mod_kernel,
        grid=(DEPTH, 3),
        in_specs=[pl.BlockSpec((n, D_MODEL), lambda i, j: (0, 0)),
                  pl.BlockSpec((None, D_MODEL, D_MODEL), lambda i, j: (i, 0, j)),
                  pl.BlockSpec((None, 1, D_MODEL), lambda i, j: (i, 0, j))],
        out_specs=pl.BlockSpec((None, n, D_MODEL), lambda i, j: (i, 0, j)),
        out_shape=jax.ShapeDtypeStruct((DEPTH, n, 3 * D_MODEL), F32),
        compiler_params=_params(2),
        name="mod_all",
    )(c_all, mod_w, mod_b.reshape(DEPTH, 1, 3 * D_MODEL))


def _ab_weights(j, w_in_all, ab_out_w, conv_a_w, conv_a_b, lru_gx_w, lru_gx_b, lru_ga_w, lru_ga_b,
                lru_a_param, conv_b_w, gdn_a_log, gdn_dt_bias, gdn_norm_w):
    pad = lambda t: jnp.zeros((1, LANE), F32).at[0, G_LANE0:G_LANE0 + H_B].set(t)
    return dict(
        layer=j, w_in=w_in_all, caw=conv_a_w[j], cab=conv_a_b[j].reshape(1, W_A),
        gw=jnp.concatenate([lru_gx_w[j], lru_ga_w[j]], axis=-1).astype(BF16),
        gxb=lru_gx_b[j].reshape(1, W_A), gab=lru_ga_b[j].reshape(1, W_A),
        apar=lru_a_param[j].reshape(1, W_A), cbw=conv_b_w[j], alog=pad(gdn_a_log[j]),
        dtb=pad(gdn_dt_bias[j]), gnw=gdn_norm_w[j].reshape(1, DV), w_out=ab_out_w[j].astype(BF16))


_AB_PRE_NAMES = ("w_in", "caw", "cab", "gw", "gxb", "gab", "apar", "cbw", "alog", "dtb")


def _ab_pre_specs(w):
    layer = w["layer"]
    specs = []
    for name in _AB_PRE_NAMES:
        if name == "w_in":
            specs.append(pl.BlockSpec((None, N_IN, D_MODEL), lambda *_: (layer, 0, 0),
                                      pipeline_mode=pl.Buffered(1)))
        else:
            specs.append(_const(w[name].shape))
    return specs


def _ab_scratch(nb, rows):
    return [pltpu.VMEM((3 * nb + rows, W_A), F32), pltpu.VMEM((QKV_B // LANE, 3 * nb + rows, LANE), F32),
            pltpu.VMEM((rows, W_A), F32), pltpu.VMEM((H_B, rows, LANE), F32),
            pltpu.VMEM((rows, W_A), F32),
            pltpu.VMEM((rows, LANE), F32), pltpu.VMEM((rows, LANE), F32), pltpu.VMEM((nb, W_A), F32)]


def _ab_prompt(x, mod, normw, w, nb, nt):
    rows = nb * nt
    batch_major = x.ndim == 3
    n_rows = x.shape[0] * x.shape[1] if batch_major else x.shape[0]
    n_chunks = n_rows // rows
    pre = [w[k] for k in _AB_PRE_NAMES]
    if batch_major:
        x_spec = pl.BlockSpec((nb, nt, D_MODEL), lambda c: (0, c, 0))
        extra = [pltpu.VMEM((D_MODEL // LANE, rows, LANE), F32)]
    else:
        x_spec = pl.BlockSpec((rows, D_MODEL), lambda c: (c, 0))
        extra = []
    return pl.pallas_call(
        functools.partial(_ab_prompt_kernel, nb, nt, batch_major),
        grid=(n_chunks,),
        in_specs=[x_spec, _const(mod.shape), _const(normw.shape)]
        + _ab_pre_specs(w) + [_const(w["gnw"].shape), _const(w["w_out"].shape)],
        out_specs=[pl.BlockSpec((rows, D_MODEL), lambda c: (c, 0)), _full((3 * nb, W_A)), _full((nb, W_A)),
                   _full((3 * nb, QKV_B)), _const((nb, H_B, DK, DV))],
        out_shape=[jax.ShapeDtypeStruct((n_rows, D_MODEL), F32), jax.ShapeDtypeStruct((3 * nb, W_A), F32),
                   jax.ShapeDtypeStruct((nb, W_A), F32), jax.ShapeDtypeStruct((3 * nb, QKV_B), F32),
                   jax.ShapeDtypeStruct((nb, H_B, DK, DV), F32)],
        scratch_shapes=_ab_scratch(nb, rows) + extra,
        compiler_params=_params(1),
        name="ab_prompt",
    )(x, mod, normw, *pre, w["gnw"], w["w_out"])


def _ab_sample(x, mod, normw, w, conv_a, lru_h, conv_b, delta_all, layer, delta_new_all):
    nb = x.shape[0]
    pre = [w[k] for k in _AB_PRE_NAMES]
    ins = [x, mod, normw, *pre, conv_a, lru_h, conv_b]
    in_specs = ([_full(x.shape), _full(mod.shape), _full(normw.shape)] + _ab_pre_specs(w)
                + [_full(conv_a.shape), _full(lru_h.shape), _full(conv_b.shape)])
    outs = [(nb, QKV_B), (nb, LANE), (nb, LANE), (nb, W_B), (nb, W_A), (3 * nb, W_A), (nb, W_A),
            (3 * nb, QKV_B)]
    qkv, g, beta, zb, ya, ca_new, lru_new, cb_new = pl.pallas_call(
        functools.partial(_ab_sample_pre_kernel, nb),
        grid=(1,),
        in_specs=in_specs,
        out_specs=[_full(s) for s in outs],
        out_shape=[jax.ShapeDtypeStruct(s, F32) for s in outs],
        scratch_shapes=_ab_scratch(nb, nb),
        compiler_params=_params(1),
        name="ab_sample_pre",
    )(*ins)
    bb = GDN_STEP_SEQS
    row_block = lambda width: pl.BlockSpec((bb, 1, width), lambda i: (i, 0, 0))
    n_layers = delta_all.shape[0]
    state_block = pl.BlockSpec((None, bb, H_B, DK, DV), lambda i: (layer, i, 0, 0, 0))
    step_ins = [qkv.reshape(nb, 1, QKV_B), g.reshape(nb, 1, LANE), beta.reshape(nb, 1, LANE),
                zb.reshape(nb, 1, W_B), w["gnw"], delta_all]
    step_specs = [row_block(QKV_B), row_block(LANE), row_block(LANE), row_block(W_B), _full((1, DV)),
                  state_block]
    if layer == 0:
        out_state_block = pl.BlockSpec((n_layers, bb, H_B, DK, DV), lambda i: (0, i, 0, 0, 0))
        aliases = {}
    else:
        step_ins.append(delta_new_all)
        step_specs.append(pl.BlockSpec(memory_space=pl.ANY))
        out_state_block = state_block
        aliases = {len(step_ins) - 1: 1}
    o, delta_new_all = pl.pallas_call(
        functools.partial(_gdn_step_kernel, bb, layer, n_layers),
        grid=(nb // bb,),
        in_specs=step_specs,
        out_specs=[row_block(W_B), out_state_block],
        out_shape=[jax.ShapeDtypeStruct((nb, 1, W_B), F32), jax.ShapeDtypeStruct(delta_all.shape, F32)],
        input_output_aliases=aliases,
        compiler_params=_params(1),
        name="gdn_step",
    )(*step_ins)
    o = o.reshape(nb, W_B)
    y = pl.pallas_call(
        functools.partial(_ab_sample_post_kernel, nb),
        in_specs=[_full(x.shape), _full(mod.shape), _full(ya.shape), _full(o.shape), _full(w["w_out"].shape)],
        out_specs=_full(x.shape),
        out_shape=jax.ShapeDtypeStruct(x.shape, F32),
        compiler_params=pltpu.CompilerParams(vmem_limit_bytes=VMEM_LIMIT),
        name="ab_sample_post",
    )(x, mod, ya, o, w["w_out"])
    return y, ca_new, lru_new, cb_new, delta_new_all


def _s5_prep(s5_a_re, s5_a_im, s5_log_dt, s5_b_re, s5_b_im):
    n = s5_a_re.shape[0]
    gp = pl.BlockSpec((None, G_C, P_C), lambda i: (i, 0, 0))
    gcp = pl.BlockSpec((None, G_C, CG, P_C), lambda i: (i, 0, 0, 0))
    return pl.pallas_call(
        _s5_prep_kernel,
        grid=(n,),
        in_specs=[gp, gp, pl.BlockSpec((None, G_C, 1), lambda i: (i, 0, 0)), gcp, gcp],
        out_specs=[gp, gp, gcp, gcp],
        out_shape=[jax.ShapeDtypeStruct((n, G_C, P_C), F32)] * 2
        + [jax.ShapeDtypeStruct((n, G_C, CG, P_C), F32)] * 2,
        compiler_params=_params(1),
        name="s5_prep",
    )(s5_a_re, s5_a_im, s5_log_dt.reshape(n, G_C, 1), jnp.swapaxes(s5_b_re, 2, 3), jnp.swapaxes(s5_b_im, 2, 3))


def _block_diag_in(t):
    gl = G_C // S5_LANE_BLOCKS
    n = t.shape[0]
    t = t.reshape(n, S5_LANE_BLOCKS, gl, CG, P_C)
    eye = jnp.eye(gl, dtype=t.dtype)
    return jnp.einsum("njgcp,gh->njgchp", t, eye).reshape(n, S5_LANE_BLOCKS, gl * CG, gl * P_C)


def _block_diag_out(t):
    gl = G_C // S5_LANE_BLOCKS
    n = t.shape[0]
    t = t.reshape(n, S5_LANE_BLOCKS, gl, CG, P_C)
    eye = jnp.eye(gl, dtype=t.dtype)
    return jnp.einsum("njgcp,gh->njgphc", t, eye).reshape(n, S5_LANE_BLOCKS, gl * P_C, gl * CG)


def _layer_spec(stacked, layer):
    rest = stacked.shape[1:]
    return pl.BlockSpec((None,) + rest, lambda *_: (layer,) + (0,) * len(rest), pipeline_mode=pl.Buffered(1))


_S5_STACKED = ("w_in", "bre", "bim", "cre", "cim", "abr", "abi", "d", "gluw", "glub", "w_out")


def _s5_layer(x, mod, normw, w, layer, nb, nt, state, final_norm, batch_major_out=False):
    rows = nb * nt
    n_chunks = x.shape[0] // rows
    stacked = [w[k] for k in _S5_STACKED]
    ins = [x, mod, normw, *stacked, w["fnw"]]
    in_specs = ([pl.BlockSpec((rows, D_MODEL), lambda c: (c, 0)), _const(mod.shape), _const(normw.shape)]
                + [_layer_spec(t, layer) for t in stacked] + [_const(w["fnw"].shape)])
    if state is not None:
        ins += list(state)
        in_specs += [_const((nb, NS))] * 2
    if batch_major_out:
        y_spec = pl.BlockSpec((nb, nt, D_MODEL), lambda c: (0, c, 0))
        y_shape = (nb, x.shape[0] // nb, D_MODEL)
        extra = [pltpu.VMEM((D_MODEL // LANE, rows, LANE), F32)]
    else:
        y_spec = pl.BlockSpec((rows, D_MODEL), lambda c: (c, 0))
        y_shape = x.shape
        extra = []
    return pl.pallas_call(
        functools.partial(_s5_kernel, nb, nt, state is not None, final_norm, batch_major_out),
        grid=(n_chunks,),
        in_specs=in_specs,
        out_specs=[y_spec, _full((nb, NS)), _full((nb, NS))],
        out_shape=[jax.ShapeDtypeStruct(y_shape, F32), jax.ShapeDtypeStruct((nb, NS), F32),
                   jax.ShapeDtypeStruct((nb, NS), F32)],
        scratch_shapes=[pltpu.VMEM((nb + rows, NS), F32), pltpu.VMEM((nb + rows, NS), F32),
                        pltpu.VMEM((rows, D_MODEL), F32)] + extra,
        compiler_params=_params(1),
        name="s5_layer",
    )(*ins)


def _to_time_major(t):
    b, l, c = t.shape
    return jnp.swapaxes(t, 0, 1).reshape(l * b, c)


def _from_time_major(t, b):
    lb, c = t.shape
    return jnp.swapaxes(t.reshape(lb // b, b, c), 0, 1)


def kernel(x_prompt, x_sample, c_prompt, c_sample, state_conv_a, state_lru, state_conv_b, state_delta, state_s5_re, state_s5_im, norm_w, mod_w, mod_b, ab_in_w, ab_out_w, conv_a_w, conv_a_b, lru_gx_w, lru_gx_b, lru_ga_w, lru_ga_b, lru_a_param, conv_b_w, gdn_a_log, gdn_dt_bias, gdn_norm_w, c_in_w, c_out_w, s5_a_re, s5_a_im, s5_b_re, s5_b_im, s5_c_re, s5_c_im, s5_d, s5_log_dt, glu_w, glu_b, final_norm_w):
    nbp = x_prompt.shape[0]
    nbs = x_sample.shape[0]
    nt = GDN_BLOCK

    mods = _mod_all(jnp.concatenate([c_prompt, c_sample], axis=0), mod_w, mod_b)
    abr, abi, bbr, bbi = _s5_prep(s5_a_re, s5_a_im, s5_log_dt, s5_b_re, s5_b_im)
    fnw = final_norm_w.reshape(1, D_MODEL)
    n_c = c_in_w.shape[0]
    s5w = dict(
        w_in=c_in_w.astype(BF16), bre=_block_diag_in(bbr).astype(BF16), bim=_block_diag_in(bbi).astype(BF16),
        cre=_block_diag_out(s5_c_re).astype(BF16), cim=_block_diag_out(s5_c_im).astype(BF16),
        abr=abr.reshape(n_c, 1, NS), abi=abi.reshape(n_c, 1, NS), d=s5_d.reshape(n_c, 1, D_MODEL),
        gluw=glu_w.astype(BF16), glub=glu_b.reshape(n_c, 1, D_MODEL), w_out=c_out_w.astype(BF16), fnw=fnw)
    w_in_all = jnp.swapaxes(ab_in_w, 1, 2).astype(BF16)

    xp = x_prompt
    xs = x_sample.reshape(nbs, D_MODEL)
    p_states = [[] for _ in range(6)]
    s_states = [[] for _ in range(6)]
    s_delta = None
    for i in range(DEPTH):
        j = i // 2
        normw = norm_w[i].reshape(1, D_MODEL)
        mod_p = mods[i, 0:nbp]
        mod_s = mods[i, nbp:nbp + nbs]
        if i % 2 == 0:
            w = _ab_weights(j, w_in_all, ab_out_w, conv_a_w, conv_a_b, lru_gx_w, lru_gx_b, lru_ga_w, lru_ga_b,
                            lru_a_param, conv_b_w, gdn_a_log, gdn_dt_bias, gdn_norm_w)
            xp, ca, lh, cb, ds = _ab_prompt(xp, mod_p, normw, w, nbp, nt)
            for lst, val in zip(p_states[:4], (_from_time_major(ca, nbp), lh, _from_time_major(cb, nbp), ds)):
                lst.append(val)
            xs, ca, lh, cb, s_delta = _ab_sample(xs, mod_s, normw, w, _to_time_major(state_conv_a[j]),
                                                 state_lru[j], _to_time_major(state_conv_b[j]), state_delta, j,
                                                 s_delta)
            for lst, val in zip(s_states[:3], (_from_time_major(ca, nbs), lh, _from_time_major(cb, nbs))):
                lst.append(val)
        else:
            last = i == DEPTH - 1
            xp, sr, si = _s5_layer(xp, mod_p, normw, s5w, j, nbp, nt, None, last, batch_major_out=last)
            p_states[4].append(sr.reshape(nbp, G_C, P_C))
            p_states[5].append(si.reshape(nbp, G_C, P_C))
            xs, sr, si = _s5_layer(xs, mod_s, normw, s5w, j, nbs, 1,
                                   (state_s5_re[j].reshape(nbs, NS), state_s5_im[j].reshape(nbs, NS)), last)
            s_states[4].append(sr.reshape(nbs, G_C, P_C))
            s_states[5].append(si.reshape(nbs, G_C, P_C))
    y_prompt = xp
    y_sample = xs.reshape(nbs, 1, D_MODEL)
    stack = lambda lists: tuple(jnp.stack(l) for l in lists)
    s_out = stack(s_states[:3]) + (s_delta,) + stack(s_states[4:])
    return (y_prompt, y_sample) + stack(p_states) + s_out
```

```python
import functools

import jax
import jax.numpy as jnp
from jax import lax
from jax.experimental import pallas as pl
from jax.experimental.pallas import tpu as pltpu

F32 = jnp.float32
BF16 = jnp.bfloat16

D_MODEL = 1024
DEPTH = 4
CONV_W = 4
W_A = 1024
H_A = 8
BW_A = 128
LRU_C = 8.0
H_B = 8
DK = 128
DV = 128
W_B = H_B * DV
QKV_B = 3 * W_B
CG = 16
G_C = 64
P_C = 64
NS = G_C * P_C
EPS = 1e-6
GELU_SQRT_2_OVER_PI = 0.7978845608028654
GELU_CUBIC = 0.044715
LANE = 128
GDN_STEP_SEQS = 8
N_IN = 2 * W_A + QKV_B + W_B + 2 * H_B
BETA_LANE0 = LANE - 2 * H_B
G_LANE0 = LANE - H_B
GDN_BLOCK = 64
GDN_SEQS = 4
TILE_COST = 2
LRU_TASK_COST = 3
SLICE_TASK_COST = 1
S5_LANE_BLOCKS = D_MODEL // LANE
S5_BLOCK_STATE = NS // S5_LANE_BLOCKS
VMEM_LIMIT = 60 * 1024 * 1024


def _dot(a, b):
    return jnp.dot(a.astype(BF16), b.astype(BF16), preferred_element_type=F32)


def _dot_nt(a, b):
    return lax.dot_general(a.astype(BF16), b.astype(BF16), (((1,), (1,)), ((), ())),
                           preferred_element_type=F32)


def _silu(x):
    return x * jax.nn.sigmoid(x)


def _softplus(x):
    return jnp.maximum(x, 0.0) + jnp.log1p(jnp.exp(-jnp.abs(x)))


def _norm_mod(x, normw, mod_ref, nb):
    rows = x.shape[0]
    ms = jnp.mean(x * x, axis=-1, keepdims=True)
    y = x * lax.rsqrt(ms + EPS) * normw
    shift = mod_ref[:, 0:D_MODEL]
    scale = mod_ref[:, D_MODEL:2 * D_MODEL]
    y3 = y.reshape(rows // nb, nb, D_MODEL)
    return (y3 * (1.0 + scale)[None] + shift[None]).reshape(rows, D_MODEL)


def _residual(x, out, mod_ref, nb):
    rows = x.shape[0]
    gate = mod_ref[:, 2 * D_MODEL:3 * D_MODEL]
    return x + (out.reshape(rows // nb, nb, D_MODEL) * gate[None]).reshape(rows, D_MODEL)


def _conv_inplace(ext, lane0, w_ref, w_lane0, nb, rows, post):
    sl = slice(lane0, lane0 + LANE)
    wl = slice(w_lane0, w_lane0 + LANE)
    acc = ext[0:rows, sl] * w_ref[0:1, wl]
    for j in range(1, CONV_W):
        acc = acc + ext[j * nb:j * nb + rows, sl] * w_ref[j:j + 1, wl]
    tail = ext[rows:rows + 3 * nb, sl]
    ext[3 * nb:3 * nb + rows, sl] = post(acc)
    ext[0:3 * nb, sl] = tail


def _l2norm(t):
    return t * lax.rsqrt(jnp.sum(t * t, axis=-1, keepdims=True) + EPS)


def _ab_pre(first, nb, nt, reset_first, x, mod_ref, normw_ref, w_in_ref, caw_ref, cab_ref,
            gw_ref, gxb_ref, gab_ref, apar_ref, cbw_ref, alog_ref, dtb_ref,
            ext_a, ext_b, za, zbs, lb, gsc, bsc, h_s):
    rows = nb * nt
    hb = _norm_mod(x, normw_ref[...], mod_ref, nb).astype(BF16)
    tile = 2 * LANE

    def emit_tile(c0):
        part = _dot_nt(hb, w_in_ref[c0:c0 + tile, :])
        for m in range(tile // LANE):
            c = c0 + m * LANE
            piece = part[:, m * LANE:(m + 1) * LANE]
            if c < W_A:
                ext_a[3 * nb:3 * nb + rows, c:c + LANE] = piece
            elif c < 2 * W_A:
                za[:, c - W_A:c - W_A + LANE] = piece
            elif c < 2 * W_A + QKV_B:
                ext_b[(c - 2 * W_A) // LANE, 3 * nb:3 * nb + rows, :] = piece
            else:
                zbs[(c - 2 * W_A - QKV_B) // LANE] = piece

    def lru_block(blk):
        sl = slice(blk * BW_A, (blk + 1) * BW_A)
        _conv_inplace(ext_a, blk * BW_A, caw_ref, blk * BW_A, nb, rows, lambda acc: acc + cab_ref[:, sl])
        xb = ext_a[3 * nb:3 * nb + rows, sl]
        xbb = xb.astype(BF16)
        gates = _dot(xbb, gw_ref[blk])
        gate_x = jax.nn.sigmoid(gates[:, 0:BW_A] + gxb_ref[:, sl])
        gate_a = jax.nn.sigmoid(gates[:, BW_A:2 * BW_A] + gab_ref[:, sl])
        log_a = -LRU_C * gate_a * _softplus(-apar_ref[:, sl])
        a = jnp.exp(log_a)
        m2 = 1.0 - a * a
        mult = jnp.where(m2 > 0.0, m2 * lax.rsqrt(m2), 0.0)
        if reset_first:
            row = lax.broadcasted_iota(jnp.int32, (rows, BW_A), 0)
            mult = jnp.where(jnp.logical_and(first, row < nb), 1.0, mult)
        bval = mult * gate_x * xb
        h = h_s[:, sl]
        for t in range(nt):
            r = slice(t * nb, (t + 1) * nb)
            h = a[r] * h + bval[r]
            lb[r, sl] = h
        h_s[:, sl] = h

    def qkv_slice(n):
        if n < H_B:
            post = lambda acc: _l2norm(_silu(acc)) * (DK ** -0.5)
        elif n < 2 * H_B:
            post = lambda acc: _l2norm(_silu(acc))
        else:
            post = _silu
        _conv_inplace(ext_b.at[n], 0, cbw_ref, n * LANE, nb, rows, post)

    col_xa, col_za, col_qkv, col_zb = 0, W_A, 2 * W_A, 2 * W_A + QKV_B
    tiles = ([col_xa + i * tile for i in range(W_A // tile)] + [col_qkv + i * tile for i in range(QKV_B // tile)]
             + [col_za + i * tile for i in range(W_A // tile)] + [col_zb + i * tile for i in range(W_B // tile)])
    n_xa = W_A // tile
    tasks = []
    for blk in range(H_A):
        tasks.append((n_xa - 1, LRU_TASK_COST, functools.partial(lru_block, blk)))
        for n in range(3 * blk, 3 * blk + 3):
            tasks.append((n_xa + n // 2, SLICE_TASK_COST, functools.partial(qkv_slice, n)))
    matmul_cost = vector_cost = 0
    for ti, c0 in enumerate(tiles):
        emit_tile(c0)
        matmul_cost += TILE_COST
        while tasks and tasks[0][0] <= ti and vector_cost + tasks[0][1] <= matmul_cost:
            _, cost, task = tasks.pop(0)
            task()
            vector_cost += cost
    ba = _dot_nt(hb, w_in_ref[N_IN - LANE:N_IN, :])
    bsc[...] = jax.nn.sigmoid(ba)
    g = -jnp.exp(alog_ref[...]) * _softplus(ba + dtb_ref[...])
    acc = g[0:nb]
    gsc[0:nb, :] = acc
    for t in range(1, nt):
        acc = acc + g[t * nb:(t + 1) * nb]
        gsc[t * nb:(t + 1) * nb, :] = acc
    for _, _, task in tasks:
        task()


def _gdn_block_masks():
    n = 2 * GDN_BLOCK
    ri = lax.broadcasted_iota(jnp.int32, (n, n), 0)
    ci = lax.broadcasted_iota(jnp.int32, (n, n), 1)
    same = (ri >= GDN_BLOCK) == (ci >= GDN_BLOCK)
    tri = jnp.where(jnp.logical_and(same, ri >= ci), 1.0, 0.0).astype(F32)
    strict = jnp.where(jnp.logical_and(same, ri > ci), 1.0, 0.0).astype(F32)
    eye = jnp.where(ri == ci, 1.0, 0.0).astype(F32)
    levels = []
    for l in range(6):
        rb = ri >> l
        sub = jnp.logical_and((rb & 1) == 1, (ci >> l) == rb - 1)
        levels.append(jnp.where(jnp.logical_and(same, sub), 1.0, 0.0).astype(F32))
    return tri, strict, eye, levels


def _gdn_block(seqs, nb, ext_b, zbs, gsc, bsc, s_ref, gnw_ref, masks):
    c_len = GDN_BLOCK
    top = lax.broadcasted_iota(jnp.int32, (2 * c_len, LANE), 0) < c_len
    tri, strict, eye, levels = masks
    chains = [(si, p) for si in range(len(seqs)) for p in range(H_B // 2)]
    n = range(len(chains))

    def cat(a0, a1):
        return jnp.concatenate([a0, a1], axis=0)

    rows = [pl.ds(3 * nb + b, c_len, stride=nb) for b in seqs]
    zrows = [pl.ds(b, c_len, stride=nb) for b in seqs]

    def head_pair(base, si, p):
        return cat(ext_b[base + 2 * p, rows[si], :], ext_b[base + 2 * p + 1, rows[si], :])

    def col_pair(t, lane0, p):
        return cat(t[:, lane0 + 2 * p:lane0 + 2 * p + 1], t[:, lane0 + 2 * p + 1:lane0 + 2 * p + 2])

    gcb = [gsc[zr, :] for zr in zrows]
    betab = [bsc[zr, :] for zr in zrows]
    glast = [gsc[pl.ds((c_len - 1) * nb + b, 1), :] for b in seqs]
    q = [head_pair(0, si, p) for si, p in chains]
    k = [head_pair(H_B, si, p) for si, p in chains]
    v = [head_pair(2 * H_B, si, p) for si, p in chains]
    c = [col_pair(gcb[si], G_LANE0, p) for si, p in chains]
    bcol = [col_pair(betab[si], BETA_LANE0, p) for si, p in chains]
    gl = [col_pair(jnp.broadcast_to(glast[si], (c_len, LANE)), G_LANE0, p) for si, p in chains]
    gct = [cat(gcb[s], gcb[s + 1]).T for s in range(0, len(seqs), 2)]
    decay = []
    for i, (si, p) in enumerate(chains):
        half = slice((si % 2) * c_len, (si % 2 + 1) * c_len)
        g0 = G_LANE0 + 2 * p
        crow = jnp.concatenate([gct[si // 2][g0:g0 + 1, half], gct[si // 2][g0 + 1:g0 + 2, half]], axis=1)
        cm = jnp.broadcast_to(c[i], (2 * c_len, 2 * c_len))
        decay.append(jnp.exp((cm - jnp.broadcast_to(crow, (2 * c_len, 2 * c_len))) * tri) * tri)
    kb = [k[i] * bcol[i] for i in n]
    a_mat = [_dot_nt(kb[i], k[i]) * decay[i] * strict for i in n]
    qk = [_dot_nt(q[i], k[i]) * decay[i] for i in n]
    x = [eye - a_mat[i] * levels[0] for i in n]
    for l in range(1, 6):
        t = [_dot(a_mat[i] * levels[l], x[i]) for i in n]
        x = [x[i] - _dot(x[i], t[i]) for i in n]
    sol = [_dot(x[i], jnp.concatenate([v[i] * bcol[i], kb[i] * jnp.exp(c[i])], axis=1)) for i in n]
    s0 = [s_ref[seqs[si], 2 * p] for si, p in chains]
    s1 = [s_ref[seqs[si], 2 * p + 1] for si, p in chains]

    def per_head(lhs, i):
        return cat(_dot(lhs[0:c_len], s0[i]), _dot(lhs[c_len:2 * c_len], s1[i]))

    ws = [per_head(sol[i][:, DV:2 * DV], i) for i in n]
    qs = [per_head(q[i] * jnp.exp(c[i]), i) for i in n]
    v_new = [sol[i][:, 0:DV] - ws[i] for i in n]
    o = [qs[i] + _dot(qk[i], v_new[i]) for i in n]
    upd = []
    for i in n:
        kdec = k[i] * jnp.exp(gl[i] - c[i])
        vblk = jnp.concatenate([jnp.where(top, v_new[i], 0.0), jnp.where(top, 0.0, v_new[i])], axis=1)
        upd.append(_dot(kdec.T, vblk))
    for i, (si, p) in enumerate(chains):
        b = seqs[si]
        h0, h1 = 2 * p, 2 * p + 1
        s_ref[b, h0] = s0[i] * jnp.exp(glast[si][:, G_LANE0 + h0:G_LANE0 + h0 + 1]) + upd[i][:, 0:DV]
        s_ref[b, h1] = s1[i] * jnp.exp(glast[si][:, G_LANE0 + h1:G_LANE0 + h1 + 1]) + upd[i][:, DV:2 * DV]
        on = o[i] * lax.rsqrt(jnp.mean(o[i] * o[i], axis=-1, keepdims=True) + EPS) * gnw_ref[...]
        zb = cat(zbs[h0, zrows[si], :], zbs[h1, zrows[si], :])
        og = on * _silu(zb)
        ext_b[h0, rows[si], :] = og[0:c_len]
        ext_b[h1, rows[si], :] = og[c_len:2 * c_len]


def _ab_post(x, nb, mod_ref, ya, o, w_out_ref):
    out = _dot(ya.astype(BF16), w_out_ref[0:W_A, :]) + _dot(o.astype(BF16), w_out_ref[W_A:W_A + W_B, :])
    return _residual(x, out, mod_ref, nb)


def _to_time_major_rows(x_ref, x_tm, nb, nt):
    for b in range(nb):
        for m in range(D_MODEL // LANE):
            x_tm[m, pl.ds(b, nt, stride=nb), :] = x_ref[b, :, m * LANE:(m + 1) * LANE]
    return jnp.concatenate([x_tm[m] for m in range(D_MODEL // LANE)], axis=1)


def _from_time_major_rows(y, y_ref, y_tm, nb, nt):
    for m in range(D_MODEL // LANE):
        y_tm[m] = y[:, m * LANE:(m + 1) * LANE]
    for b in range(nb):
        for m in range(D_MODEL // LANE):
            y_ref[b, :, m * LANE:(m + 1) * LANE] = y_tm[m, pl.ds(b, nt, stride=nb), :]


def _ab_prompt_kernel(nb, nt, batch_major_in, x_ref, mod_ref, normw_ref, w_in_ref, caw_ref, cab_ref,
                      gw_ref, gxb_ref, gab_ref, apar_ref, cbw_ref, alog_ref, dtb_ref, gnw_ref,
                      w_out_ref, y_ref, ca_out, lru_out, cb_out, s_ref,
                      ext_a, ext_b, za, zbs, lb, gsc, bsc, h_s, *x_tm):
    c = pl.program_id(0)
    rows = nb * nt

    @pl.when(c == 0)
    def _():
        ext_a[0:3 * nb, :] = jnp.zeros((3 * nb, W_A), F32)
        ext_b[:, 0:3 * nb, :] = jnp.zeros((QKV_B // LANE, 3 * nb, LANE), F32)
        h_s[...] = jnp.zeros(h_s.shape, F32)
        s_ref[...] = jnp.zeros(s_ref.shape, F32)

    x = _to_time_major_rows(x_ref, x_tm[0], nb, nt) if batch_major_in else x_ref[...]
    _ab_pre(c == 0, nb, nt, True, x, mod_ref, normw_ref, w_in_ref, caw_ref, cab_ref, gw_ref,
            gxb_ref, gab_ref, apar_ref, cbw_ref, alog_ref, dtb_ref,
            ext_a, ext_b, za, zbs, lb, gsc, bsc, h_s)

    masks = _gdn_block_masks()

    def per_seq_group(i, carry):
        _gdn_block([GDN_SEQS * i + s for s in range(GDN_SEQS)], nb, ext_b, zbs, gsc, bsc, s_ref, gnw_ref, masks)
        return carry

    lax.fori_loop(0, nb // GDN_SEQS, per_seq_group, 0)

    ya = lb[...] * _silu(za[...])
    o = jnp.concatenate([ext_b[h, 3 * nb:3 * nb + rows, :] for h in range(H_B)], axis=1)
    if batch_major_in:
        x = jnp.concatenate([x_tm[0][m] for m in range(D_MODEL // LANE)], axis=1)
    else:
        x = x_ref[...]
    y_ref[...] = _ab_post(x, nb, mod_ref, ya, o, w_out_ref)

    @pl.when(c == pl.num_programs(0) - 1)
    def _():
        ca_out[...] = ext_a[0:3 * nb, :]
        for n in range(QKV_B // LANE):
            cb_out[:, n * LANE:(n + 1) * LANE] = ext_b[n, 0:3 * nb, :]
        lru_out[...] = h_s[...]


def _ab_sample_pre_kernel(nb, x_ref, mod_ref, normw_ref, w_in_ref, caw_ref, cab_ref, gw_ref,
                          gxb_ref, gab_ref, apar_ref, cbw_ref, alog_ref, dtb_ref,
                          ca_in, lru_in, cb_in,
                          qkv_out, g_out, beta_out, zb_out, ya_out, ca_out, lru_out, cb_out,
                          ext_a, ext_b, za, zbs, lb, gsc, bsc, h_s):
    ext_a[0:3 * nb, :] = ca_in[...]
    for n in range(QKV_B // LANE):
        ext_b[n, 0:3 * nb, :] = cb_in[:, n * LANE:(n + 1) * LANE]
    h_s[...] = lru_in[...]
    _ab_pre(False, nb, 1, False, x_ref[...], mod_ref, normw_ref, w_in_ref, caw_ref, cab_ref,
            gw_ref, gxb_ref, gab_ref, apar_ref, cbw_ref, alog_ref, dtb_ref,
            ext_a, ext_b, za, zbs, lb, gsc, bsc, h_s)
    for n in range(QKV_B // LANE):
        qkv_out[:, n * LANE:(n + 1) * LANE] = ext_b[n, 3 * nb:4 * nb, :]
        cb_out[:, n * LANE:(n + 1) * LANE] = ext_b[n, 0:3 * nb, :]
    g_out[...] = gsc[...]
    beta_out[...] = bsc[...]
    for h in range(H_B):
        zb_out[:, h * LANE:(h + 1) * LANE] = zbs[h]
    ya_out[...] = lb[...] * _silu(za[...])
    ca_out[...] = ext_a[0:3 * nb, :]
    lru_out[...] = h_s[...]


def _gdn_step_kernel(bb, layer, n_layers, qkv_ref, g_ref, beta_ref, zb_ref, gnw_ref, s_in, *rest):
    if layer == 0:
        o_ref, s_all = rest
        for l in range(1, n_layers):
            s_all[l] = jnp.zeros(s_all.shape[1:], F32)
        s_out = s_all.at[0]
    else:
        _, o_ref, s_out = rest

    def per_seq(i, carry):
        g_row = g_ref[i]
        beta_row = beta_ref[i]
        heads = range(H_B)
        q = [qkv_ref[i, :, h * LANE:(h + 1) * LANE] for h in heads]
        k = [qkv_ref[i, :, H_B * DK + h * LANE:H_B * DK + (h + 1) * LANE] for h in heads]
        v = [qkv_ref[i, :, 2 * H_B * DK + h * LANE:2 * H_B * DK + (h + 1) * LANE] for h in heads]
        eg = [jnp.exp(g_row[:, G_LANE0 + h:G_LANE0 + h + 1]) for h in heads]
        kcol = [jnp.broadcast_to(k[h], (DK, DK)).T for h in heads]
        qcol = [jnp.broadcast_to(q[h], (DK, DK)).T for h in heads]
        s = [s_in[i, h] for h in heads]
        ks = [jnp.sum(kcol[h] * s[h], axis=0, keepdims=True) for h in heads]
        v_new = [beta_row[:, BETA_LANE0 + h:BETA_LANE0 + h + 1] * (v[h] - eg[h] * ks[h]) for h in heads]
        s_new = [eg[h] * s[h] + kcol[h] * v_new[h] for h in heads]
        for h in heads:
            s_out[i, h] = s_new[h]
        o = [jnp.sum(qcol[h] * s_new[h], axis=0, keepdims=True) for h in heads]
        for h in heads:
            on = o[h] * lax.rsqrt(jnp.mean(o[h] * o[h], axis=-1, keepdims=True) + EPS) * gnw_ref[...]
            zb = zb_ref[i, :, h * LANE:(h + 1) * LANE]
            o_ref[i, :, h * LANE:(h + 1) * LANE] = on * _silu(zb)
        return carry

    lax.fori_loop(0, bb, per_seq, 0)


def _ab_sample_post_kernel(nb, x_ref, mod_ref, ya_ref, o_ref, w_out_ref, y_ref):
    y_ref[...] = _ab_post(x_ref[...], nb, mod_ref, ya_ref[...], o_ref[...], w_out_ref)


def _s5_kernel(nb, nt, has_state, final_norm, batch_major_out, *refs):
    (x_ref, mod_ref, normw_ref, w_in_ref, bre_ref, bim_ref, cre_ref, cim_ref, abr_ref, abi_ref, d_ref,
     gluw_ref, glub_ref, w_out_ref, fnw_ref) = refs[:15]
    refs = refs[15:]
    if has_state:
        sre_in, sim_in = refs[:2]
        refs = refs[2:]
    y_ref, sre_out, sim_out, xs_re, xs_im, y_s, uz_s = refs[:7]
    c = pl.program_id(0)
    rows = nb * nt

    @pl.when(c == 0)
    def _():
        if has_state:
            xs_re[0:nb, :] = sre_in[...]
            xs_im[0:nb, :] = sim_in[...]
        else:
            xs_re[0:nb, :] = jnp.zeros((nb, NS), F32)
            xs_im[0:nb, :] = jnp.zeros((nb, NS), F32)

    hb = _norm_mod(x_ref[...], normw_ref[...], mod_ref, nb).astype(BF16)
    n_half = 2 if nt % 2 == 0 else 1
    for part in range(n_half):
        rs = slice(part * (rows // n_half), (part + 1) * (rows // n_half))
        uz_s[rs, :] = _dot(hb[rs], w_in_ref[...])
    def b_proj(j):
        sl = slice(j * S5_BLOCK_STATE, (j + 1) * S5_BLOCK_STATE)
        uj = uz_s[:, j * LANE:(j + 1) * LANE].astype(BF16)
        xs_re[nb:nb + rows, sl] = _dot(uj, bre_ref[j])
        xs_im[nb:nb + rows, sl] = _dot(uj, bim_ref[j])

    def recurrence(j):
        sl = slice(j * S5_BLOCK_STATE, (j + 1) * S5_BLOCK_STATE)
        ar = jnp.broadcast_to(abr_ref[:, sl], (nb, S5_BLOCK_STATE))
        ai = jnp.broadcast_to(abi_ref[:, sl], (nb, S5_BLOCK_STATE))
        sr, si = xs_re[0:nb, sl], xs_im[0:nb, sl]
        for t in range(nt):
            r = slice(nb + t * nb, 2 * nb + t * nb)
            sr, si = ar * sr - ai * si + xs_re[r, sl], ar * si + ai * sr + xs_im[r, sl]
            xs_re[r, sl] = sr
            xs_im[r, sl] = si
        xs_re[0:nb, sl] = sr
        xs_im[0:nb, sl] = si

    def c_proj(j):
        sl = slice(j * S5_BLOCK_STATE, (j + 1) * S5_BLOCK_STATE)
        cl = slice(j * LANE, (j + 1) * LANE)
        yj = (_dot(xs_re[nb:nb + rows, sl].astype(BF16), cre_ref[j])
              - _dot(xs_im[nb:nb + rows, sl].astype(BF16), cim_ref[j]))
        yj = yj + d_ref[:, cl] * uz_s[:, cl]
        cdf = 0.5 * (1.0 + jnp.tanh(GELU_SQRT_2_OVER_PI * (yj + GELU_CUBIC * (yj * yj * yj))))
        y_s[:, cl] = yj * cdf

    for j in range(S5_LANE_BLOCKS):
        b_proj(j)
    for j in range(S5_LANE_BLOCKS):
        recurrence(j)
        c_proj(j)
    n_parts = 2 if nt % 2 == 0 else 1
    parts = []
    for part in range(n_parts):
        rs = slice(part * (rows // n_parts), (part + 1) * (rows // n_parts))
        y = y_s[rs, :]
        y = y * jax.nn.sigmoid(_dot(y.astype(BF16), gluw_ref[...]) + glub_ref[...])
        y = y * _silu(uz_s[rs, D_MODEL:2 * D_MODEL])
        xn = _residual(x_ref[rs, :], _dot(y.astype(BF16), w_out_ref[...]), mod_ref, nb)
        if final_norm:
            xn = xn * lax.rsqrt(jnp.mean(xn * xn, axis=-1, keepdims=True) + EPS) * fnw_ref[...]
        parts.append(xn)
    xn = jnp.concatenate(parts, axis=0)
    if batch_major_out:
        _from_time_major_rows(xn, y_ref, refs[7], nb, nt)
    else:
        y_ref[...] = xn

    @pl.when(c == pl.num_programs(0) - 1)
    def _():
        sre_out[...] = xs_re[0:nb, :]
        sim_out[...] = xs_im[0:nb, :]


def _s5_prep_kernel(are_ref, aim_ref, ldt_ref, bre_ref, bim_ref, abr_out, abi_out, bbr_out, bbi_out):
    a_re = are_ref[...]
    a_im = aim_ref[...]
    dt = jnp.exp(ldt_ref[...])
    mag = jnp.exp(a_re * dt)
    abr = mag * jnp.cos(a_im * dt)
    abi = mag * jnp.sin(a_im * dt)
    abr_out[...] = abr
    abi_out[...] = abi
    den = a_re * a_re + a_im * a_im
    nr = abr - 1.0
    cr = (nr * a_re + abi * a_im) / den
    ci = (abi * a_re - nr * a_im) / den
    b_re = bre_ref[...]
    b_im = bim_ref[...]
    bbr_out[...] = cr[:, None, :] * b_re - ci[:, None, :] * b_im
    bbi_out[...] = cr[:, None, :] * b_im + ci[:, None, :] * b_re


def _mod_kernel(c_ref, w_ref, b_ref, o_ref):
    c = c_ref[...]
    o_ref[...] = _dot(_silu(c).astype(BF16), w_ref[...].astype(BF16)) + b_ref[...]


def _full(shape):
    n = len(shape)
    return pl.BlockSpec(shape, lambda *_: (0,) * n)


def _const(shape):
    n = len(shape)
    return pl.BlockSpec(shape, lambda *_: (0,) * n, pipeline_mode=pl.Buffered(1))


def _params(n_grid):
    return pltpu.CompilerParams(dimension_semantics=("arbitrary",) * n_grid, vmem_limit_bytes=VMEM_LIMIT)


def _mod_all(c_all, mod_w, mod_b):
    n = c_all.shape[0]
    return pl.pallas_call(
        _mod_kernel,
        grid=(DEPTH, 3),
        in_specs=[pl.BlockSpec((n, D_MODEL), lambda i, j: (0, 0)),
                  pl.BlockSpec((None, D_MODEL, D_MODEL), lambda i, j: (i, 0, j)),
                  pl.BlockSpec((None, 1, D_MODEL), lambda i, j: (i, 0, j))],
        out_specs=pl.BlockSpec((None, n, D_MODEL), lambda i, j: (i, 0, j)),
        out_shape=jax.ShapeDtypeStruct((DEPTH, n, 3 * D_MODEL), F32),
        compiler_params=_params(2),
        name="mod_all",
    )(c_all, mod_w, mod_b.reshape(DEPTH, 1, 3 * D_MODEL))


def _ab_weights(j, w_in_all, ab_out_w, conv_a_w, conv_a_b, lru_gx_w, lru_gx_b, lru_ga_w, lru_ga_b,
                lru_a_param, conv_b_w, gdn_a_log, gdn_dt_bias, gdn_norm_w):
    pad = lambda t: jnp.zeros((1, LANE), F32).at[0, G_LANE0:G_LANE0 + H_B].set(t)
    return dict(
        layer=j, w_in=w_in_all, caw=conv_a_w[j], cab=conv_a_b[j].reshape(1, W_A),
        gw=jnp.concatenate([lru_gx_w[j], lru_ga_w[j]], axis=-1).astype(BF16),
        gxb=lru_gx_b[j].reshape(1, W_A), gab=lru_ga_b[j].reshape(1, W_A),
        apar=lru_a_param[j].reshape(1, W_A), cbw=conv_b_w[j], alog=pad(gdn_a_log[j]),
        dtb=pad(gdn_dt_bias[j]), gnw=gdn_norm_w[j].reshape(1, DV), w_out=ab_out_w[j].astype(BF16))


_AB_PRE_NAMES = ("w_in", "caw", "cab", "gw", "gxb", "gab", "apar", "cbw", "alog", "dtb")


def _ab_pre_specs(w):
    layer = w["layer"]
    specs = []
    for name in _AB_PRE_NAMES:
        if name == "w_in":
            specs.append(pl.BlockSpec((None, N_IN, D_MODEL), lambda *_: (layer, 0, 0),
                                      pipeline_mode=pl.Buffered(1)))
        else:
            specs.append(_const(w[name].shape))
    return specs


def _ab_scratch(nb, rows):
    return [pltpu.VMEM((3 * nb + rows, W_A), F32), pltpu.VMEM((QKV_B // LANE, 3 * nb + rows, LANE), F32),
            pltpu.VMEM((rows, W_A), F32), pltpu.VMEM((H_B, rows, LANE), F32),
            pltpu.VMEM((rows, W_A), F32),
            pltpu.VMEM((rows, LANE), F32), pltpu.VMEM((rows, LANE), F32), pltpu.VMEM((nb, W_A), F32)]


def _ab_prompt(x, mod, normw, w, nb, nt):
    rows = nb * nt
    batch_major = x.ndim == 3
    n_rows = x.shape[0] * x.shape[1] if batch_major else x.shape[0]
    n_chunks = n_rows // rows
    pre = [w[k] for k in _AB_PRE_NAMES]
    if batch_major:
        x_spec = pl.BlockSpec((nb, nt, D_MODEL), lambda c: (0, c, 0))
        extra = [pltpu.VMEM((D_MODEL // LANE, rows, LANE), F32)]
    else:
        x_spec = pl.BlockSpec((rows, D_MODEL), lambda c: (c, 0))
        extra = []
    return pl.pallas_call(
        functools.partial(_ab_prompt_kernel, nb, nt, batch_major),
        grid=(n_chunks,),
        in_specs=[x_spec, _const(mod.shape), _const(normw.shape)]
        + _ab_pre_specs(w) + [_const(w["gnw"].shape), _const(w["w_out"].shape)],
        out_specs=[pl.BlockSpec((rows, D_MODEL), lambda c: (c, 0)), _full((3 * nb, W_A)), _full((nb, W_A)),
                   _full((3 * nb, QKV_B)), _const((nb, H_B, DK, DV))],
        out_shape=[jax.ShapeDtypeStruct((n_rows, D_MODEL), F32), jax.ShapeDtypeStruct((3 * nb, W_A), F32),
                   jax.ShapeDtypeStruct((nb, W_A), F32), jax.ShapeDtypeStruct((3 * nb, QKV_B), F32),
                   jax.ShapeDtypeStruct((nb, H_B, DK, DV), F32)],
        scratch_shapes=_ab_scratch(nb, rows) + extra,
        compiler_params=_params(1),
        name="ab_prompt",
    )(x, mod, normw, *pre, w["gnw"], w["w_out"])


def _ab_sample(x, mod, normw, w, conv_a, lru_h, conv_b, delta_all, layer, delta_new_all):
    nb = x.shape[0]
    pre = [w[k] for k in _AB_PRE_NAMES]
    ins = [x, mod, normw, *pre, conv_a, lru_h, conv_b]
    in_specs = ([_full(x.shape), _full(mod.shape), _full(normw.shape)] + _ab_pre_specs(w)
                + [_full(conv_a.shape), _full(lru_h.shape), _full(conv_b.shape)])
    outs = [(nb, QKV_B), (nb, LANE), (nb, LANE), (nb, W_B), (nb, W_A), (3 * nb, W_A), (nb, W_A),
            (3 * nb, QKV_B)]
    qkv, g, beta, zb, ya, ca_new, lru_new, cb_new = pl.pallas_call(
        functools.partial(_ab_sample_pre_kernel, nb),
        grid=(1,),
        in_specs=in_specs,
        out_specs=[_full(s) for s in outs],
        out_shape=[jax.ShapeDtypeStruct(s, F32) for s in outs],
        scratch_shapes=_ab_scratch(nb, nb),
        compiler_params=_params(1),
        name="ab_sample_pre",
    )(*ins)
    bb = GDN_STEP_SEQS
    row_block = lambda width: pl.BlockSpec((bb, 1, width), lambda i: (i, 0, 0))
    n_layers = delta_all.shape[0]
    state_block = pl.BlockSpec((None, bb, H_B, DK, DV), lambda i: (layer, i, 0, 0, 0))
    step_ins = [qkv.reshape(nb, 1, QKV_B), g.reshape(nb, 1, LANE), beta.reshape(nb, 1, LANE),
                zb.reshape(nb, 1, W_B), w["gnw"], delta_all]
    step_specs = [row_block(QKV_B), row_block(LANE), row_block(LANE), row_block(W_B), _full((1, DV)),
                  state_block]
    if layer == 0:
        out_state_block = pl.BlockSpec((n_layers, bb, H_B, DK, DV), lambda i: (0, i, 0, 0, 0))
        aliases = {}
    else:
        step_ins.append(delta_new_all)
        step_specs.append(pl.BlockSpec(memory_space=pl.ANY))
        out_state_block = state_block
        aliases = {len(step_ins) - 1: 1}
    o, delta_new_all = pl.pallas_call(
        functools.partial(_gdn_step_kernel, bb, layer, n_layers),
        grid=(nb // bb,),
        in_specs=step_specs,
        out_specs=[row_block(W_B), out_state_block],
        out_shape=[jax.ShapeDtypeStruct((nb, 1, W_B), F32), jax.ShapeDtypeStruct(delta_all.shape, F32)],
        input_output_aliases=aliases,
        compiler_params=_params(1),
        name="gdn_step",
    )(*step_ins)
    o = o.reshape(nb, W_B)
    y = pl.pallas_call(
        functools.partial(_ab_sample_post_kernel, nb),
        in_specs=[_full(x.shape), _full(mod.shape), _full(ya.shape), _full(o.shape), _full(w["w_out"].shape)],
        out_specs=_full(x.shape),
        out_shape=jax.ShapeDtypeStruct(x.shape, F32),
        compiler_params=pltpu.CompilerParams(vmem_limit_bytes=VMEM_LIMIT),
        name="ab_sample_post",
    )(x, mod, ya, o, w["w_out"])
    return y, ca_new, lru_new, cb_new, delta_new_all


def _s5_prep(s5_a_re, s5_a_im, s5_log_dt, s5_b_re, s5_b_im):
    n = s5_a_re.shape[0]
    gp = pl.BlockSpec((None, G_C, P_C), lambda i: (i, 0, 0))
    gcp = pl.BlockSpec((None, G_C, CG, P_C), lambda i: (i, 0, 0, 0))
    return pl.pallas_call(
        _s5_prep_kernel,
        grid=(n,),
        in_specs=[gp, gp, pl.BlockSpec((None, G_C, 1), lambda i: (i, 0, 0)), gcp, gcp],
        out_specs=[gp, gp, gcp, gcp],
        out_shape=[jax.ShapeDtypeStruct((n, G_C, P_C), F32)] * 2
        + [jax.ShapeDtypeStruct((n, G_C, CG, P_C), F32)] * 2,
        compiler_params=_params(1),
        name="s5_prep",
    )(s5_a_re, s5_a_im, s5_log_dt.reshape(n, G_C, 1), jnp.swapaxes(s5_b_re, 2, 3), jnp.swapaxes(s5_b_im, 2, 3))


def _block_diag_in(t):
    gl = G_C // S5_LANE_BLOCKS
    t = t.reshape(S5_LANE_BLOCKS, gl, CG, P_C)
    eye = jnp.eye(gl, dtype=t.dtype)
    return jnp.einsum("jgcp,gh->jgchp", t, eye).reshape(S5_LANE_BLOCKS, gl * CG, gl * P_C)


def _block_diag_out(t):
    gl = G_C // S5_LANE_BLOCKS
    t = t.reshape(S5_LANE_BLOCKS, gl, CG, P_C)
    eye = jnp.eye(gl, dtype=t.dtype)
    return jnp.einsum("jgcp,gh->jgphc", t, eye).reshape(S5_LANE_BLOCKS, gl * P_C, gl * CG)


def _s5_layer(x, mod, normw, w, nb, nt, state, final_norm, batch_major_out=False):
    rows = nb * nt
    n_chunks = x.shape[0] // rows
    consts = [mod, normw, w["w_in"], w["bre"], w["bim"], w["cre"], w["cim"], w["abr"], w["abi"], w["d"],
              w["gluw"], w["glub"], w["w_out"], w["fnw"]]
    ins = [x, *consts]
    in_specs = [pl.BlockSpec((rows, D_MODEL), lambda c: (c, 0))] + [_const(t.shape) for t in consts]
    if state is not None:
        ins += list(state)
        in_specs += [_const((nb, NS))] * 2
    if batch_major_out:
        y_spec = pl.BlockSpec((nb, nt, D_MODEL), lambda c: (0, c, 0))
        y_shape = (nb, x.shape[0] // nb, D_MODEL)
        extra = [pltpu.VMEM((D_MODEL // LANE, rows, LANE), F32)]
    else:
        y_spec = pl.BlockSpec((rows, D_MODEL), lambda c: (c, 0))
        y_shape = x.shape
        extra = []
    return pl.pallas_call(
        functools.partial(_s5_kernel, nb, nt, state is not None, final_norm, batch_major_out),
        grid=(n_chunks,),
        in_specs=in_specs,
        out_specs=[y_spec, _full((nb, NS)), _full((nb, NS))],
        out_shape=[jax.ShapeDtypeStruct(y_shape, F32), jax.ShapeDtypeStruct((nb, NS), F32),
                   jax.ShapeDtypeStruct((nb, NS), F32)],
        scratch_shapes=[pltpu.VMEM((nb + rows, NS), F32), pltpu.VMEM((nb + rows, NS), F32),
                        pltpu.VMEM((rows, D_MODEL), F32), pltpu.VMEM((rows, 2 * D_MODEL), F32)] + extra,
        compiler_params=_params(1),
        name="s5_layer",
    )(*ins)


def _to_time_major(t):
    b, l, c = t.shape
    return jnp.swapaxes(t, 0, 1).reshape(l * b, c)


def _from_time_major(t, b):
    lb, c = t.shape
    return jnp.swapaxes(t.reshape(lb // b, b, c), 0, 1)


def kernel(x_prompt, x_sample, c_prompt, c_sample, state_conv_a, state_lru, state_conv_b, state_delta, state_s5_re, state_s5_im, norm_w, mod_w, mod_b, ab_in_w, ab_out_w, conv_a_w, conv_a_b, lru_gx_w, lru_gx_b, lru_ga_w, lru_ga_b, lru_a_param, conv_b_w, gdn_a_log, gdn_dt_bias, gdn_norm_w, c_in_w, c_out_w, s5_a_re, s5_a_im, s5_b_re, s5_b_im, s5_c_re, s5_c_im, s5_d, s5_log_dt, glu_w, glu_b, final_norm_w):
    nbp = x_prompt.shape[0]
    nbs = x_sample.shape[0]
    nt = GDN_BLOCK

    mods = _mod_all(jnp.concatenate([c_prompt, c_sample], axis=0), mod_w, mod_b)
    abr, abi, bbr, bbi = _s5_prep(s5_a_re, s5_a_im, s5_log_dt, s5_b_re, s5_b_im)
    fnw = final_norm_w.reshape(1, D_MODEL)
    w_in_all = jnp.swapaxes(ab_in_w, 1, 2).astype(BF16)

    xp = x_prompt
    xs = x_sample.reshape(nbs, D_MODEL)
    p_states = [[] for _ in range(6)]
    s_states = [[] for _ in range(6)]
    s_delta = None
    for i in range(DEPTH):
        j = i // 2
        normw = norm_w[i].reshape(1, D_MODEL)
        mod_p = mods[i, 0:nbp]
        mod_s = mods[i, nbp:nbp + nbs]
        if i % 2 == 0:
            w = _ab_weights(j, w_in_all, ab_out_w, conv_a_w, conv_a_b, lru_gx_w, lru_gx_b, lru_ga_w, lru_ga_b,
                            lru_a_param, conv_b_w, gdn_a_log, gdn_dt_bias, gdn_norm_w)
            xp, ca, lh, cb, ds = _ab_prompt(xp, mod_p, normw, w, nbp, nt)
            for lst, val in zip(p_states[:4], (_from_time_major(ca, nbp), lh, _from_time_major(cb, nbp), ds)):
                lst.append(val)
            xs, ca, lh, cb, s_delta = _ab_sample(xs, mod_s, normw, w, _to_time_major(state_conv_a[j]),
                                                 state_lru[j], _to_time_major(state_conv_b[j]), state_delta, j,
                                                 s_delta)
            for lst, val in zip(s_states[:3], (_from_time_major(ca, nbs), lh, _from_time_major(cb, nbs))):
                lst.append(val)
        else:
            w = dict(
                w_in=c_in_w[j].astype(BF16), bre=_block_diag_in(bbr[j]).astype(BF16),
                bim=_block_diag_in(bbi[j]).astype(BF16), cre=_block_diag_out(s5_c_re[j]).astype(BF16),
                cim=_block_diag_out(s5_c_im[j]).astype(BF16), abr=abr[j].reshape(1, NS),
                abi=abi[j].reshape(1, NS), d=s5_d[j].reshape(1, D_MODEL), gluw=glu_w[j].astype(BF16),
                glub=glu_b[j].reshape(1, D_MODEL), w_out=c_out_w[j].astype(BF16), fnw=fnw)
            last = i == DEPTH - 1
            xp, sr, si = _s5_layer(xp, mod_p, normw, w, nbp, nt, None, last, batch_major_out=last)
            p_states[4].append(sr.reshape(nbp, G_C, P_C))
            p_states[5].append(si.reshape(nbp, G_C, P_C))
            xs, sr, si = _s5_layer(xs, mod_s, normw, w, nbs, 1,
                                   (state_s5_re[j].reshape(nbs, NS), state_s5_im[j].reshape(nbs, NS)), last)
            s_states[4].append(sr.reshape(nbs, G_C, P_C))
            s_states[5].append(si.reshape(nbs, G_C, P_C))
    y_prompt = xp
    y_sample = xs.reshape(nbs, 1, D_MODEL)
    stack = lambda lists: tuple(jnp.stack(l) for l in lists)
    s_out = stack(s_states[:3]) + (s_delta,) + stack(s_states[4:])
    return (y_prompt, y_sample) + stack(p_states) + s_out
```

```python
import functools

import jax
import jax.numpy as jnp
from jax import lax
from jax.experimental import pallas as pl
from jax.experimental.pallas import tpu as pltpu

F32 = jnp.float32
BF16 = jnp.bfloat16

D_MODEL = 1024
DEPTH = 4
CONV_W = 4
W_A = 1024
H_A = 8
BW_A = 128
LRU_C = 8.0
H_B = 8
DK = 128
DV = 128
W_B = H_B * DV
QKV_B = 3 * W_B
CG = 16
G_C = 64
P_C = 64
NS = G_C * P_C
EPS = 1e-6
GELU_SQRT_2_OVER_PI = 0.7978845608028654
GELU_CUBIC = 0.044715
LANE = 128
GDN_STEP_SEQS = 8
N_IN = 2 * W_A + QKV_B + W_B + 2 * H_B
BETA_LANE0 = LANE - 2 * H_B
G_LANE0 = LANE - H_B
GDN_BLOCK = 64
GDN_SEQS = 4
TILE_COST = 2
LRU_TASK_COST = 3
SLICE_TASK_COST = 1
S5_LANE_BLOCKS = D_MODEL // LANE
S5_BLOCK_STATE = NS // S5_LANE_BLOCKS
VMEM_LIMIT = 60 * 1024 * 1024


def _dot(a, b):
    return jnp.dot(a.astype(BF16), b.astype(BF16), preferred_element_type=F32)


def _dot_nt(a, b):
    return lax.dot_general(a.astype(BF16), b.astype(BF16), (((1,), (1,)), ((), ())),
                           preferred_element_type=F32)


def _silu(x):
    return x * jax.nn.sigmoid(x)


def _softplus(x):
    return jnp.maximum(x, 0.0) + jnp.log1p(jnp.exp(-jnp.abs(x)))


def _norm_mod(x, normw, mod_ref, nb):
    rows = x.shape[0]
    ms = jnp.mean(x * x, axis=-1, keepdims=True)
    y = x * lax.rsqrt(ms + EPS) * normw
    shift = mod_ref[:, 0:D_MODEL]
    scale = mod_ref[:, D_MODEL:2 * D_MODEL]
    y3 = y.reshape(rows // nb, nb, D_MODEL)
    return (y3 * (1.0 + scale)[None] + shift[None]).reshape(rows, D_MODEL)


def _residual(x, out, mod_ref, nb):
    rows = x.shape[0]
    gate = mod_ref[:, 2 * D_MODEL:3 * D_MODEL]
    return x + (out.reshape(rows // nb, nb, D_MODEL) * gate[None]).reshape(rows, D_MODEL)


def _conv_inplace(ext, lane0, w_ref, w_lane0, nb, rows, post):
    sl = slice(lane0, lane0 + LANE)
    wl = slice(w_lane0, w_lane0 + LANE)
    acc = ext[0:rows, sl] * w_ref[0:1, wl]
    for j in range(1, CONV_W):
        acc = acc + ext[j * nb:j * nb + rows, sl] * w_ref[j:j + 1, wl]
    tail = ext[rows:rows + 3 * nb, sl]
    ext[3 * nb:3 * nb + rows, sl] = post(acc)
    ext[0:3 * nb, sl] = tail


def _l2norm(t):
    return t * lax.rsqrt(jnp.sum(t * t, axis=-1, keepdims=True) + EPS)


def _ab_pre(first, nb, nt, reset_first, x, mod_ref, normw_ref, w_in_ref, caw_ref, cab_ref,
            gw_ref, gxb_ref, gab_ref, apar_ref, cbw_ref, alog_ref, dtb_ref,
            ext_a, ext_b, za, zbs, lb, gsc, bsc, h_s):
    rows = nb * nt
    hb = _norm_mod(x, normw_ref[...], mod_ref, nb).astype(BF16)
    tile = 2 * LANE

    def emit_tile(c0):
        part = _dot_nt(hb, w_in_ref[c0:c0 + tile, :])
        for m in range(tile // LANE):
            c = c0 + m * LANE
            piece = part[:, m * LANE:(m + 1) * LANE]
            if c < W_A:
                ext_a[3 * nb:3 * nb + rows, c:c + LANE] = piece
            elif c < 2 * W_A:
                za[:, c - W_A:c - W_A + LANE] = piece
            elif c < 2 * W_A + QKV_B:
                ext_b[(c - 2 * W_A) // LANE, 3 * nb:3 * nb + rows, :] = piece
            else:
                zbs[(c - 2 * W_A - QKV_B) // LANE] = piece

    def lru_block(blk):
        sl = slice(blk * BW_A, (blk + 1) * BW_A)
        _conv_inplace(ext_a, blk * BW_A, caw_ref, blk * BW_A, nb, rows, lambda acc: acc + cab_ref[:, sl])
        xb = ext_a[3 * nb:3 * nb + rows, sl]
        xbb = xb.astype(BF16)
        gates = _dot(xbb, gw_ref[blk])
        gate_x = jax.nn.sigmoid(gates[:, 0:BW_A] + gxb_ref[:, sl])
        gate_a = jax.nn.sigmoid(gates[:, BW_A:2 * BW_A] + gab_ref[:, sl])
        log_a = -LRU_C * gate_a * _softplus(-apar_ref[:, sl])
        a = jnp.exp(log_a)
        m2 = 1.0 - a * a
        mult = jnp.where(m2 > 0.0, m2 * lax.rsqrt(m2), 0.0)
        if reset_first:
            row = lax.broadcasted_iota(jnp.int32, (rows, BW_A), 0)
            mult = jnp.where(jnp.logical_and(first, row < nb), 1.0, mult)
        bval = mult * gate_x * xb
        h = h_s[:, sl]
        for t in range(nt):
            r = slice(t * nb, (t + 1) * nb)
            h = a[r] * h + bval[r]
            lb[r, sl] = h
        h_s[:, sl] = h

    def qkv_slice(n):
        if n < H_B:
            post = lambda acc: _l2norm(_silu(acc)) * (DK ** -0.5)
        elif n < 2 * H_B:
            post = lambda acc: _l2norm(_silu(acc))
        else:
            post = _silu
        _conv_inplace(ext_b.at[n], 0, cbw_ref, n * LANE, nb, rows, post)

    col_xa, col_za, col_qkv, col_zb = 0, W_A, 2 * W_A, 2 * W_A + QKV_B
    tiles = ([col_xa + i * tile for i in range(W_A // tile)] + [col_qkv + i * tile for i in range(QKV_B // tile)]
             + [col_za + i * tile for i in range(W_A // tile)] + [col_zb + i * tile for i in range(W_B // tile)])
    n_xa = W_A // tile
    tasks = []
    for blk in range(H_A):
        tasks.append((n_xa - 1, LRU_TASK_COST, functools.partial(lru_block, blk)))
        for n in range(3 * blk, 3 * blk + 3):
            tasks.append((n_xa + n // 2, SLICE_TASK_COST, functools.partial(qkv_slice, n)))
    matmul_cost = vector_cost = 0
    for ti, c0 in enumerate(tiles):
        emit_tile(c0)
        matmul_cost += TILE_COST
        while tasks and tasks[0][0] <= ti and vector_cost + tasks[0][1] <= matmul_cost:
            _, cost, task = tasks.pop(0)
            task()
            vector_cost += cost
    ba = _dot_nt(hb, w_in_ref[N_IN - LANE:N_IN, :])
    bsc[...] = jax.nn.sigmoid(ba)
    g = -jnp.exp(alog_ref[...]) * _softplus(ba + dtb_ref[...])
    acc = g[0:nb]
    gsc[0:nb, :] = acc
    for t in range(1, nt):
        acc = acc + g[t * nb:(t + 1) * nb]
        gsc[t * nb:(t + 1) * nb, :] = acc
    for _, _, task in tasks:
        task()


def _gdn_block_masks():
    n = 2 * GDN_BLOCK
    ri = lax.broadcasted_iota(jnp.int32, (n, n), 0)
    ci = lax.broadcasted_iota(jnp.int32, (n, n), 1)
    same = (ri >= GDN_BLOCK) == (ci >= GDN_BLOCK)
    tri = jnp.where(jnp.logical_and(same, ri >= ci), 1.0, 0.0).astype(F32)
    strict = jnp.where(jnp.logical_and(same, ri > ci), 1.0, 0.0).astype(F32)
    eye = jnp.where(ri == ci, 1.0, 0.0).astype(F32)
    levels = []
    for l in range(6):
        rb = ri >> l
        sub = jnp.logical_and((rb & 1) == 1, (ci >> l) == rb - 1)
        levels.append(jnp.where(jnp.logical_and(same, sub), 1.0, 0.0).astype(F32))
    return tri, strict, eye, levels


def _gdn_block(seqs, nb, ext_b, zbs, gsc, bsc, s_ref, gnw_ref, masks):
    c_len = GDN_BLOCK
    top = lax.broadcasted_iota(jnp.int32, (2 * c_len, LANE), 0) < c_len
    tri, strict, eye, levels = masks
    chains = [(si, p) for si in range(len(seqs)) for p in range(H_B // 2)]
    n = range(len(chains))

    def cat(a0, a1):
        return jnp.concatenate([a0, a1], axis=0)

    rows = [pl.ds(3 * nb + b, c_len, stride=nb) for b in seqs]
    zrows = [pl.ds(b, c_len, stride=nb) for b in seqs]

    def head_pair(base, si, p):
        return cat(ext_b[base + 2 * p, rows[si], :], ext_b[base + 2 * p + 1, rows[si], :])

    def col_pair(t, lane0, p):
        return cat(t[:, lane0 + 2 * p:lane0 + 2 * p + 1], t[:, lane0 + 2 * p + 1:lane0 + 2 * p + 2])

    gcb = [gsc[zr, :] for zr in zrows]
    betab = [bsc[zr, :] for zr in zrows]
    glast = [gsc[pl.ds((c_len - 1) * nb + b, 1), :] for b in seqs]
    q = [head_pair(0, si, p) for si, p in chains]
    k = [head_pair(H_B, si, p) for si, p in chains]
    v = [head_pair(2 * H_B, si, p) for si, p in chains]
    c = [col_pair(gcb[si], G_LANE0, p) for si, p in chains]
    bcol = [col_pair(betab[si], BETA_LANE0, p) for si, p in chains]
    gl = [col_pair(jnp.broadcast_to(glast[si], (c_len, LANE)), G_LANE0, p) for si, p in chains]
    gct = [cat(gcb[s], gcb[s + 1]).T for s in range(0, len(seqs), 2)]
    decay = []
    for i, (si, p) in enumerate(chains):
        half = slice((si % 2) * c_len, (si % 2 + 1) * c_len)
        g0 = G_LANE0 + 2 * p
        crow = jnp.concatenate([gct[si // 2][g0:g0 + 1, half], gct[si // 2][g0 + 1:g0 + 2, half]], axis=1)
        cm = jnp.broadcast_to(c[i], (2 * c_len, 2 * c_len))
        decay.append(jnp.exp((cm - jnp.broadcast_to(crow, (2 * c_len, 2 * c_len))) * tri) * tri)
    kb = [k[i] * bcol[i] for i in n]
    a_mat = [_dot_nt(kb[i], k[i]) * decay[i] * strict for i in n]
    qk = [_dot_nt(q[i], k[i]) * decay[i] for i in n]
    x = [eye - a_mat[i] * levels[0] for i in n]
    for l in range(1, 6):
        t = [_dot(a_mat[i] * levels[l], x[i]) for i in n]
        x = [x[i] - _dot(x[i], t[i]) for i in n]
    sol = [_dot(x[i], jnp.concatenate([v[i] * bcol[i], kb[i] * jnp.exp(c[i])], axis=1)) for i in n]
    s0 = [s_ref[seqs[si], 2 * p] for si, p in chains]
    s1 = [s_ref[seqs[si], 2 * p + 1] for si, p in chains]

    def per_head(lhs, i):
        return cat(_dot(lhs[0:c_len], s0[i]), _dot(lhs[c_len:2 * c_len], s1[i]))

    ws = [per_head(sol[i][:, DV:2 * DV], i) for i in n]
    qs = [per_head(q[i] * jnp.exp(c[i]), i) for i in n]
    v_new = [sol[i][:, 0:DV] - ws[i] for i in n]
    o = [qs[i] + _dot(qk[i], v_new[i]) for i in n]
    upd = []
    for i in n:
        kdec = k[i] * jnp.exp(gl[i] - c[i])
        vblk = jnp.concatenate([jnp.where(top, v_new[i], 0.0), jnp.where(top, 0.0, v_new[i])], axis=1)
        upd.append(_dot(kdec.T, vblk))
    for i, (si, p) in enumerate(chains):
        b = seqs[si]
        h0, h1 = 2 * p, 2 * p + 1
        s_ref[b, h0] = s0[i] * jnp.exp(glast[si][:, G_LANE0 + h0:G_LANE0 + h0 + 1]) + upd[i][:, 0:DV]
        s_ref[b, h1] = s1[i] * jnp.exp(glast[si][:, G_LANE0 + h1:G_LANE0 + h1 + 1]) + upd[i][:, DV:2 * DV]
        on = o[i] * lax.rsqrt(jnp.mean(o[i] * o[i], axis=-1, keepdims=True) + EPS) * gnw_ref[...]
        zb = cat(zbs[h0, zrows[si], :], zbs[h1, zrows[si], :])
        og = on * _silu(zb)
        ext_b[h0, rows[si], :] = og[0:c_len]
        ext_b[h1, rows[si], :] = og[c_len:2 * c_len]


def _ab_post(x, nb, mod_ref, ya, o, w_out_ref):
    out = _dot(ya.astype(BF16), w_out_ref[0:W_A, :]) + _dot(o.astype(BF16), w_out_ref[W_A:W_A + W_B, :])
    return _residual(x, out, mod_ref, nb)


def _to_time_major_rows(x_ref, x_tm, nb, nt):
    for b in range(nb):
        for m in range(D_MODEL // LANE):
            x_tm[m, pl.ds(b, nt, stride=nb), :] = x_ref[b, :, m * LANE:(m + 1) * LANE]
    return jnp.concatenate([x_tm[m] for m in range(D_MODEL // LANE)], axis=1)


def _from_time_major_rows(y, y_ref, y_tm, nb, nt):
    for m in range(D_MODEL // LANE):
        y_tm[m] = y[:, m * LANE:(m + 1) * LANE]
    for b in range(nb):
        for m in range(D_MODEL // LANE):
            y_ref[b, :, m * LANE:(m + 1) * LANE] = y_tm[m, pl.ds(b, nt, stride=nb), :]


def _ab_prompt_kernel(nb, nt, batch_major_in, x_ref, mod_ref, normw_ref, w_in_ref, caw_ref, cab_ref,
                      gw_ref, gxb_ref, gab_ref, apar_ref, cbw_ref, alog_ref, dtb_ref, gnw_ref,
                      w_out_ref, y_ref, ca_out, lru_out, cb_out, s_ref,
                      ext_a, ext_b, za, zbs, lb, gsc, bsc, h_s, *x_tm):
    c = pl.program_id(0)
    rows = nb * nt

    @pl.when(c == 0)
    def _():
        ext_a[0:3 * nb, :] = jnp.zeros((3 * nb, W_A), F32)
        ext_b[:, 0:3 * nb, :] = jnp.zeros((QKV_B // LANE, 3 * nb, LANE), F32)
        h_s[...] = jnp.zeros(h_s.shape, F32)
        s_ref[...] = jnp.zeros(s_ref.shape, F32)

    x = _to_time_major_rows(x_ref, x_tm[0], nb, nt) if batch_major_in else x_ref[...]
    _ab_pre(c == 0, nb, nt, True, x, mod_ref, normw_ref, w_in_ref, caw_ref, cab_ref, gw_ref,
            gxb_ref, gab_ref, apar_ref, cbw_ref, alog_ref, dtb_ref,
            ext_a, ext_b, za, zbs, lb, gsc, bsc, h_s)

    masks = _gdn_block_masks()

    def per_seq_group(i, carry):
        _gdn_block([GDN_SEQS * i + s for s in range(GDN_SEQS)], nb, ext_b, zbs, gsc, bsc, s_ref, gnw_ref, masks)
        return carry

    lax.fori_loop(0, nb // GDN_SEQS, per_seq_group, 0)

    half = rows // 2
    for part in range(2):
        rs = slice(part * half, (part + 1) * half)
        ya = lb[rs, :] * _silu(za[rs, :])
        o = jnp.concatenate([ext_b[h, 3 * nb + part * half:3 * nb + (part + 1) * half, :] for h in range(H_B)],
                            axis=1)
        if batch_major_in:
            x = jnp.concatenate([x_tm[0][m, rs, :] for m in range(D_MODEL // LANE)], axis=1)
        else:
            x = x_ref[rs, :]
        y_ref[rs, :] = _ab_post(x, nb, mod_ref, ya, o, w_out_ref)

    @pl.when(c == pl.num_programs(0) - 1)
    def _():
        ca_out[...] = ext_a[0:3 * nb, :]
        for n in range(QKV_B // LANE):
            cb_out[:, n * LANE:(n + 1) * LANE] = ext_b[n, 0:3 * nb, :]
        lru_out[...] = h_s[...]


def _ab_sample_pre_kernel(nb, x_ref, mod_ref, normw_ref, w_in_ref, caw_ref, cab_ref, gw_ref,
                          gxb_ref, gab_ref, apar_ref, cbw_ref, alog_ref, dtb_ref,
                          ca_in, lru_in, cb_in,
                          qkv_out, g_out, beta_out, zb_out, ya_out, ca_out, lru_out, cb_out,
                          ext_a, ext_b, za, zbs, lb, gsc, bsc, h_s):
    ext_a[0:3 * nb, :] = ca_in[...]
    for n in range(QKV_B // LANE):
        ext_b[n, 0:3 * nb, :] = cb_in[:, n * LANE:(n + 1) * LANE]
    h_s[...] = lru_in[...]
    _ab_pre(False, nb, 1, False, x_ref[...], mod_ref, normw_ref, w_in_ref, caw_ref, cab_ref,
            gw_ref, gxb_ref, gab_ref, apar_ref, cbw_ref, alog_ref, dtb_ref,
            ext_a, ext_b, za, zbs, lb, gsc, bsc, h_s)
    for n in range(QKV_B // LANE):
        qkv_out[:, n * LANE:(n + 1) * LANE] = ext_b[n, 3 * nb:4 * nb, :]
        cb_out[:, n * LANE:(n + 1) * LANE] = ext_b[n, 0:3 * nb, :]
    g_out[...] = gsc[...]
    beta_out[...] = bsc[...]
    for h in range(H_B):
        zb_out[:, h * LANE:(h + 1) * LANE] = zbs[h]
    ya_out[...] = lb[...] * _silu(za[...])
    ca_out[...] = ext_a[0:3 * nb, :]
    lru_out[...] = h_s[...]


def _gdn_step_kernel(bb, layer, n_layers, qkv_ref, g_ref, beta_ref, zb_ref, gnw_ref, s_in, *rest):
    if layer == 0:
        o_ref, s_all = rest
        for l in range(1, n_layers):
            s_all[l] = jnp.zeros(s_all.shape[1:], F32)
        s_out = s_all.at[0]
    else:
        _, o_ref, s_out = rest

    def per_seq(i, carry):
        g_row = g_ref[i]
        beta_row = beta_ref[i]
        heads = range(H_B)
        q = [qkv_ref[i, :, h * LANE:(h + 1) * LANE] for h in heads]
        k = [qkv_ref[i, :, H_B * DK + h * LANE:H_B * DK + (h + 1) * LANE] for h in heads]
        v = [qkv_ref[i, :, 2 * H_B * DK + h * LANE:2 * H_B * DK + (h + 1) * LANE] for h in heads]
        eg = [jnp.exp(g_row[:, G_LANE0 + h:G_LANE0 + h + 1]) for h in heads]
        kcol = [jnp.broadcast_to(k[h], (DK, DK)).T for h in heads]
        qcol = [jnp.broadcast_to(q[h], (DK, DK)).T for h in heads]
        s = [s_in[i, h] for h in heads]
        ks = [jnp.sum(kcol[h] * s[h], axis=0, keepdims=True) for h in heads]
        v_new = [beta_row[:, BETA_LANE0 + h:BETA_LANE0 + h + 1] * (v[h] - eg[h] * ks[h]) for h in heads]
        s_new = [eg[h] * s[h] + kcol[h] * v_new[h] for h in heads]
        for h in heads:
            s_out[i, h] = s_new[h]
        o = [jnp.sum(qcol[h] * s_new[h], axis=0, keepdims=True) for h in heads]
        for h in heads:
            on = o[h] * lax.rsqrt(jnp.mean(o[h] * o[h], axis=-1, keepdims=True) + EPS) * gnw_ref[...]
            zb = zb_ref[i, :, h * LANE:(h + 1) * LANE]
            o_ref[i, :, h * LANE:(h + 1) * LANE] = on * _silu(zb)
        return carry

    lax.fori_loop(0, bb, per_seq, 0)


def _ab_sample_post_kernel(nb, x_ref, mod_ref, ya_ref, o_ref, w_out_ref, y_ref):
    y_ref[...] = _ab_post(x_ref[...], nb, mod_ref, ya_ref[...], o_ref[...], w_out_ref)


def _s5_kernel(nb, nt, has_state, final_norm, batch_major_out, *refs):
    (x_ref, mod_ref, normw_ref, w_in_ref, bre_ref, bim_ref, cre_ref, cim_ref, abr_ref, abi_ref, d_ref,
     gluw_ref, glub_ref, w_out_ref, fnw_ref) = refs[:15]
    refs = refs[15:]
    if has_state:
        sre_in, sim_in = refs[:2]
        refs = refs[2:]
    y_ref, sre_out, sim_out, xs_re, xs_im, y_s, uz_s = refs[:7]
    c = pl.program_id(0)
    rows = nb * nt

    @pl.when(c == 0)
    def _():
        if has_state:
            xs_re[0:nb, :] = sre_in[...]
            xs_im[0:nb, :] = sim_in[...]
        else:
            xs_re[0:nb, :] = jnp.zeros((nb, NS), F32)
            xs_im[0:nb, :] = jnp.zeros((nb, NS), F32)

    hb = _norm_mod(x_ref[...], normw_ref[...], mod_ref, nb).astype(BF16)
    n_half = 2 if nt % 2 == 0 else 1
    for part in range(n_half):
        rs = slice(part * (rows // n_half), (part + 1) * (rows // n_half))
        uz_s[rs, :] = _dot(hb[rs], w_in_ref[...])
    def b_proj(j):
        sl = slice(j * S5_BLOCK_STATE, (j + 1) * S5_BLOCK_STATE)
        uj = uz_s[:, j * LANE:(j + 1) * LANE].astype(BF16)
        xs_re[nb:nb + rows, sl] = _dot(uj, bre_ref[j])
        xs_im[nb:nb + rows, sl] = _dot(uj, bim_ref[j])

    def recurrence(j):
        sl = slice(j * S5_BLOCK_STATE, (j + 1) * S5_BLOCK_STATE)
        ar = jnp.broadcast_to(abr_ref[:, sl], (nb, S5_BLOCK_STATE))
        ai = jnp.broadcast_to(abi_ref[:, sl], (nb, S5_BLOCK_STATE))
        sr, si = xs_re[0:nb, sl], xs_im[0:nb, sl]
        for t in range(nt):
            r = slice(nb + t * nb, 2 * nb + t * nb)
            sr, si = ar * sr - ai * si + xs_re[r, sl], ar * si + ai * sr + xs_im[r, sl]
            xs_re[r, sl] = sr
            xs_im[r, sl] = si
        xs_re[0:nb, sl] = sr
        xs_im[0:nb, sl] = si

    def c_proj(j):
        sl = slice(j * S5_BLOCK_STATE, (j + 1) * S5_BLOCK_STATE)
        cl = slice(j * LANE, (j + 1) * LANE)
        yj = (_dot(xs_re[nb:nb + rows, sl].astype(BF16), cre_ref[j])
              - _dot(xs_im[nb:nb + rows, sl].astype(BF16), cim_ref[j]))
        yj = yj + d_ref[:, cl] * uz_s[:, cl]
        cdf = 0.5 * (1.0 + jnp.tanh(GELU_SQRT_2_OVER_PI * (yj + GELU_CUBIC * (yj * yj * yj))))
        y_s[:, cl] = yj * cdf

    for j in range(S5_LANE_BLOCKS):
        b_proj(j)
    for j in range(S5_LANE_BLOCKS):
        recurrence(j)
        c_proj(j)
    n_parts = 2 if nt % 2 == 0 else 1
    parts = []
    for part in range(n_parts):
        rs = slice(part * (rows // n_parts), (part + 1) * (rows // n_parts))
        y = y_s[rs, :]
        y = y * jax.nn.sigmoid(_dot(y.astype(BF16), gluw_ref[...]) + glub_ref[...])
        y = y * _silu(uz_s[rs, D_MODEL:2 * D_MODEL])
        xn = _residual(x_ref[rs, :], _dot(y.astype(BF16), w_out_ref[...]), mod_ref, nb)
        if final_norm:
            xn = xn * lax.rsqrt(jnp.mean(xn * xn, axis=-1, keepdims=True) + EPS) * fnw_ref[...]
        parts.append(xn)
    xn = jnp.concatenate(parts, axis=0)
    if batch_major_out:
        _from_time_major_rows(xn, y_ref, refs[7], nb, nt)
    else:
        y_ref[...] = xn

    @pl.when(c == pl.num_programs(0) - 1)
    def _():
        sre_out[...] = xs_re[0:nb, :]
        sim_out[...] = xs_im[0:nb, :]


def _s5_prep_kernel(are_ref, aim_ref, ldt_ref, bre_ref, bim_ref, abr_out, abi_out, bbr_out, bbi_out):
    a_re = are_ref[...]
    a_im = aim_ref[...]
    dt = jnp.exp(ldt_ref[...])
    mag = jnp.exp(a_re * dt)
    abr = mag * jnp.cos(a_im * dt)
    abi = mag * jnp.sin(a_im * dt)
    abr_out[...] = abr
    abi_out[...] = abi
    den = a_re * a_re + a_im * a_im
    nr = abr - 1.0
    cr = (nr * a_re + abi * a_im) / den
    ci = (abi * a_re - nr * a_im) / den
    b_re = bre_ref[...]
    b_im = bim_ref[...]
    bbr_out[...] = cr[:, None, :] * b_re - ci[:, None, :] * b_im
    bbi_out[...] = cr[:, None, :] * b_im + ci[:, None, :] * b_re


def _mod_kernel(c_ref, w_ref, b_ref, o_ref):
    c = c_ref[...]
    o_ref[...] = _dot(_silu(c).astype(BF16), w_ref[...].astype(BF16)) + b_ref[...]


def _full(shape):
    n = len(shape)
    return pl.BlockSpec(shape, lambda *_: (0,) * n)


def _const(shape):
    n = len(shape)
    return pl.BlockSpec(shape, lambda *_: (0,) * n, pipeline_mode=pl.Buffered(1))


def _params(n_grid):
    return pltpu.CompilerParams(dimension_semantics=("arbitrary",) * n_grid, vmem_limit_bytes=VMEM_LIMIT)


def _mod_all(c_all, mod_w, mod_b):
    n = c_all.shape[0]
    return pl.pallas_call(
        _mod_kernel,
        grid=(DEPTH, 3),
        in_specs=[pl.BlockSpec((n, D_MODEL), lambda i, j: (0, 0)),
                  pl.BlockSpec((None, D_MODEL, D_MODEL), lambda i, j: (i, 0, j)),
                  pl.BlockSpec((None, 1, D_MODEL), lambda i, j: (i, 0, j))],
        out_specs=pl.BlockSpec((None, n, D_MODEL), lambda i, j: (i, 0, j)),
        out_shape=jax.ShapeDtypeStruct((DEPTH, n, 3 * D_MODEL), F32),
        compiler_params=_params(2),
        name="mod_all",
    )(c_all, mod_w, mod_b.reshape(DEPTH, 1, 3 * D_MODEL))


def _ab_weights(j, w_in_all, ab_out_w, conv_a_w, conv_a_b, lru_gx_w, lru_gx_b, lru_ga_w, lru_ga_b,
                lru_a_param, conv_b_w, gdn_a_log, gdn_dt_bias, gdn_norm_w):
    pad = lambda t: jnp.zeros((1, LANE), F32).at[0, G_LANE0:G_LANE0 + H_B].set(t)
    return dict(
        layer=j, w_in=w_in_all, caw=conv_a_w[j], cab=conv_a_b[j].reshape(1, W_A),
        gw=jnp.concatenate([lru_gx_w[j], lru_ga_w[j]], axis=-1).astype(BF16),
        gxb=lru_gx_b[j].reshape(1, W_A), gab=lru_ga_b[j].reshape(1, W_A),
        apar=lru_a_param[j].reshape(1, W_A), cbw=conv_b_w[j], alog=pad(gdn_a_log[j]),
        dtb=pad(gdn_dt_bias[j]), gnw=gdn_norm_w[j].reshape(1, DV), w_out=ab_out_w[j].astype(BF16))


_AB_PRE_NAMES = ("w_in", "caw", "cab", "gw", "gxb", "gab", "apar", "cbw", "alog", "dtb")


def _ab_pre_specs(w):
    layer = w["layer"]
    specs = []
    for name in _AB_PRE_NAMES:
        if name == "w_in":
            specs.append(pl.BlockSpec((None, N_IN, D_MODEL), lambda *_: (layer, 0, 0),
                                      pipeline_mode=pl.Buffered(1)))
        else:
            specs.append(_const(w[name].shape))
    return specs


def _ab_scratch(nb, rows):
    return [pltpu.VMEM((3 * nb + rows, W_A), F32), pltpu.VMEM((QKV_B // LANE, 3 * nb + rows, LANE), F32),
            pltpu.VMEM((rows, W_A), F32), pltpu.VMEM((H_B, rows, LANE), F32),
            pltpu.VMEM((rows, W_A), F32),
            pltpu.VMEM((rows, LANE), F32), pltpu.VMEM((rows, LANE), F32), pltpu.VMEM((nb, W_A), F32)]


def _ab_prompt(x, mod, normw, w, nb, nt):
    rows = nb * nt
    batch_major = x.ndim == 3
    n_rows = x.shape[0] * x.shape[1] if batch_major else x.shape[0]
    n_chunks = n_rows // rows
    pre = [w[k] for k in _AB_PRE_NAMES]
    if batch_major:
        x_spec = pl.BlockSpec((nb, nt, D_MODEL), lambda c: (0, c, 0))
        extra = [pltpu.VMEM((D_MODEL // LANE, rows, LANE), F32)]
    else:
        x_spec = pl.BlockSpec((rows, D_MODEL), lambda c: (c, 0))
        extra = []
    return pl.pallas_call(
        functools.partial(_ab_prompt_kernel, nb, nt, batch_major),
        grid=(n_chunks,),
        in_specs=[x_spec, _const(mod.shape), _const(normw.shape)]
        + _ab_pre_specs(w) + [_const(w["gnw"].shape), _const(w["w_out"].shape)],
        out_specs=[pl.BlockSpec((rows, D_MODEL), lambda c: (c, 0)), _full((3 * nb, W_A)), _full((nb, W_A)),
                   _full((3 * nb, QKV_B)), _const((nb, H_B, DK, DV))],
        out_shape=[jax.ShapeDtypeStruct((n_rows, D_MODEL), F32), jax.ShapeDtypeStruct((3 * nb, W_A), F32),
                   jax.ShapeDtypeStruct((nb, W_A), F32), jax.ShapeDtypeStruct((3 * nb, QKV_B), F32),
                   jax.ShapeDtypeStruct((nb, H_B, DK, DV), F32)],
        scratch_shapes=_ab_scratch(nb, rows) + extra,
        compiler_params=_params(1),
        name="ab_prompt",
    )(x, mod, normw, *pre, w["gnw"], w["w_out"])


def _ab_sample(x, mod, normw, w, conv_a, lru_h, conv_b, delta_all, layer, delta_new_all):
    nb = x.shape[0]
    pre = [w[k] for k in _AB_PRE_NAMES]
    ins = [x, mod, normw, *pre, conv_a, lru_h, conv_b]
    in_specs = ([_full(x.shape), _full(mod.shape), _full(normw.shape)] + _ab_pre_specs(w)
                + [_full(conv_a.shape), _full(lru_h.shape), _full(conv_b.shape)])
    outs = [(nb, QKV_B), (nb, LANE), (nb, LANE), (nb, W_B), (nb, W_A), (3 * nb, W_A), (nb, W_A),
            (3 * nb, QKV_B)]
    qkv, g, beta, zb, ya, ca_new, lru_new, cb_new = pl.pallas_call(
        functools.partial(_ab_sample_pre_kernel, nb),
        grid=(1,),
        in_specs=in_specs,
        out_specs=[_full(s) for s in outs],
        out_shape=[jax.ShapeDtypeStruct(s, F32) for s in outs],
        scratch_shapes=_ab_scratch(nb, nb),
        compiler_params=_params(1),
        name="ab_sample_pre",
    )(*ins)
    bb = GDN_STEP_SEQS
    row_block = lambda width: pl.BlockSpec((bb, 1, width), lambda i: (i, 0, 0))
    n_layers = delta_all.shape[0]
    state_block = pl.BlockSpec((None, bb, H_B, DK, DV), lambda i: (layer, i, 0, 0, 0))
    step_ins = [qkv.reshape(nb, 1, QKV_B), g.reshape(nb, 1, LANE), beta.reshape(nb, 1, LANE),
                zb.reshape(nb, 1, W_B), w["gnw"], delta_all]
    step_specs = [row_block(QKV_B), row_block(LANE), row_block(LANE), row_block(W_B), _full((1, DV)),
                  state_block]
    if layer == 0:
        out_state_block = pl.BlockSpec((n_layers, bb, H_B, DK, DV), lambda i: (0, i, 0, 0, 0))
        aliases = {}
    else:
        step_ins.append(delta_new_all)
        step_specs.append(pl.BlockSpec(memory_space=pl.ANY))
        out_state_block = state_block
        aliases = {len(step_ins) - 1: 1}
    o, delta_new_all = pl.pallas_call(
        functools.partial(_gdn_step_kernel, bb, layer, n_layers),
        grid=(nb // bb,),
        in_specs=step_specs,
        out_specs=[row_block(W_B), out_state_block],
        out_shape=[jax.ShapeDtypeStruct((nb, 1, W_B), F32), jax.ShapeDtypeStruct(delta_all.shape, F32)],
        input_output_aliases=aliases,
        compiler_params=_params(1),
        name="gdn_step",
    )(*step_ins)
    o = o.reshape(nb, W_B)
    y = pl.pallas_call(
        functools.partial(_ab_sample_post_kernel, nb),
        in_specs=[_full(x.shape), _full(mod.shape), _full(ya.shape), _full(o.shape), _full(w["w_out"].shape)],
        out_specs=_full(x.shape),
        out_shape=jax.ShapeDtypeStruct(x.shape, F32),
        compiler_params=pltpu.CompilerParams(vmem_limit_bytes=VMEM_LIMIT),
        name="ab_sample_post",
    )(x, mod, ya, o, w["w_out"])
    return y, ca_new, lru_new, cb_new, delta_new_all


def _s5_prep(s5_a_re, s5_a_im, s5_log_dt, s5_b_re, s5_b_im):
    n = s5_a_re.shape[0]
    gp = pl.BlockSpec((None, G_C, P_C), lambda i: (i, 0, 0))
    gcp = pl.BlockSpec((None, G_C, CG, P_C), lambda i: (i, 0, 0, 0))
    return pl.pallas_call(
        _s5_prep_kernel,
        grid=(n,),
        in_specs=[gp, gp, pl.BlockSpec((None, G_C, 1), lambda i: (i, 0, 0)), gcp, gcp],
        out_specs=[gp, gp, gcp, gcp],
        out_shape=[jax.ShapeDtypeStruct((n, G_C, P_C), F32)] * 2
        + [jax.ShapeDtypeStruct((n, G_C, CG, P_C), F32)] * 2,
        compiler_params=_params(1),
        name="s5_prep",
    )(s5_a_re, s5_a_im, s5_log_dt.reshape(n, G_C, 1), jnp.swapaxes(s5_b_re, 2, 3), jnp.swapaxes(s5_b_im, 2, 3))


def _block_diag_in(t):
    gl = G_C // S5_LANE_BLOCKS
    t = t.reshape(S5_LANE_BLOCKS, gl, CG, P_C)
    eye = jnp.eye(gl, dtype=t.dtype)
    return jnp.einsum("jgcp,gh->jgchp", t, eye).reshape(S5_LANE_BLOCKS, gl * CG, gl * P_C)


def _block_diag_out(t):
    gl = G_C // S5_LANE_BLOCKS
    t = t.reshape(S5_LANE_BLOCKS, gl, CG, P_C)
    eye = jnp.eye(gl, dtype=t.dtype)
    return jnp.einsum("jgcp,gh->jgphc", t, eye).reshape(S5_LANE_BLOCKS, gl * P_C, gl * CG)


def _s5_layer(x, mod, normw, w, nb, nt, state, final_norm, batch_major_out=False):
    rows = nb * nt
    n_chunks = x.shape[0] // rows
    consts = [mod, normw, w["w_in"], w["bre"], w["bim"], w["cre"], w["cim"], w["abr"], w["abi"], w["d"],
              w["gluw"], w["glub"], w["w_out"], w["fnw"]]
    ins = [x, *consts]
    in_specs = [pl.BlockSpec((rows, D_MODEL), lambda c: (c, 0))] + [_const(t.shape) for t in consts]
    if state is not None:
        ins += list(state)
        in_specs += [_const((nb, NS))] * 2
    if batch_major_out:
        y_spec = pl.BlockSpec((nb, nt, D_MODEL), lambda c: (0, c, 0))
        y_shape = (nb, x.shape[0] // nb, D_MODEL)
        extra = [pltpu.VMEM((D_MODEL // LANE, rows, LANE), F32)]
    else:
        y_spec = pl.BlockSpec((rows, D_MODEL), lambda c: (c, 0))
        y_shape = x.shape
        extra = []
    return pl.pallas_call(
        functools.partial(_s5_kernel, nb, nt, state is not None, final_norm, batch_major_out),
        grid=(n_chunks,),
        in_specs=in_specs,
        out_specs=[y_spec, _full((nb, NS)), _full((nb, NS))],
        out_shape=[jax.ShapeDtypeStruct(y_shape, F32), jax.ShapeDtypeStruct((nb, NS), F32),
                   jax.ShapeDtypeStruct((nb, NS), F32)],
        scratch_shapes=[pltpu.VMEM((nb + rows, NS), F32), pltpu.VMEM((nb + rows, NS), F32),
                        pltpu.VMEM((rows, D_MODEL), F32), pltpu.VMEM((rows, 2 * D_MODEL), F32)] + extra,
        compiler_params=_params(1),
        name="s5_layer",
    )(*ins)


def _to_time_major(t):
    b, l, c = t.shape
    return jnp.swapaxes(t, 0, 1).reshape(l * b, c)


def _from_time_major(t, b):
    lb, c = t.shape
    return jnp.swapaxes(t.reshape(lb // b, b, c), 0, 1)


def kernel(x_prompt, x_sample, c_prompt, c_sample, state_conv_a, state_lru, state_conv_b, state_delta, state_s5_re, state_s5_im, norm_w, mod_w, mod_b, ab_in_w, ab_out_w, conv_a_w, conv_a_b, lru_gx_w, lru_gx_b, lru_ga_w, lru_ga_b, lru_a_param, conv_b_w, gdn_a_log, gdn_dt_bias, gdn_norm_w, c_in_w, c_out_w, s5_a_re, s5_a_im, s5_b_re, s5_b_im, s5_c_re, s5_c_im, s5_d, s5_log_dt, glu_w, glu_b, final_norm_w):
    nbp = x_prompt.shape[0]
    nbs = x_sample.shape[0]
    nt = GDN_BLOCK

    mods = _mod_all(jnp.concatenate([c_prompt, c_sample], axis=0), mod_w, mod_b)
    abr, abi, bbr, bbi = _s5_prep(s5_a_re, s5_a_im, s5_log_dt, s5_b_re, s5_b_im)
    fnw = final_norm_w.reshape(1, D_MODEL)
    w_in_all = jnp.swapaxes(ab_in_w, 1, 2).astype(BF16)

    xp = x_prompt
    xs = x_sample.reshape(nbs, D_MODEL)
    p_states = [[] for _ in range(6)]
    s_states = [[] for _ in range(6)]
    s_delta = None
    for i in range(DEPTH):
        j = i // 2
        normw = norm_w[i].reshape(1, D_MODEL)
        mod_p = mods[i, 0:nbp]
        mod_s = mods[i, nbp:nbp + nbs]
        if i % 2 == 0:
            w = _ab_weights(j, w_in_all, ab_out_w, conv_a_w, conv_a_b, lru_gx_w, lru_gx_b, lru_ga_w, lru_ga_b,
                            lru_a_param, conv_b_w, gdn_a_log, gdn_dt_bias, gdn_norm_w)
            xp, ca, lh, cb, ds = _ab_prompt(xp, mod_p, normw, w, nbp, nt)
            for lst, val in zip(p_states[:4], (_from_time_major(ca, nbp), lh, _from_time_major(cb, nbp), ds)):
                lst.append(val)
            xs, ca, lh, cb, s_delta = _ab_sample(xs, mod_s, normw, w, _to_time_major(state_conv_a[j]),
                                                 state_lru[j], _to_time_major(state_conv_b[j]), state_delta, j,
                                                 s_delta)
            for lst, val in zip(s_states[:3], (_from_time_major(ca, nbs), lh, _from_time_major(cb, nbs))):
                lst.append(val)
        else:
            w = dict(
                w_in=c_in_w[j].astype(BF16), bre=_block_diag_in(bbr[j]).astype(BF16),
                bim=_block_diag_in(bbi[j]).astype(BF16), cre=_block_diag_out(s5_c_re[j]).astype(BF16),
                cim=_block_diag_out(s5_c_im[j]).astype(BF16), abr=abr[j].reshape(1, NS),
                abi=abi[j].reshape(1, NS), d=s5_d[j].reshape(1, D_MODEL), gluw=glu_w[j].astype(BF16),
                glub=glu_b[j].reshape(1, D_MODEL), w_out=c_out_w[j].astype(BF16), fnw=fnw)
            last = i == DEPTH - 1
            xp, sr, si = _s5_layer(xp, mod_p, normw, w, nbp, nt, None, last, batch_major_out=last)
            p_states[4].append(sr.reshape(nbp, G_C, P_C))
            p_states[5].append(si.reshape(nbp, G_C, P_C))
            xs, sr, si = _s5_layer(xs, mod_s, normw, w, nbs, 1,
                                   (state_s5_re[j].reshape(nbs, NS), state_s5_im[j].reshape(nbs, NS)), last)
            s_states[4].append(sr.reshape(nbs, G_C, P_C))
            s_states[5].append(si.reshape(nbs, G_C, P_C))
    y_prompt = xp
    y_sample = xs.reshape(nbs, 1, D_MODEL)
    stack = lambda lists: tuple(jnp.stack(l) for l in lists)
    s_out = stack(s_states[:3]) + (s_delta,) + stack(s_states[4:])
    return (y_prompt, y_sample) + stack(p_states) + s_out
```
